```python
import math
import jax, jax.numpy as jnp
from jax import lax
import numpy as np


D_MODEL = 1024
BATCH = 4
SEQ = 4096
DEPTH = 2

HEAD_DIM = 64
N_DIFF_HEADS = (D_MODEL // 2) // (2 * HEAD_DIM)
N_DIL_HEADS = (D_MODEL // 2) // HEAD_DIM
DIFF_WIDTH = N_DIFF_HEADS * 2 * HEAD_DIM
DIL_WIDTH = N_DIL_HEADS * HEAD_DIM
MIX_WIDTH = DIFF_WIDTH + DIL_WIDTH
D_FF = 256 * ((8 * D_MODEL // 3 + 255) // 256)
DILATED_CONFIGS = ((128, 1), (512, 4), (2048, 16))
NUM_BUCKETS = 32
MAX_DISTANCE = 2048
N_BIAS_COLS = 2 * N_DIFF_HEADS + N_DIL_HEADS
Q_BLOCK = 128
EPS = 1e-6

kernel_name = 'hybrid_diffattn_dilated_macaron'


def rmsnorm(x, g):
    x32 = x.astype(jnp.float32)
    y = x32 * lax.rsqrt(jnp.mean(x32 * x32, axis=-1, keepdims=True) + EPS)
    return (y * g.astype(jnp.float32)).astype(x.dtype)


def swiglu(x, w_gate, w_up, w_down):
    return (jax.nn.silu(x @ w_gate) * (x @ w_up)) @ w_down


def t5_bucket(dist):
    n = jnp.maximum(dist, 0)
    max_exact = NUM_BUCKETS // 2
    nf = jnp.maximum(n, 1).astype(jnp.float32)
    large = max_exact + (jnp.log(nf / max_exact) / math.log(MAX_DISTANCE / max_exact)
                         * (NUM_BUCKETS - max_exact)).astype(jnp.int32)
    large = jnp.minimum(large, NUM_BUCKETS - 1)
    return jnp.where(n < max_exact, n, large)


def lambda_init(layer):
    return 0.8 - 0.6 * math.exp(-0.3 * layer)


def diff_attention(q, k, v, lam, bias_table):
    B, H, _, S, Dh = q.shape
    nb = S // Q_BLOCK
    scale = Dh ** -0.5
    k_pos = jnp.arange(S)
    q_blocks = q.reshape(B, H, 2, nb, Q_BLOCK, Dh).transpose(3, 0, 1, 2, 4, 5)
    v32 = v.astype(jnp.float32)
    table = bias_table.astype(jnp.float32)

    def one_block(args):
        q_blk, start = args
        s = jnp.einsum('bhmqd,bhmkd->bhmqk', q_blk, k).astype(jnp.float32) * scale
        dist = (start + jnp.arange(Q_BLOCK))[:, None] - k_pos[None, :]
        bias = table[t5_bucket(dist)].transpose(2, 0, 1).reshape(H, 2, Q_BLOCK, S)
        s = jnp.where(dist >= 0, s + bias, -jnp.inf)
        p = jax.nn.softmax(s, axis=-1)
        a = p[:, :, 0] - lam * p[:, :, 1]
        return jnp.einsum('bhqk,bhkd->bhqd', a, v32)

    out = lax.map(one_block, (q_blocks, jnp.arange(nb) * Q_BLOCK))
    return out.transpose(1, 2, 0, 3, 4).reshape(B, H, S, 2 * Dh)


def dilated_branch(q, k, v, window, dil, bias_table):
    B, H, S, Dh = q.shape
    W = window // dil
    Ld = S // dil
    nb = -(-Ld // W)
    Lp = nb * W
    scale = Dh ** -0.5

    def regroup(t):
        t = t.reshape(B, H, Ld, dil, Dh).transpose(0, 1, 3, 2, 4)
        t = jnp.pad(t, ((0, 0), (0, 0), (0, 0), (0, Lp - Ld), (0, 0)))
        return t.reshape(B, H, dil, nb, W, Dh)

    def with_prev(t):
        prev = jnp.pad(t, ((0, 0), (0, 0), (0, 0), (1, 0), (0, 0), (0, 0)))[:, :, :, :nb]
        return jnp.concatenate([prev, t], axis=4)

    qg = regroup(q)
    kw = with_prev(regroup(k))
    vw = with_prev(regroup(v))
    s = jnp.einsum('bhrnqd,bhrnkd->bhrnqk', qg, kw).astype(jnp.float32) * scale
    a_idx = jnp.arange(W)[:, None]
    c_idx = jnp.arange(2 * W)[None, :]
    dist = a_idx + W - c_idx
    key_idx = jnp.arange(nb)[:, None, None] * W - W + c_idx[None]
    valid = ((dist >= 0) & (dist <= W))[None] & (key_idx >= 0)
    bias = bias_table.astype(jnp.float32)[t5_bucket(dist * dil)].transpose(2, 0, 1)
    s = jnp.where(valid, s + bias[None, :, None, None], -jnp.inf)
    m = jnp.max(s, axis=-1)
    e = jnp.exp(s - m[..., None])
    l = jnp.sum(e, axis=-1)
    o = jnp.einsum('bhrnqk,bhrnkd->bhrnqd', e, vw.astype(jnp.float32)) / l[..., None]

    def restore(t):
        tail = t.shape[5:]
        t = t.reshape((B, H, dil, Lp) + tail)[:, :, :, :Ld]
        t = jnp.moveaxis(t, 2, 3)
        return t.reshape((B, H, S) + tail)

    return restore(o), restore(m), restore(l)


def dilated_attention(q, k, v, bias_table):
    outs = [dilated_branch(q, k, v, w, d, bias_table) for (w, d) in DILATED_CONFIGS]
    m_all = jnp.stack([br[1] for br in outs])
    l_all = jnp.stack([br[2] for br in outs])
    o_all = jnp.stack([br[0] for br in outs])
    wts = l_all * jnp.exp(m_all - jnp.max(m_all, axis=0, keepdims=True))
    wts = wts / jnp.sum(wts, axis=0, keepdims=True)
    return jnp.sum(wts[..., None] * o_all, axis=0)


def token_mixer(h, w_in, w_out, lq1, lk1, lq2, lk2, subln_gain, rel_bias, layer):
    B, S, _ = h.shape
    proj = h @ w_in
    qa, ka, va, qb, kb, vb = jnp.split(
        proj, [DIFF_WIDTH, 2 * DIFF_WIDTH, 3 * DIFF_WIDTH,
               3 * DIFF_WIDTH + DIL_WIDTH, 3 * DIFF_WIDTH + 2 * DIL_WIDTH], axis=-1)
    qa = qa.reshape(B, S, N_DIFF_HEADS, 2, HEAD_DIM).transpose(0, 2, 3, 1, 4)
    ka = ka.reshape(B, S, N_DIFF_HEADS, 2, HEAD_DIM).transpose(0, 2, 3, 1, 4)
    va = va.reshape(B, S, N_DIFF_HEADS, 2 * HEAD_DIM).transpose(0, 2, 1, 3)
    lam0 = lambda_init(layer)
    f32 = jnp.float32
    lam = (jnp.exp(jnp.sum(lq1.astype(f32) * lk1.astype(f32)))
           - jnp.exp(jnp.sum(lq2.astype(f32) * lk2.astype(f32))) + lam0)
    oa = diff_attention(qa, ka, va, lam, rel_bias[:, :2 * N_DIFF_HEADS])
    oa = rmsnorm(oa, subln_gain) * (1.0 - lam0)
    oa = oa.transpose(0, 2, 1, 3).reshape(B, S, DIFF_WIDTH)
    qb = qb.reshape(B, S, N_DIL_HEADS, HEAD_DIM).transpose(0, 2, 1, 3)
    kb = kb.reshape(B, S, N_DIL_HEADS, HEAD_DIM).transpose(0, 2, 1, 3)
    vb = vb.reshape(B, S, N_DIL_HEADS, HEAD_DIM).transpose(0, 2, 1, 3)
    ob = dilated_attention(qb, kb, vb, rel_bias[:, 2 * N_DIFF_HEADS:])
    ob = ob.transpose(0, 2, 1, 3).reshape(B, S, DIL_WIDTH)
    mixed = jnp.concatenate([oa, ob], axis=-1).astype(h.dtype)
    return mixed @ w_out


def setup_inputs(seed: int = 0) -> dict:
    key = jax.random.key(seed)
    ks = jax.random.split(key, 20)
    f32 = jnp.float32
    nrm = lambda k, shape, scale: jax.random.normal(k, shape, f32) * scale
    gain = lambda k, shape: 1.0 + 0.05 * jax.random.normal(k, shape, f32)
    return {
        'x': jax.random.normal(ks[0], (BATCH, SEQ, D_MODEL), f32),
        'ffn1_norm': gain(ks[1], (DEPTH, D_MODEL)),
        'ffn1_w_gate': nrm(ks[2], (DEPTH, D_MODEL, D_FF), D_MODEL ** -0.5),
        'ffn1_w_up': nrm(ks[3], (DEPTH, D_MODEL, D_FF), D_MODEL ** -0.5),
        'ffn1_w_down': nrm(ks[4], (DEPTH, D_FF, D_MODEL), D_FF ** -0.5),
        'mix_norm': gain(ks[5], (DEPTH, D_MODEL)),
        'w_in': nrm(ks[6], (DEPTH, D_MODEL, 3 * MIX_WIDTH), D_MODEL ** -0.5),
        'lambda_q1': nrm(ks[7], (DEPTH, HEAD_DIM), 0.1),
        'lambda_k1': nrm(ks[8], (DEPTH, HEAD_DIM), 0.1),
        'lambda_q2': nrm(ks[9], (DEPTH, HEAD_DIM), 0.1),
        'lambda_k2': nrm(ks[10], (DEPTH, HEAD_DIM), 0.1),
        'subln_gain': gain(ks[11], (DEPTH, 2 * HEAD_DIM)),
        'w_out': nrm(ks[12], (DEPTH, MIX_WIDTH, D_MODEL), MIX_WIDTH ** -0.5),
        'ffn2_norm': gain(ks[13], (DEPTH, D_MODEL)),
        'ffn2_w_gate': nrm(ks[14], (DEPTH, D_MODEL, D_FF), D_MODEL ** -0.5),
        'ffn2_w_up': nrm(ks[15], (DEPTH, D_MODEL, D_FF), D_MODEL ** -0.5),
        'ffn2_w_down': nrm(ks[16], (DEPTH, D_FF, D_MODEL), D_FF ** -0.5),
        'rel_bias': nrm(ks[17], (NUM_BUCKETS, N_BIAS_COLS), 0.3),
        'final_norm': gain(ks[18], (D_MODEL,)),
    }


def reference(x, ffn1_norm, ffn1_w_gate, ffn1_w_up, ffn1_w_down, mix_norm, w_in,
              lambda_q1, lambda_k1, lambda_q2, lambda_k2, subln_gain, w_out,
              ffn2_norm, ffn2_w_gate, ffn2_w_up, ffn2_w_down, rel_bias, final_norm):
    for layer in range(DEPTH):
        x = x + 0.5 * swiglu(rmsnorm(x, ffn1_norm[layer]),
                             ffn1_w_gate[layer], ffn1_w_up[layer], ffn1_w_down[layer])
        x = x + token_mixer(rmsnorm(x, mix_norm[layer]), w_in[layer], w_out[layer],
                            lambda_q1[layer], lambda_k1[layer], lambda_q2[layer], lambda_k2[layer],
                            subln_gain[layer], rel_bias, layer)
        x = x + 0.5 * swiglu(rmsnorm(x, ffn2_norm[layer]),
                             ffn2_w_gate[layer], ffn2_w_up[layer], ffn2_w_down[layer])
    return rmsnorm(x, final_norm)
```

```python
import functools
import math

import numpy as np
import jax
import jax.numpy as jnp
from jax import lax
from jax.experimental import pallas as pl
from jax.experimental.pallas import tpu as pltpu

HEAD_DIM = 64
N_DIFF_HEADS = 4
N_DIL_HEADS = 8
DIFF_WIDTH = 512
DIL_WIDTH = 512
DILATED_CONFIGS = ((128, 1), (512, 4), (2048, 16))
NUM_BUCKETS = 32
MAX_DISTANCE = 2048
EPS = 1e-6
NEG = -1e30

LANES = 128
DIL_W = 128
ATT_T = 512
N_BIAS_TILES = 5
VMEM_LIMIT = 56 * 1024 * 1024

F32 = jnp.float32
BF16 = jnp.bfloat16


def _bucket_lower_bounds():
    d = np.arange(0, 2 * MAX_DISTANCE + 1)
    max_exact = NUM_BUCKETS // 2
    v = np.log(np.maximum(d, 1) / max_exact) / math.log(MAX_DISTANCE / max_exact) * (NUM_BUCKETS - max_exact)
    b = np.where(d < max_exact, d, np.minimum(max_exact + v.astype(np.int64), NUM_BUCKETS - 1))
    return [int(np.argmax(b == k)) for k in range(NUM_BUCKETS)]


BUCKET_LO = _bucket_lower_bounds()
FAR_DELTA = 4 * ATT_T
assert FAR_DELTA - ATT_T >= BUCKET_LO[-1]


def _lambda_init(layer):
    return 0.8 - 0.6 * math.exp(-0.3 * layer)


def _bias_of_distance(rb_ref, col, dist):
    val = jnp.full(dist.shape, rb_ref[0, col], F32)
    for b in range(1, NUM_BUCKETS):
        val = jnp.where(dist >= BUCKET_LO[b], rb_ref[b, col], val)
    return val


def _diff_bias_kernel(rb_ref, out_ref):
    h = pl.program_id(0)
    di = pl.program_id(1)
    t = ATT_T
    delta = jnp.where(di == N_BIAS_TILES - 1, FAR_DELTA, di * t)
    u = lax.broadcasted_iota(jnp.int32, (8, 2 * t), 1)
    d = u + (delta - t)
    for m in range(2):
        val = _bias_of_distance(rb_ref, 2 * h + m, d)
        val = jnp.where(d < 0, NEG, val)
        x = jnp.broadcast_to(val[0:1, :], (t, 2 * t))
        rolled = pltpu.roll(x, 0, 1, stride=1, stride_axis=0)
        out_ref[0, 0, :, m * t:(m + 1) * t] = rolled[:, t:]


def _diff_bias_tiles(rel_bias):
    t = ATT_T
    return pl.pallas_call(
        _diff_bias_kernel,
        grid=(N_DIFF_HEADS, N_BIAS_TILES),
        in_specs=[pl.BlockSpec(memory_space=pltpu.SMEM)],
        out_specs=pl.BlockSpec((1, 1, t, 2 * t), lambda h, d: (h, d, 0, 0)),
        out_shape=jax.ShapeDtypeStruct((N_DIFF_HEADS, N_BIAS_TILES, t, 2 * t), F32),
        name="diff_bias_tiles",
    )(rel_bias)


def _dil_bias_kernel(rb_ref, out_ref):
    br = pl.program_id(0)
    h = pl.program_id(1)
    w = DIL_W
    dil = jnp.where(br == 0, DILATED_CONFIGS[0][1],
                    jnp.where(br == 1, DILATED_CONFIGS[1][1], DILATED_CONFIGS[2][1]))
    u = lax.broadcasted_iota(jnp.int32, (8, 4 * w), 1)
    dist = 2 * w - u
    val = _bias_of_distance(rb_ref, 2 * N_DIFF_HEADS + h, dist * dil)
    val = jnp.where((dist >= 0) & (dist <= w), val, NEG)
    x = jnp.broadcast_to(val[0:1, :], (w, 4 * w))
    rolled = pltpu.roll(x, 0, 1, stride=1, stride_axis=0)
    out_ref[0, 0] = rolled[:, w:3 * w]


def _dil_bias_tiles(rel_bias):
    w = DIL_W
    nb = len(DILATED_CONFIGS)
    return pl.pallas_call(
        _dil_bias_kernel,
        grid=(nb, N_DIL_HEADS),
        in_specs=[pl.BlockSpec(memory_space=pltpu.SMEM)],
        out_specs=pl.BlockSpec((1, 1, w, 2 * w), lambda b, h: (b, h, 0, 0)),
        out_shape=jax.ShapeDtypeStruct((nb, N_DIL_HEADS, w, 2 * w), F32),
        name="dil_bias_tiles",
    )(rel_bias)


def _rms(x, g):
    return x * lax.rsqrt(jnp.mean(x * x, axis=-1, keepdims=True) + EPS) * g


def _ffn_kernel(x_ref, g_ref, wg_ref, wu_ref, wd_ref, fg_ref, o_ref, h_ref, *, n_chunks, chunk, final):
    x = x_ref[...]
    xn = _rms(x, g_ref[...]).astype(BF16)
    for c in range(n_chunks):
        gate = jnp.dot(xn, wg_ref[c], preferred_element_type=F32)
        up = jnp.dot(xn, wu_ref[c], preferred_element_type=F32)
        h_ref[:, c * chunk:(c + 1) * chunk] = (gate * jax.nn.sigmoid(gate) * up).astype(BF16)
    y = x + 0.5 * jnp.dot(h_ref[...], wd_ref[...], preferred_element_type=F32)
    if final:
        y = _rms(y, fg_ref[...])
    o_ref[...] = y


def _ffn(x, gain, wg, wu, wd, final_gain, *, final, tm=512):
    m, d = x.shape
    n_chunks, _, chunk = wg.shape
    dff = n_chunks * chunk
    const = lambda *shape: pl.BlockSpec(shape, lambda i: (0,) * len(shape), pipeline_mode=pl.Buffered(1))
    return pl.pallas_call(
        functools.partial(_ffn_kernel, n_chunks=n_chunks, chunk=chunk, final=final),
        grid=(m // tm,),
        in_specs=[
            pl.BlockSpec((tm, d), lambda i: (i, 0)),
            const(1, d),
            const(n_chunks, d, chunk),
            const(n_chunks, d, chunk),
            const(dff, d),
            const(1, d),
        ],
        out_specs=pl.BlockSpec((tm, d), lambda i: (i, 0)),
        out_shape=jax.ShapeDtypeStruct((m, d), F32),
        scratch_shapes=[pltpu.VMEM((tm, dff), BF16)],
        compiler_params=pltpu.CompilerParams(
            dimension_semantics=("arbitrary",), vmem_limit_bytes=VMEM_LIMIT),
        name="ffn",
    )(x, gain, wg, wu, wd, final_gain)


def _inproj_kernel(x_ref, g_ref, w_ref, o_ref, *, n_chunks, chunk):
    xn = _rms(x_ref[...], g_ref[...]).astype(BF16)
    for c in range(n_chunks):
        sl = slice(c * chunk, (c + 1) * chunk)
        o_ref[:, sl] = jnp.dot(xn, w_ref[:, sl], preferred_element_type=F32).astype(BF16)


def _inproj(x, gain, w, *, tm=512, chunk=512):
    m, d = x.shape
    n = w.shape[1]
    return pl.pallas_call(
        functools.partial(_inproj_kernel, n_chunks=n // chunk, chunk=chunk),
        grid=(m // tm,),
        in_specs=[
            pl.BlockSpec((tm, d), lambda i: (i, 0)),
            pl.BlockSpec((1, d), lambda i: (0, 0), pipeline_mode=pl.Buffered(1)),
            pl.BlockSpec((d, n), lambda i: (0, 0), pipeline_mode=pl.Buffered(1)),
        ],
        out_specs=pl.BlockSpec((tm, n), lambda i: (i, 0)),
        out_shape=jax.ShapeDtypeStruct((m, n), BF16),
        compiler_params=pltpu.CompilerParams(
            dimension_semantics=("arbitrary",), vmem_limit_bytes=VMEM_LIMIT),
        name="inproj",
    )(x, gain, w)


def _diff_attn_kernel(q_ref, k_ref, v_ref, bias_ref, lq1_ref, lk1_ref, lq2_ref, lk2_ref, sg_ref,
                      o_ref, vt_ref, acc_ref, m_ref, l_ref, *, lam0, n_kv):
    t = ATT_T
    qi = pl.program_id(2)

    @pl.when(qi == 0)
    def _():
        for c in range(n_kv):
            vt_ref[c] = v_ref[0, c * t:(c + 1) * t, :].astype(F32).T.astype(BF16)

    qt = q_ref[0].astype(F32).T
    row = lax.broadcasted_iota(jnp.int32, qt.shape, 0)
    qstack = jnp.concatenate(
        [jnp.where(row < HEAD_DIM, qt, 0.0), jnp.where(row >= HEAD_DIM, qt, 0.0)], axis=1).astype(BF16)

    m_ref[...] = jnp.full(m_ref.shape, NEG, F32)
    l_ref[...] = jnp.zeros(l_ref.shape, F32)
    acc_ref[...] = jnp.zeros(acc_ref.shape, F32)

    def body(j, carry):
        kblk = k_ref[0, pl.ds(pl.multiple_of(j * t, t), t), :]
        s = jnp.dot(kblk, qstack, preferred_element_type=F32)
        s = s + bias_ref[0, jnp.minimum(qi - j, N_BIAS_TILES - 1)]
        m_old = m_ref[...]
        m_new = jnp.maximum(m_old, jnp.max(s, axis=0, keepdims=True))
        alpha = jnp.exp(m_old - m_new)
        e = jnp.exp(s - m_new)
        l_ref[...] = alpha * l_ref[...] + jnp.sum(e, axis=0, keepdims=True)
        acc_ref[...] = alpha * acc_ref[...] + jnp.dot(vt_ref[j], e.astype(BF16), preferred_element_type=F32)
        m_ref[...] = m_new
        return carry

    lax.fori_loop(0, qi + 1, body, 0)

    lam = (jnp.exp(jnp.sum(lq1_ref[...] * lk1_ref[...])) - jnp.exp(jnp.sum(lq2_ref[...] * lk2_ref[...])) + lam0)
    o = acc_ref[...] / l_ref[...]
    o = o[:, :t] - lam * o[:, t:]
    y = o * lax.rsqrt(jnp.mean(o * o, axis=0, keepdims=True) + EPS) * sg_ref[...]
    y = y * (1.0 - lam0)
    o_ref[0] = y.T.astype(BF16)


def _diff_attn(qkv, bias_tiles, lq1, lk1, lq2, lk2, sub_gain, *, lam0):
    b, s, _ = qkv.shape
    t = ATT_T
    n_kv = s // t
    hw = 2 * HEAD_DIM
    small = lambda shape: pl.BlockSpec(shape, lambda h, bb, i: (0,) * len(shape))
    return pl.pallas_call(
        functools.partial(_diff_attn_kernel, lam0=lam0, n_kv=n_kv),
        grid=(N_DIFF_HEADS, b, n_kv),
        in_specs=[
            pl.BlockSpec((1, t, hw), lambda h, bb, i: (bb, i, h)),
            pl.BlockSpec((1, s, hw), lambda h, bb, i: (bb, 0, N_DIFF_HEADS + h)),
            pl.BlockSpec((1, s, hw), lambda h, bb, i: (bb, 0, 2 * N_DIFF_HEADS + h)),
            pl.BlockSpec((1, N_BIAS_TILES, t, 2 * t), lambda h, bb, i: (h, 0, 0, 0),
                         pipeline_mode=pl.Buffered(1)),
            small((1, HEAD_DIM)), small((1, HEAD_DIM)), small((1, HEAD_DIM)), small((1, HEAD_DIM)),
            small((hw, 1)),
        ],
        out_specs=pl.BlockSpec((1, t, hw), lambda h, bb, i: (bb, i, h)),
        out_shape=jax.ShapeDtypeStruct((b, s, DIFF_WIDTH), BF16),
        scratch_shapes=[
            pltpu.VMEM((n_kv, hw, t), BF16),
            pltpu.VMEM((hw, 2 * t), F32),
            pltpu.VMEM((1, 2 * t), F32),
            pltpu.VMEM((1, 2 * t), F32),
        ],
        compiler_params=pltpu.CompilerParams(
            dimension_semantics=("arbitrary", "arbitrary", "arbitrary"), vmem_limit_bytes=VMEM_LIMIT),
        name="diff_attn",
    )(qkv, qkv, qkv, bias_tiles, lq1, lk1, lq2, lk2, sub_gain)


def _dilated_kernel(q_ref, kc_ref, kp_ref, vc_ref, vp_ref, bias_ref, o_ref, lw_ref, *, tb):
    w = DIL_W
    first = pl.program_id(2) == 0
    lane = lax.broadcasted_iota(jnp.int32, (w, LANES), 1)
    col = lax.broadcasted_iota(jnp.int32, (w, 2 * w), 1)
    low = lane < HEAD_DIM
    for a in range(tb // w):
        lw_blk = jnp.zeros((w, LANES), F32)
        for p in range(N_DIL_HEADS // 2):
            ls = slice(p * LANES, (p + 1) * LANES)
            qp = q_ref[0, a * w:(a + 1) * w, ls]
            if a == 0:
                kk = jnp.concatenate([kp_ref[0, :, ls], kc_ref[0, 0:w, ls]], axis=0)
                vv = jnp.concatenate([vp_ref[0, :, ls], vc_ref[0, 0:w, ls]], axis=0)
            else:
                kk = kc_ref[0, (a - 1) * w:(a + 1) * w, ls]
                vv = vc_ref[0, (a - 1) * w:(a + 1) * w, ls]
            outs = []
            for hh in range(2):
                qm = jnp.where(low if hh == 0 else jnp.logical_not(low), qp, jnp.zeros_like(qp))
                s = lax.dot_general(qm, kk, (((1,), (1,)), ((), ())), preferred_element_type=F32)
                s = s + bias_ref[2 * p + hh]
                if a == 0:
                    s = jnp.where(first & (col < w), NEG, s)
                m = jnp.max(s, axis=1, keepdims=True)
                e = jnp.exp(s - m)
                l = jnp.sum(e, axis=1, keepdims=True)
                outs.append(jnp.dot(e.astype(BF16), vv, preferred_element_type=F32) / l)
                lw_blk = jnp.where(lane == 2 * p + hh, m + jnp.log(l), lw_blk)
            o_ref[0, a * w:(a + 1) * w, ls] = jnp.where(low, outs[0], outs[1])
        lw_ref[0, a * w:(a + 1) * w, :] = lw_blk


def _dilated_branch(qkv, bias, dil):
    b, s, n = qkv.shape
    w = DIL_W
    ld = s // dil
    tb = min(ld, 512)
    nblk = n // DIL_WIDTH
    qcol = 3 * DIFF_WIDTH // DIL_WIDTH
    view = qkv.reshape(b, ld, dil * n)
    cur = lambda off: pl.BlockSpec((1, tb, DIL_WIDTH), lambda bb, r, i: (bb, i, r * nblk + qcol + off))
    prev = lambda off: pl.BlockSpec(
        (1, w, DIL_WIDTH), lambda bb, r, i: (bb, jnp.maximum(i * (tb // w) - 1, 0), r * nblk + qcol + off))
    o, lw = pl.pallas_call(
        functools.partial(_dilated_kernel, tb=tb),
        grid=(b, dil, ld // tb),
        in_specs=[cur(0), cur(1), prev(1), cur(2), prev(2),
                  pl.BlockSpec((N_DIL_HEADS, w, 2 * w), lambda bb, r, i: (0, 0, 0))],
        out_specs=[pl.BlockSpec((1, tb, DIL_WIDTH), lambda bb, r, i: (bb, i, r)),
                   pl.BlockSpec((1, tb, LANES), lambda bb, r, i: (bb, i, r))],
        out_shape=[jax.ShapeDtypeStruct((b, ld, dil * DIL_WIDTH), F32),
                   jax.ShapeDtypeStruct((b, ld, dil * LANES), F32)],
        compiler_params=pltpu.CompilerParams(
            dimension_semantics=("arbitrary", "arbitrary", "arbitrary"), vmem_limit_bytes=VMEM_LIMIT),
        name=f"dilated_d{dil}",
    )(view, view, view, view, view, bias)
    return o.reshape(b, s, DIL_WIDTH), lw.reshape(b, s, LANES)


def _outproj_kernel(x_ref, oa_ref, o1_ref, o2_ref, o3_ref, l1_ref, l2_ref, l3_ref, w_ref, o_ref):
    lws = [l1_ref[...], l2_ref[...], l3_ref[...]]
    top = jnp.maximum(jnp.maximum(lws[0], lws[1]), lws[2])
    wts = [jnp.exp(v - top) for v in lws]
    den = wts[0] + wts[1] + wts[2]
    fr = [v / den for v in wts]
    obs = [o1_ref, o2_ref, o3_ref]
    tm = x_ref.shape[0]
    low = lax.broadcasted_iota(jnp.int32, (tm, LANES), 1) < HEAD_DIM
    parts = []
    for p in range(N_DIL_HEADS // 2):
        ls = slice(p * LANES, (p + 1) * LANES)
        acc = jnp.zeros((tm, LANES), F32)
        for b in range(3):
            f = jnp.where(low, fr[b][:, 2 * p:2 * p + 1], fr[b][:, 2 * p + 1:2 * p + 2])
            acc = acc + f * obs[b][:, ls]
        parts.append(acc)
    ob = jnp.concatenate(parts, axis=1).astype(BF16)
    y = jnp.dot(oa_ref[...], w_ref[0:DIFF_WIDTH, :], preferred_element_type=F32)
    y = y + jnp.dot(ob, w_ref[DIFF_WIDTH:, :], preferred_element_type=F32)
    o_ref[...] = x_ref[...] + y


def _outproj(x, oa, obs, lws, w, *, tm=512):
    m, d = x.shape
    row = lambda n: pl.BlockSpec((tm, n), lambda i: (i, 0))
    return pl.pallas_call(
        _outproj_kernel,
        grid=(m // tm,),
        in_specs=[row(d), row(DIFF_WIDTH), row(DIL_WIDTH), row(DIL_WIDTH), row(DIL_WIDTH),
                  row(LANES), row(LANES), row(LANES),
                  pl.BlockSpec(w.shape, lambda i: (0, 0), pipeline_mode=pl.Buffered(1))],
        out_specs=row(d),
        out_shape=jax.ShapeDtypeStruct((m, d), F32),
        compiler_params=pltpu.CompilerParams(
            dimension_semantics=("arbitrary",), vmem_limit_bytes=VMEM_LIMIT),
        name="outproj",
    )(x, oa, *obs, *lws, w)


def _chunked(w, chunk=256):
    d, f = w.shape
    return w.astype(BF16).reshape(d, f // chunk, chunk).transpose(1, 0, 2)


def kernel(x, ffn1_norm, ffn1_w_gate, ffn1_w_up, ffn1_w_down, mix_norm, w_in, lambda_q1, lambda_k1,
           lambda_q2, lambda_k2, subln_gain, w_out, ffn2_norm, ffn2_w_gate, ffn2_w_up, ffn2_w_down,
           rel_bias, final_norm):
    b, s, d = x.shape
    depth = w_in.shape[0]
    m = b * s
    diff_bias = _diff_bias_tiles(rel_bias)
    dil_bias = _dil_bias_tiles(rel_bias)
    qscale = jnp.concatenate([
        jnp.full((DIFF_WIDTH,), HEAD_DIM ** -0.5, F32), jnp.ones((2 * DIFF_WIDTH,), F32),
        jnp.full((DIL_WIDTH,), HEAD_DIM ** -0.5, F32), jnp.ones((2 * DIL_WIDTH,), F32)])
    h = x.reshape(m, d)
    for layer in range(depth):
        row = lambda v: v[layer].reshape(1, -1)
        h = _ffn(h, row(ffn1_norm), _chunked(ffn1_w_gate[layer]), _chunked(ffn1_w_up[layer]),
                 ffn1_w_down[layer].astype(BF16), row(ffn1_norm), final=False)
        qkv = _inproj(h, row(mix_norm), (w_in[layer] * qscale).astype(BF16)).reshape(b, s, -1)
        oa = _diff_attn(qkv, diff_bias, row(lambda_q1), row(lambda_k1), row(lambda_q2), row(lambda_k2),
                        subln_gain[layer].reshape(-1, 1), lam0=_lambda_init(layer))
        obs, lws = [], []
        for br, (_, dil) in enumerate(DILATED_CONFIGS):
            o_b, lw_b = _dilated_branch(qkv, dil_bias[br], dil)
            obs.append(o_b.reshape(m, DIL_WIDTH))
            lws.append(lw_b.reshape(m, LANES))
        h = _outproj(h, oa.reshape(m, DIFF_WIDTH), obs, lws, w_out[layer].astype(BF16))
        last = layer == depth - 1
        h = _ffn(h, row(ffn2_norm), _chunked(ffn2_w_gate[layer]), _chunked(ffn2_w_up[layer]),
                 ffn2_w_down[layer].astype(BF16), final_norm.reshape(1, -1), final=last)
    return h.reshape(b, s, d)
```

```python
import functools
import math

import numpy as np
import jax
import jax.numpy as jnp
from jax import lax
from jax.experimental import pallas as pl
from jax.experimental.pallas import tpu as pltpu

HEAD_DIM = 64
N_DIFF_HEADS = 4
N_DIL_HEADS = 8
DIFF_WIDTH = 512
DIL_WIDTH = 512
DILATED_CONFIGS = ((128, 1), (512, 4), (2048, 16))
NUM_BUCKETS = 32
MAX_DISTANCE = 2048
EPS = 1e-6
NEG = -1e30

LANES = 128
DIL_W = 128
ATT_TQ = 512
ATT_TK = 512
ATT_R = ATT_TQ // ATT_TK
ATT_CW = 256
VMEM_LIMIT = 56 * 1024 * 1024
LOG2E = math.log2(math.e)

F32 = jnp.float32
BF16 = jnp.bfloat16


def _bucket_lower_bounds():
    d = np.arange(0, 2 * MAX_DISTANCE + 1)
    max_exact = NUM_BUCKETS // 2
    v = np.log(np.maximum(d, 1) / max_exact) / math.log(MAX_DISTANCE / max_exact) * (NUM_BUCKETS - max_exact)
    b = np.where(d < max_exact, d, np.minimum(max_exact + v.astype(np.int64), NUM_BUCKETS - 1))
    return [int(np.argmax(b == k)) for k in range(NUM_BUCKETS)]


BUCKET_LO = _bucket_lower_bounds()
N_BIAS_TILES = ATT_R + -(-(BUCKET_LO[-1] + ATT_TK - 1) // ATT_TK)
BIAS_ROLL_W = pl.next_power_of_2(ATT_TQ + ATT_TK)


def _lambda_init(layer):
    return 0.8 - 0.6 * math.exp(-0.3 * layer)


def _bias_of_distance(rb_ref, col, dist):
    val = jnp.full(dist.shape, rb_ref[0, col], F32)
    for b in range(1, NUM_BUCKETS):
        val = jnp.where(dist >= BUCKET_LO[b], rb_ref[b, col], val)
    return val


def _diff_bias_kernel(rb_ref, out_ref):
    h = pl.program_id(0)
    delta = (pl.program_id(1) - (ATT_R - 1)) * ATT_TK
    tq, tk = ATT_TQ, ATT_TK
    u = lax.broadcasted_iota(jnp.int32, (8, BIAS_ROLL_W), 1)
    d = u + (delta - tk)
    for m in range(2):
        val = _bias_of_distance(rb_ref, 2 * h + m, d) * LOG2E
        val = jnp.where(d < 0, NEG, val)
        x = jnp.broadcast_to(val[0:1, :], (tk, BIAS_ROLL_W))
        rolled = pltpu.roll(x, 0, 1, stride=1, stride_axis=0)
        out_ref[0, 0, :, m * tq:(m + 1) * tq] = rolled[:, tk:tk + tq]


def _diff_bias_tiles(rel_bias):
    tq, tk = ATT_TQ, ATT_TK
    return pl.pallas_call(
        _diff_bias_kernel,
        grid=(N_DIFF_HEADS, N_BIAS_TILES),
        in_specs=[pl.BlockSpec(memory_space=pltpu.SMEM)],
        out_specs=pl.BlockSpec((1, 1, tk, 2 * tq), lambda h, d: (h, d, 0, 0)),
        out_shape=jax.ShapeDtypeStruct((N_DIFF_HEADS, N_BIAS_TILES, tk, 2 * tq), F32),
        name="diff_bias_tiles",
    )(rel_bias)


def _dil_bias_kernel(rb_ref, out_ref):
    br = pl.program_id(0)
    h = pl.program_id(1)
    w = DIL_W
    dil = jnp.where(br == 0, DILATED_CONFIGS[0][1],
                    jnp.where(br == 1, DILATED_CONFIGS[1][1], DILATED_CONFIGS[2][1]))
    u = lax.broadcasted_iota(jnp.int32, (8, 4 * w), 1)
    dist = 2 * w - u
    val = _bias_of_distance(rb_ref, 2 * N_DIFF_HEADS + h, dist * dil)
    val = jnp.where((dist >= 0) & (dist <= w), val, NEG)
    x = jnp.broadcast_to(val[0:1, :], (w, 4 * w))
    rolled = pltpu.roll(x, 0, 1, stride=1, stride_axis=0)
    out_ref[0, 0] = rolled[:, w:3 * w]


def _dil_bias_tiles(rel_bias):
    w = DIL_W
    nb = len(DILATED_CONFIGS)
    return pl.pallas_call(
        _dil_bias_kernel,
        grid=(nb, N_DIL_HEADS),
        in_specs=[pl.BlockSpec(memory_space=pltpu.SMEM)],
        out_specs=pl.BlockSpec((1, 1, w, 2 * w), lambda b, h: (b, h, 0, 0)),
        out_shape=jax.ShapeDtypeStruct((nb, N_DIL_HEADS, w, 2 * w), F32),
        name="dil_bias_tiles",
    )(rel_bias)


def _rms(x, g):
    return x * lax.rsqrt(jnp.mean(x * x, axis=-1, keepdims=True) + EPS) * g


def _ffn_kernel(x_ref, g_ref, wg_ref, wu_ref, wd_ref, fg_ref, o_ref, h_ref, *, n_chunks, chunk, final):
    x = x_ref[...]
    xn = _rms(x, g_ref[...]).astype(BF16)
    for c in range(n_chunks):
        gate = jnp.dot(xn, wg_ref[c], preferred_element_type=F32)
        up = jnp.dot(xn, wu_ref[c], preferred_element_type=F32)
        h_ref[:, c * chunk:(c + 1) * chunk] = (gate * jax.nn.sigmoid(gate) * up).astype(BF16)
    y = x + 0.5 * jnp.dot(h_ref[...], wd_ref[...], preferred_element_type=F32)
    if final:
        y = _rms(y, fg_ref[...])
    o_ref[...] = y


def _ffn(x, gain, wg, wu, wd, final_gain, *, final, tm=512):
    m, d = x.shape
    n_chunks, _, chunk = wg.shape
    dff = n_chunks * chunk
    const = lambda *shape: pl.BlockSpec(shape, lambda i: (0,) * len(shape), pipeline_mode=pl.Buffered(1))
    return pl.pallas_call(
        functools.partial(_ffn_kernel, n_chunks=n_chunks, chunk=chunk, final=final),
        grid=(m // tm,),
        in_specs=[
            pl.BlockSpec((tm, d), lambda i: (i, 0)),
            const(1, d),
            const(n_chunks, d, chunk),
            const(n_chunks, d, chunk),
            const(dff, d),
            const(1, d),
        ],
        out_specs=pl.BlockSpec((tm, d), lambda i: (i, 0)),
        out_shape=jax.ShapeDtypeStruct((m, d), F32),
        scratch_shapes=[pltpu.VMEM((tm, dff), BF16)],
        compiler_params=pltpu.CompilerParams(
            dimension_semantics=("arbitrary",), vmem_limit_bytes=VMEM_LIMIT),
        name="ffn",
    )(x, gain, wg, wu, wd, final_gain)


def _inproj_kernel(x_ref, g_ref, w_ref, o_ref, *, n_chunks, chunk):
    xn = _rms(x_ref[...], g_ref[...]).astype(BF16)
    for c in range(n_chunks):
        sl = slice(c * chunk, (c + 1) * chunk)
        o_ref[:, sl] = jnp.dot(xn, w_ref[:, sl], preferred_element_type=F32).astype(BF16)


def _inproj(x, gain, w, *, tm=512, chunk=512):
    m, d = x.shape
    n = w.shape[1]
    return pl.pallas_call(
        functools.partial(_inproj_kernel, n_chunks=n // chunk, chunk=chunk),
        grid=(m // tm,),
        in_specs=[
            pl.BlockSpec((tm, d), lambda i: (i, 0)),
            pl.BlockSpec((1, d), lambda i: (0, 0), pipeline_mode=pl.Buffered(1)),
            pl.BlockSpec((d, n), lambda i: (0, 0), pipeline_mode=pl.Buffered(1)),
        ],
        out_specs=pl.BlockSpec((tm, n), lambda i: (i, 0)),
        out_shape=jax.ShapeDtypeStruct((m, n), BF16),
        compiler_params=pltpu.CompilerParams(
            dimension_semantics=("arbitrary",), vmem_limit_bytes=VMEM_LIMIT),
        name="inproj",
    )(x, gain, w)


def _diff_attn_kernel(q_ref, k_ref, v_ref, bias_ref, lq1_ref, lk1_ref, lq2_ref, lk2_ref, sg_ref,
                      o_ref, vt_ref, qs_ref, acc_ref, m_ref, l_ref, al_ref, mxa_ref, mxb_ref,
                      sa_ref, sb_ref, ea_ref, eb_ref, *, lam0, n_kv):
    tq, tk = ATT_TQ, ATT_TK
    qi = pl.program_id(2)
    n_tiles = ATT_R * (qi + 1)

    @pl.when(qi == 0)
    def _():
        for c in range(n_kv):
            vt_ref[c] = v_ref[0, c * tk:(c + 1) * tk, :].astype(F32).T.astype(BF16)

    qt = q_ref[0].astype(F32).T
    row = lax.broadcasted_iota(jnp.int32, qt.shape, 0)
    qs_ref[...] = jnp.concatenate(
        [jnp.where(row < HEAD_DIM, qt, 0.0), jnp.where(row >= HEAD_DIM, qt, 0.0)], axis=1).astype(BF16)

    cw = ATT_CW
    chunks = [slice(c * cw, (c + 1) * cw) for c in range(2 * tq // cw)]

    def score_chunk(j, cols, s_ref, mx_ref):
        kblk = k_ref[0, pl.ds(pl.multiple_of(j * tk, tk), tk), :]
        bi = jnp.minimum(ATT_R * (qi + 1) - 1 - j, N_BIAS_TILES - 1)
        t = jnp.dot(kblk, qs_ref[:, cols], preferred_element_type=F32) + bias_ref[0, bi, :, cols]
        s_ref[:, cols] = t
        mx_ref[:, cols] = jnp.max(t, axis=0, keepdims=True)

    def pv_chunk(j, cols, e_ref):
        acc_ref[:, cols] = (al_ref[:, cols] * acc_ref[:, cols]
                            + jnp.dot(vt_ref[j], e_ref[:, cols], preferred_element_type=F32))

    def softmax_chunk(cols, s_ref, mx_ref, e_ref):
        m_old = m_ref[:, cols]
        m_new = jnp.maximum(m_old, mx_ref[:, cols])
        alpha = jnp.exp2(m_old - m_new)
        e = jnp.exp2(s_ref[:, cols] - m_new)
        l_ref[:, cols] = alpha * l_ref[:, cols] + jnp.sum(e, axis=0, keepdims=True)
        m_ref[:, cols] = m_new
        al_ref[:, cols] = alpha
        e_ref[:, cols] = e.astype(BF16)

    def step(j, cur, nxt):
        jn = jnp.minimum(j + 1, n_tiles - 1)
        jp = jnp.maximum(j - 1, 0)
        for cols in chunks:
            score_chunk(jn, cols, nxt[0], nxt[1])
            pv_chunk(jp, cols, nxt[2])
            softmax_chunk(cols, *cur)

    buf_a = (sa_ref, mxa_ref, ea_ref)
    buf_b = (sb_ref, mxb_ref, eb_ref)
    m_ref[...] = jnp.full(m_ref.shape, NEG, F32)
    l_ref[...] = jnp.zeros(l_ref.shape, F32)
    al_ref[...] = jnp.ones(al_ref.shape, F32)
    acc_ref[...] = jnp.zeros(acc_ref.shape, F32)
    eb_ref[...] = jnp.zeros(eb_ref.shape, BF16)
    for cols in chunks:
        score_chunk(0, cols, sa_ref, mxa_ref)

    def body(j, carry):
        even = j % 2 == 0

        @pl.when(even)
        def _():
            step(j, buf_a, buf_b)

        @pl.when(jnp.logical_not(even))
        def _():
            step(j, buf_b, buf_a)

        return carry

    lax.fori_loop(0, n_tiles, body, 0)

    @pl.when(n_tiles % 2 == 1)
    def _():
        for cols in chunks:
            pv_chunk(n_tiles - 1, cols, ea_ref)

    @pl.when(n_tiles % 2 == 0)
    def _():
        for cols in chunks:
            pv_chunk(n_tiles - 1, cols, eb_ref)

    lam = (jnp.exp(jnp.sum(lq1_ref[...] * lk1_ref[...])) - jnp.exp(jnp.sum(lq2_ref[...] * lk2_ref[...])) + lam0)
    o = acc_ref[...] / l_ref[...]
    o = o[:, :tq] - lam * o[:, tq:]
    y = o * lax.rsqrt(jnp.mean(o * o, axis=0, keepdims=True) + EPS) * sg_ref[...]
    y = y * (1.0 - lam0)
    o_ref[0] = y.T.astype(BF16)


def _diff_attn(qkv, bias_tiles, lq1, lk1, lq2, lk2, sub_gain, *, lam0):
    b, s, _ = qkv.shape
    tq, tk = ATT_TQ, ATT_TK
    hw = 2 * HEAD_DIM
    small = lambda shape: pl.BlockSpec(shape, lambda h, bb, i: (0,) * len(shape))
    return pl.pallas_call(
        functools.partial(_diff_attn_kernel, lam0=lam0, n_kv=s // tk),
        grid=(N_DIFF_HEADS, b, s // tq),
        in_specs=[
            pl.BlockSpec((1, tq, hw), lambda h, bb, i: (bb, i, h)),
            pl.BlockSpec((1, s, hw), lambda h, bb, i: (bb, 0, N_DIFF_HEADS + h)),
            pl.BlockSpec((1, s, hw), lambda h, bb, i: (bb, 0, 2 * N_DIFF_HEADS + h)),
            pl.BlockSpec((1, N_BIAS_TILES, tk, 2 * tq), lambda h, bb, i: (h, 0, 0, 0),
                         pipeline_mode=pl.Buffered(1)),
            small((1, HEAD_DIM)), small((1, HEAD_DIM)), small((1, HEAD_DIM)), small((1, HEAD_DIM)),
            small((hw, 1)),
        ],
        out_specs=pl.BlockSpec((1, tq, hw), lambda h, bb, i: (bb, i, h)),
        out_shape=jax.ShapeDtypeStruct((b, s, DIFF_WIDTH), BF16),
        scratch_shapes=[
            pltpu.VMEM((s // tk, hw, tk), BF16),
            pltpu.VMEM((hw, 2 * tq), BF16),
            pltpu.VMEM((hw, 2 * tq), F32),
            pltpu.VMEM((1, 2 * tq), F32),
            pltpu.VMEM((1, 2 * tq), F32),
            pltpu.VMEM((1, 2 * tq), F32),
            pltpu.VMEM((1, 2 * tq), F32),
            pltpu.VMEM((1, 2 * tq), F32),
            pltpu.VMEM((tk, 2 * tq), F32),
            pltpu.VMEM((tk, 2 * tq), F32),
            pltpu.VMEM((tk, 2 * tq), BF16),
            pltpu.VMEM((tk, 2 * tq), BF16),
        ],
        compiler_params=pltpu.CompilerParams(
            dimension_semantics=("arbitrary", "arbitrary", "arbitrary"), vmem_limit_bytes=VMEM_LIMIT),
        name="diff_attn",
    )(qkv, qkv, qkv, bias_tiles, lq1, lk1, lq2, lk2, sub_gain)


def _dilated_kernel(q_ref, kc_ref, kp_ref, vc_ref, vp_ref, bias_ref, o_ref, lw_ref, *, tb):
    w = DIL_W
    first = pl.program_id(2) == 0
    lane = lax.broadcasted_iota(jnp.int32, (w, LANES), 1)
    col = lax.broadcasted_iota(jnp.int32, (w, 2 * w), 1)
    low = lane < HEAD_DIM
    for a in range(tb // w):
        lw_blk = jnp.zeros((w, LANES), F32)
        for p in range(N_DIL_HEADS // 2):
            ls = slice(p * LANES, (p + 1) * LANES)
            qp = q_ref[0, a * w:(a + 1) * w, ls]
            if a == 0:
                kk = jnp.concatenate([kp_ref[0, :, ls], kc_ref[0, 0:w, ls]], axis=0)
                vv = jnp.concatenate([vp_ref[0, :, ls], vc_ref[0, 0:w, ls]], axis=0)
            else:
                kk = kc_ref[0, (a - 1) * w:(a + 1) * w, ls]
                vv = vc_ref[0, (a - 1) * w:(a + 1) * w, ls]
            outs = []
            for hh in range(2):
                qm = jnp.where(low if hh == 0 else jnp.logical_not(low), qp, jnp.zeros_like(qp))
                s = lax.dot_general(qm, kk, (((1,), (1,)), ((), ())), preferred_element_type=F32)
                s = s + bias_ref[2 * p + hh]
                if a == 0:
                    s = jnp.where(first & (col < w), NEG, s)
                m = jnp.max(s, axis=1, keepdims=True)
                e = jnp.exp(s - m)
                l = jnp.sum(e, axis=1, keepdims=True)
                outs.append(jnp.dot(e.astype(BF16), vv, preferred_element_type=F32) / l)
                lw_blk = jnp.where(lane == 2 * p + hh, m + jnp.log(l), lw_blk)
            o_ref[0, a * w:(a + 1) * w, ls] = jnp.where(low, outs[0], outs[1])
        lw_ref[0, a * w:(a + 1) * w, :] = lw_blk


def _dilated_branch(qkv, bias, dil):
    b, s, n = qkv.shape
    w = DIL_W
    ld = s // dil
    tb = min(ld, 512)
    nblk = n // DIL_WIDTH
    qcol = 3 * DIFF_WIDTH // DIL_WIDTH
    view = qkv.reshape(b, ld, dil * n)
    cur = lambda off: pl.BlockSpec((1, tb, DIL_WIDTH), lambda bb, r, i: (bb, i, r * nblk + qcol + off))
    prev = lambda off: pl.BlockSpec(
        (1, w, DIL_WIDTH), lambda bb, r, i: (bb, jnp.maximum(i * (tb // w) - 1, 0), r * nblk + qcol + off))
    o, lw = pl.pallas_call(
        functools.partial(_dilated_kernel, tb=tb),
        grid=(b, dil, ld // tb),
        in_specs=[cur(0), cur(1), prev(1), cur(2), prev(2),
                  pl.BlockSpec((N_DIL_HEADS, w, 2 * w), lambda bb, r, i: (0, 0, 0))],
        out_specs=[pl.BlockSpec((1, tb, DIL_WIDTH), lambda bb, r, i: (bb, i, r)),
                   pl.BlockSpec((1, tb, LANES), lambda bb, r, i: (bb, i, r))],
        out_shape=[jax.ShapeDtypeStruct((b, ld, dil * DIL_WIDTH), F32),
                   jax.ShapeDtypeStruct((b, ld, dil * LANES), F32)],
        compiler_params=pltpu.CompilerParams(
            dimension_semantics=("arbitrary", "arbitrary", "arbitrary"), vmem_limit_bytes=VMEM_LIMIT),
        name=f"dilated_d{dil}",
    )(view, view, view, view, view, bias)
    return o.reshape(b, s, DIL_WIDTH), lw.reshape(b, s, LANES)


def _outproj_kernel(x_ref, oa_ref, o1_ref, o2_ref, o3_ref, l1_ref, l2_ref, l3_ref, w_ref, o_ref):
    lws = [l1_ref[...], l2_ref[...], l3_ref[...]]
    top = jnp.maximum(jnp.maximum(lws[0], lws[1]), lws[2])
    wts = [jnp.exp(v - top) for v in lws]
    den = wts[0] + wts[1] + wts[2]
    fr = [v / den for v in wts]
    obs = [o1_ref, o2_ref, o3_ref]
    tm = x_ref.shape[0]
    low = lax.broadcasted_iota(jnp.int32, (tm, LANES), 1) < HEAD_DIM
    parts = []
    for p in range(N_DIL_HEADS // 2):
        ls = slice(p * LANES, (p + 1) * LANES)
        acc = jnp.zeros((tm, LANES), F32)
        for b in range(3):
            f = jnp.where(low, fr[b][:, 2 * p:2 * p + 1], fr[b][:, 2 * p + 1:2 * p + 2])
            acc = acc + f * obs[b][:, ls]
        parts.append(acc)
    ob = jnp.concatenate(parts, axis=1).astype(BF16)
    y = jnp.dot(oa_ref[...], w_ref[0:DIFF_WIDTH, :], preferred_element_type=F32)
    y = y + jnp.dot(ob, w_ref[DIFF_WIDTH:, :], preferred_element_type=F32)
    o_ref[...] = x_ref[...] + y


def _outproj(x, oa, obs, lws, w, *, tm=512):
    m, d = x.shape
    row = lambda n: pl.BlockSpec((tm, n), lambda i: (i, 0))
    return pl.pallas_call(
        _outproj_kernel,
        grid=(m // tm,),
        in_specs=[row(d), row(DIFF_WIDTH), row(DIL_WIDTH), row(DIL_WIDTH), row(DIL_WIDTH),
                  row(LANES), row(LANES), row(LANES),
                  pl.BlockSpec(w.shape, lambda i: (0, 0), pipeline_mode=pl.Buffered(1))],
        out_specs=row(d),
        out_shape=jax.ShapeDtypeStruct((m, d), F32),
        compiler_params=pltpu.CompilerParams(
            dimension_semantics=("arbitrary",), vmem_limit_bytes=VMEM_LIMIT),
        name="outproj",
    )(x, oa, *obs, *lws, w)


def _chunked(w, chunk=256):
    d, f = w.shape
    return w.astype(BF16).reshape(d, f // chunk, chunk).transpose(1, 0, 2)


def kernel(x, ffn1_norm, ffn1_w_gate, ffn1_w_up, ffn1_w_down, mix_norm, w_in, lambda_q1, lambda_k1,
           lambda_q2, lambda_k2, subln_gain, w_out, ffn2_norm, ffn2_w_gate, ffn2_w_up, ffn2_w_down,
           rel_bias, final_norm):
    b, s, d = x.shape
    depth = w_in.shape[0]
    m = b * s
    diff_bias = _diff_bias_tiles(rel_bias)
    dil_bias = _dil_bias_tiles(rel_bias)
    qscale = jnp.concatenate([
        jnp.full((DIFF_WIDTH,), HEAD_DIM ** -0.5 * LOG2E, F32), jnp.ones((2 * DIFF_WIDTH,), F32),
        jnp.full((DIL_WIDTH,), HEAD_DIM ** -0.5, F32), jnp.ones((2 * DIL_WIDTH,), F32)])
    h = x.reshape(m, d)
    for layer in range(depth):
        row = lambda v: v[layer].reshape(1, -1)
        h = _ffn(h, row(ffn1_norm), _chunked(ffn1_w_gate[layer]), _chunked(ffn1_w_up[layer]),
                 ffn1_w_down[layer].astype(BF16), row(ffn1_norm), final=False)
        qkv = _inproj(h, row(mix_norm), (w_in[layer] * qscale).astype(BF16)).reshape(b, s, -1)
        oa = _diff_attn(qkv, diff_bias, row(lambda_q1), row(lambda_k1), row(lambda_q2), row(lambda_k2),
                        subln_gain[layer].reshape(-1, 1), lam0=_lambda_init(layer))
        obs, lws = [], []
        for br, (_, dil) in enumerate(DILATED_CONFIGS):
            o_b, lw_b = _dilated_branch(qkv, dil_bias[br], dil)
            obs.append(o_b.reshape(m, DIL_WIDTH))
            lws.append(lw_b.reshape(m, LANES))
        h = _outproj(h, oa.reshape(m, DIFF_WIDTH), obs, lws, w_out[layer].astype(BF16))
        last = layer == depth - 1
        h = _ffn(h, row(ffn2_norm), _chunked(ffn2_w_gate[layer]), _chunked(ffn2_w_up[layer]),
                 ffn2_w_down[layer].astype(BF16), final_norm.reshape(1, -1), final=last)
    return h.reshape(b, s, d)
```

```python
import functools
import math

import numpy as np
import jax
import jax.numpy as jnp
from jax import lax
from jax.experimental import pallas as pl
from jax.experimental.pallas import tpu as pltpu

HEAD_DIM = 64
N_DIFF_HEADS = 4
N_DIL_HEADS = 8
DIFF_WIDTH = 512
DIL_WIDTH = 512
DILATED_CONFIGS = ((128, 1), (512, 4), (2048, 16))
NUM_BUCKETS = 32
MAX_DISTANCE = 2048
EPS = 1e-6
NEG = -1e30

LANES = 128
DIL_W = 128
ATT_TQ = 512
ATT_TK = 256
ATT_R = ATT_TQ // ATT_TK
assert ATT_R % 2 == 0
ATT_PAD = 16
ATT_CW = 256
VMEM_LIMIT = 56 * 1024 * 1024
LOG2E = math.log2(math.e)

F32 = jnp.float32
BF16 = jnp.bfloat16


def _bucket_lower_bounds():
    d = np.arange(0, 2 * MAX_DISTANCE + 1)
    max_exact = NUM_BUCKETS // 2
    v = np.log(np.maximum(d, 1) / max_exact) / math.log(MAX_DISTANCE / max_exact) * (NUM_BUCKETS - max_exact)
    b = np.where(d < max_exact, d, np.minimum(max_exact + v.astype(np.int64), NUM_BUCKETS - 1))
    return [int(np.argmax(b == k)) for k in range(NUM_BUCKETS)]


BUCKET_LO = _bucket_lower_bounds()
N_BIAS_TILES = ATT_R + -(-(BUCKET_LO[-1] + ATT_TK - 1) // ATT_TK)
BIAS_ROLL_W = pl.next_power_of_2(ATT_TQ + ATT_TK)


def _lambda_init(layer):
    return 0.8 - 0.6 * math.exp(-0.3 * layer)


def _bias_of_distance(rb_ref, col, dist):
    val = jnp.full(dist.shape, rb_ref[0, col], F32)
    for b in range(1, NUM_BUCKETS):
        val = jnp.where(dist >= BUCKET_LO[b], rb_ref[b, col], val)
    return val


def _diff_bias_kernel(rb_ref, out_ref):
    h = pl.program_id(0)
    delta = (pl.program_id(1) - (ATT_R - 1)) * ATT_TK
    tq, tk = ATT_TQ, ATT_TK
    u = lax.broadcasted_iota(jnp.int32, (8, BIAS_ROLL_W), 1)
    d = u + (delta - tk)
    for m in range(2):
        val = _bias_of_distance(rb_ref, 2 * h + m, d) * LOG2E
        val = jnp.where(d < 0, NEG, val)
        x = jnp.broadcast_to(val[0:1, :], (tk, BIAS_ROLL_W))
        rolled = pltpu.roll(x, 0, 1, stride=1, stride_axis=0)
        out_ref[0, 0, :, m * tq:(m + 1) * tq] = rolled[:, tk:tk + tq]


def _diff_bias_tiles(rel_bias):
    tq, tk = ATT_TQ, ATT_TK
    return pl.pallas_call(
        _diff_bias_kernel,
        grid=(N_DIFF_HEADS, N_BIAS_TILES),
        in_specs=[pl.BlockSpec(memory_space=pltpu.SMEM)],
        out_specs=pl.BlockSpec((1, 1, tk, 2 * tq), lambda h, d: (h, d, 0, 0)),
        out_shape=jax.ShapeDtypeStruct((N_DIFF_HEADS, N_BIAS_TILES, tk, 2 * tq), F32),
        name="diff_bias_tiles",
    )(rel_bias)


def _dil_bias_kernel(rb_ref, out_ref):
    br = pl.program_id(0)
    h = pl.program_id(1)
    w = DIL_W
    dil = jnp.where(br == 0, DILATED_CONFIGS[0][1],
                    jnp.where(br == 1, DILATED_CONFIGS[1][1], DILATED_CONFIGS[2][1]))
    u = lax.broadcasted_iota(jnp.int32, (8, 4 * w), 1)
    dist = 2 * w - u
    val = _bias_of_distance(rb_ref, 2 * N_DIFF_HEADS + h, dist * dil)
    val = jnp.where((dist >= 0) & (dist <= w), val, NEG)
    x = jnp.broadcast_to(val[0:1, :], (w, 4 * w))
    rolled = pltpu.roll(x, 0, 1, stride=1, stride_axis=0)
    out_ref[0, 0] = rolled[:, w:3 * w]


def _dil_bias_tiles(rel_bias):
    w = DIL_W
    nb = len(DILATED_CONFIGS)
    return pl.pallas_call(
        _dil_bias_kernel,
        grid=(nb, N_DIL_HEADS),
        in_specs=[pl.BlockSpec(memory_space=pltpu.SMEM)],
        out_specs=pl.BlockSpec((1, 1, w, 2 * w), lambda b, h: (b, h, 0, 0)),
        out_shape=jax.ShapeDtypeStruct((nb, N_DIL_HEADS, w, 2 * w), F32),
        name="dil_bias_tiles",
    )(rel_bias)


def _rms(x, g):
    return x * lax.rsqrt(jnp.mean(x * x, axis=-1, keepdims=True) + EPS) * g


def _ffn_kernel(x_ref, g_ref, wg_ref, wu_ref, wd_ref, fg_ref, o_ref, h_ref, *, n_chunks, chunk, final):
    x = x_ref[...]
    xn = _rms(x, g_ref[...]).astype(BF16)
    for c in range(n_chunks):
        gate = jnp.dot(xn, wg_ref[c], preferred_element_type=F32)
        up = jnp.dot(xn, wu_ref[c], preferred_element_type=F32)
        h_ref[:, c * chunk:(c + 1) * chunk] = (gate * jax.nn.sigmoid(gate) * up).astype(BF16)
    y = x + 0.5 * jnp.dot(h_ref[...], wd_ref[...], preferred_element_type=F32)
    if final:
        y = _rms(y, fg_ref[...])
    o_ref[...] = y


def _ffn(x, gain, wg, wu, wd, final_gain, *, final, tm=512):
    m, d = x.shape
    n_chunks, _, chunk = wg.shape
    dff = n_chunks * chunk
    const = lambda *shape: pl.BlockSpec(shape, lambda i: (0,) * len(shape), pipeline_mode=pl.Buffered(1))
    return pl.pallas_call(
        functools.partial(_ffn_kernel, n_chunks=n_chunks, chunk=chunk, final=final),
        grid=(m // tm,),
        in_specs=[
            pl.BlockSpec((tm, d), lambda i: (i, 0)),
            const(1, d),
            const(n_chunks, d, chunk),
            const(n_chunks, d, chunk),
            const(dff, d),
            const(1, d),
        ],
        out_specs=pl.BlockSpec((tm, d), lambda i: (i, 0)),
        out_shape=jax.ShapeDtypeStruct((m, d), F32),
        scratch_shapes=[pltpu.VMEM((tm, dff), BF16)],
        compiler_params=pltpu.CompilerParams(
            dimension_semantics=("arbitrary",), vmem_limit_bytes=VMEM_LIMIT),
        name="ffn",
    )(x, gain, wg, wu, wd, final_gain)


def _inproj_kernel(x_ref, g_ref, w_ref, o_ref, *, n_chunks, chunk):
    xn = _rms(x_ref[...], g_ref[...]).astype(BF16)
    for c in range(n_chunks):
        sl = slice(c * chunk, (c + 1) * chunk)
        o_ref[:, sl] = jnp.dot(xn, w_ref[:, sl], preferred_element_type=F32).astype(BF16)


def _inproj(x, gain, w, *, tm=512, chunk=512):
    m, d = x.shape
    n = w.shape[1]
    return pl.pallas_call(
        functools.partial(_inproj_kernel, n_chunks=n // chunk, chunk=chunk),
        grid=(m // tm,),
        in_specs=[
            pl.BlockSpec((tm, d), lambda i: (i, 0)),
            pl.BlockSpec((1, d), lambda i: (0, 0), pipeline_mode=pl.Buffered(1)),
            pl.BlockSpec((d, n), lambda i: (0, 0), pipeline_mode=pl.Buffered(1)),
        ],
        out_specs=pl.BlockSpec((tm, n), lambda i: (i, 0)),
        out_shape=jax.ShapeDtypeStruct((m, n), BF16),
        compiler_params=pltpu.CompilerParams(
            dimension_semantics=("arbitrary",), vmem_limit_bytes=VMEM_LIMIT),
        name="inproj",
    )(x, gain, w)


def _diff_attn_kernel(q_ref, k_ref, v_ref, bias_ref, lq1_ref, lk1_ref, lq2_ref, lk2_ref, sg_ref,
                      o_ref, vt_ref, qs_ref, acc_ref, m_ref, al_ref, mxa_ref, mxb_ref,
                      sa_ref, sb_ref, ea_ref, eb_ref, *, lam0, n_kv):
    tq, tk = ATT_TQ, ATT_TK
    hw = 2 * HEAD_DIM
    qi = pl.program_id(2)
    n_tiles = ATT_R * (qi + 1)

    @pl.when(qi == 0)
    def _():
        ones_row = (lax.broadcasted_iota(jnp.int32, (ATT_PAD, tk), 0) == 0).astype(BF16)
        for c in range(n_kv):
            vt_ref[c, 0:hw, :] = v_ref[0, c * tk:(c + 1) * tk, :].astype(F32).T.astype(BF16)
            vt_ref[c, hw:, :] = ones_row

    qt = q_ref[0].astype(F32).T
    row = lax.broadcasted_iota(jnp.int32, qt.shape, 0)
    qs_ref[...] = jnp.concatenate(
        [jnp.where(row < HEAD_DIM, qt, 0.0), jnp.where(row >= HEAD_DIM, qt, 0.0)], axis=1).astype(BF16)

    cw = ATT_CW
    chunks = [slice(c * cw, (c + 1) * cw) for c in range(2 * tq // cw)]

    def score_chunk(j, cols, s_ref, mx_ref):
        kblk = k_ref[0, pl.ds(pl.multiple_of(j * tk, tk), tk), :]
        bi = jnp.minimum(ATT_R * (qi + 1) - 1 - j, N_BIAS_TILES - 1)
        t = jnp.dot(kblk, qs_ref[:, cols], preferred_element_type=F32) + bias_ref[0, bi, :, cols]
        s_ref[:, cols] = t
        mx_ref[:, cols] = jnp.max(t, axis=0, keepdims=True)

    def pv_chunk(j, cols, e_ref):
        acc_ref[:, cols] = (al_ref[:, cols] * acc_ref[:, cols]
                            + jnp.dot(vt_ref[j], e_ref[:, cols], preferred_element_type=F32))

    def softmax_chunk(cols, s_ref, mx_ref, e_ref):
        m_old = m_ref[:, cols]
        m_new = jnp.maximum(m_old, mx_ref[:, cols])
        al_ref[:, cols] = jnp.exp2(m_old - m_new)
        m_ref[:, cols] = m_new
        e_ref[:, cols] = jnp.exp2(s_ref[:, cols] - m_new).astype(BF16)

    def phase(j, cur, nxt):
        jn = jnp.minimum(j + 1, n_tiles - 1)
        jp = jnp.maximum(j - 1, 0)
        for cols in chunks:
            score_chunk(jn, cols, nxt[0], nxt[1])
            pv_chunk(jp, cols, nxt[2])
            softmax_chunk(cols, *cur)

    buf_a = (sa_ref, mxa_ref, ea_ref)
    buf_b = (sb_ref, mxb_ref, eb_ref)
    m_ref[...] = jnp.full(m_ref.shape, NEG, F32)
    al_ref[...] = jnp.ones(al_ref.shape, F32)
    acc_ref[...] = jnp.zeros(acc_ref.shape, F32)
    eb_ref[...] = jnp.zeros(eb_ref.shape, BF16)
    for cols in chunks:
        score_chunk(0, cols, sa_ref, mxa_ref)

    def body(i, carry):
        phase(2 * i, buf_a, buf_b)
        phase(2 * i + 1, buf_b, buf_a)
        return carry

    lax.fori_loop(0, n_tiles // 2, body, 0)
    for cols in chunks:
        pv_chunk(n_tiles - 1, cols, eb_ref)

    lam = (jnp.exp(jnp.sum(lq1_ref[...] * lk1_ref[...])) - jnp.exp(jnp.sum(lq2_ref[...] * lk2_ref[...])) + lam0)
    o = acc_ref[0:hw, :] / acc_ref[hw:hw + 1, :]
    o = o[:, :tq] - lam * o[:, tq:]
    y = o * lax.rsqrt(jnp.mean(o * o, axis=0, keepdims=True) + EPS) * sg_ref[...]
    y = y * (1.0 - lam0)
    o_ref[0] = y.T.astype(BF16)


def _diff_attn(qkv, bias_tiles, lq1, lk1, lq2, lk2, sub_gain, *, lam0):
    b, s, _ = qkv.shape
    tq, tk = ATT_TQ, ATT_TK
    hw = 2 * HEAD_DIM
    small = lambda shape: pl.BlockSpec(shape, lambda h, bb, i: (0,) * len(shape))
    return pl.pallas_call(
        functools.partial(_diff_attn_kernel, lam0=lam0, n_kv=s // tk),
        grid=(N_DIFF_HEADS, b, s // tq),
        in_specs=[
            pl.BlockSpec((1, tq, hw), lambda h, bb, i: (bb, i, h)),
            pl.BlockSpec((1, s, hw), lambda h, bb, i: (bb, 0, N_DIFF_HEADS + h)),
            pl.BlockSpec((1, s, hw), lambda h, bb, i: (bb, 0, 2 * N_DIFF_HEADS + h)),
            pl.BlockSpec((1, N_BIAS_TILES, tk, 2 * tq), lambda h, bb, i: (h, 0, 0, 0),
                         pipeline_mode=pl.Buffered(1)),
            small((1, HEAD_DIM)), small((1, HEAD_DIM)), small((1, HEAD_DIM)), small((1, HEAD_DIM)),
            small((hw, 1)),
        ],
        out_specs=pl.BlockSpec((1, tq, hw), lambda h, bb, i: (bb, i, h)),
        out_shape=jax.ShapeDtypeStruct((b, s, DIFF_WIDTH), BF16),
        scratch_shapes=[
            pltpu.VMEM((s // tk, hw + ATT_PAD, tk), BF16),
            pltpu.VMEM((hw, 2 * tq), BF16),
            pltpu.VMEM((hw + ATT_PAD, 2 * tq), F32),
            pltpu.VMEM((1, 2 * tq), F32),
            pltpu.VMEM((1, 2 * tq), F32),
            pltpu.VMEM((1, 2 * tq), F32),
            pltpu.VMEM((1, 2 * tq), F32),
            pltpu.VMEM((tk, 2 * tq), F32),
            pltpu.VMEM((tk, 2 * tq), F32),
            pltpu.VMEM((tk, 2 * tq), BF16),
            pltpu.VMEM((tk, 2 * tq), BF16),
        ],
        compiler_params=pltpu.CompilerParams(
            dimension_semantics=("arbitrary", "arbitrary", "arbitrary"), vmem_limit_bytes=VMEM_LIMIT),
        name="diff_attn",
    )(qkv, qkv, qkv, bias_tiles, lq1, lk1, lq2, lk2, sub_gain)


def _dilated_kernel(q_ref, kc_ref, kp_ref, vc_ref, vp_ref, bias_ref, o_ref, lw_ref, *, tb):
    w = DIL_W
    first = pl.program_id(2) == 0
    lane = lax.broadcasted_iota(jnp.int32, (w, LANES), 1)
    col = lax.broadcasted_iota(jnp.int32, (w, 2 * w), 1)
    low = lane < HEAD_DIM
    for a in range(tb // w):
        lw_blk = jnp.zeros((w, LANES), F32)
        for p in range(N_DIL_HEADS // 2):
            ls = slice(p * LANES, (p + 1) * LANES)
            qp = q_ref[0, a * w:(a + 1) * w, ls]
            if a == 0:
                kk = jnp.concatenate([kp_ref[0, :, ls], kc_ref[0, 0:w, ls]], axis=0)
                vv = jnp.concatenate([vp_ref[0, :, ls], vc_ref[0, 0:w, ls]], axis=0)
            else:
                kk = kc_ref[0, (a - 1) * w:(a + 1) * w, ls]
                vv = vc_ref[0, (a - 1) * w:(a + 1) * w, ls]
            outs = []
            for hh in range(2):
                qm = jnp.where(low if hh == 0 else jnp.logical_not(low), qp, jnp.zeros_like(qp))
                s = lax.dot_general(qm, kk, (((1,), (1,)), ((), ())), preferred_element_type=F32)
                s = s + bias_ref[2 * p + hh]
                if a == 0:
                    s = jnp.where(first & (col < w), NEG, s)
                m = jnp.max(s, axis=1, keepdims=True)
                e = jnp.exp(s - m)
                l = jnp.sum(e, axis=1, keepdims=True)
                outs.append(jnp.dot(e.astype(BF16), vv, preferred_element_type=F32) / l)
                lw_blk = jnp.where(lane == 2 * p + hh, m + jnp.log(l), lw_blk)
            o_ref[0, a * w:(a + 1) * w, ls] = jnp.where(low, outs[0], outs[1])
        lw_ref[0, a * w:(a + 1) * w, :] = lw_blk


def _dilated_branch(qkv, bias, dil):
    b, s, n = qkv.shape
    w = DIL_W
    ld = s // dil
    tb = min(ld, 512)
    nblk = n // DIL_WIDTH
    qcol = 3 * DIFF_WIDTH // DIL_WIDTH
    view = qkv.reshape(b, ld, dil * n)
    cur = lambda off: pl.BlockSpec((1, tb, DIL_WIDTH), lambda bb, r, i: (bb, i, r * nblk + qcol + off))
    prev = lambda off: pl.BlockSpec(
        (1, w, DIL_WIDTH), lambda bb, r, i: (bb, jnp.maximum(i * (tb // w) - 1, 0), r * nblk + qcol + off))
    o, lw = pl.pallas_call(
        functools.partial(_dilated_kernel, tb=tb),
        grid=(b, dil, ld // tb),
        in_specs=[cur(0), cur(1), prev(1), cur(2), prev(2),
                  pl.BlockSpec((N_DIL_HEADS, w, 2 * w), lambda bb, r, i: (0, 0, 0))],
        out_specs=[pl.BlockSpec((1, tb, DIL_WIDTH), lambda bb, r, i: (bb, i, r)),
                   pl.BlockSpec((1, tb, LANES), lambda bb, r, i: (bb, i, r))],
        out_shape=[jax.ShapeDtypeStruct((b, ld, dil * DIL_WIDTH), F32),
                   jax.ShapeDtypeStruct((b, ld, dil * LANES), F32)],
        compiler_params=pltpu.CompilerParams(
            dimension_semantics=("arbitrary", "arbitrary", "arbitrary"), vmem_limit_bytes=VMEM_LIMIT),
        name=f"dilated_d{dil}",
    )(view, view, view, view, view, bias)
    return o.reshape(b, s, DIL_WIDTH), lw.reshape(b, s, LANES)


def _outproj_kernel(x_ref, oa_ref, o1_ref, o2_ref, o3_ref, l1_ref, l2_ref, l3_ref, w_ref, o_ref):
    lws = [l1_ref[...], l2_ref[...], l3_ref[...]]
    top = jnp.maximum(jnp.maximum(lws[0], lws[1]), lws[2])
    wts = [jnp.exp(v - top) for v in lws]
    den = wts[0] + wts[1] + wts[2]
    fr = [v / den for v in wts]
    obs = [o1_ref, o2_ref, o3_ref]
    tm = x_ref.shape[0]
    low = lax.broadcasted_iota(jnp.int32, (tm, LANES), 1) < HEAD_DIM
    parts = []
    for p in range(N_DIL_HEADS // 2):
        ls = slice(p * LANES, (p + 1) * LANES)
        acc = jnp.zeros((tm, LANES), F32)
        for b in range(3):
            f = jnp.where(low, fr[b][:, 2 * p:2 * p + 1], fr[b][:, 2 * p + 1:2 * p + 2])
            acc = acc + f * obs[b][:, ls]
        parts.append(acc)
    ob = jnp.concatenate(parts, axis=1).astype(BF16)
    y = jnp.dot(oa_ref[...], w_ref[0:DIFF_WIDTH, :], preferred_element_type=F32)
    y = y + jnp.dot(ob, w_ref[DIFF_WIDTH:, :], preferred_element_type=F32)
    o_ref[...] = x_ref[...] + y


def _outproj(x, oa, obs, lws, w, *, tm=512):
    m, d = x.shape
    row = lambda n: pl.BlockSpec((tm, n), lambda i: (i, 0))
    return pl.pallas_call(
        _outproj_kernel,
        grid=(m // tm,),
        in_specs=[row(d), row(DIFF_WIDTH), row(DIL_WIDTH), row(DIL_WIDTH), row(DIL_WIDTH),
                  row(LANES), row(LANES), row(LANES),
                  pl.BlockSpec(w.shape, lambda i: (0, 0), pipeline_mode=pl.Buffered(1))],
        out_specs=row(d),
        out_shape=jax.ShapeDtypeStruct((m, d), F32),
        compiler_params=pltpu.CompilerParams(
            dimension_semantics=("arbitrary",), vmem_limit_bytes=VMEM_LIMIT),
        name="outproj",
    )(x, oa, *obs, *lws, w)


def _chunked(w, chunk=256):
    d, f = w.shape
    return w.astype(BF16).reshape(d, f // chunk, chunk).transpose(1, 0, 2)


def kernel(x, ffn1_norm, ffn1_w_gate, ffn1_w_up, ffn1_w_down, mix_norm, w_in, lambda_q1, lambda_k1,
           lambda_q2, lambda_k2, subln_gain, w_out, ffn2_norm, ffn2_w_gate, ffn2_w_up, ffn2_w_down,
           rel_bias, final_norm):
    b, s, d = x.shape
    depth = w_in.shape[0]
    m = b * s
    diff_bias = _diff_bias_tiles(rel_bias)
    dil_bias = _dil_bias_tiles(rel_bias)
    qscale = jnp.concatenate([
        jnp.full((DIFF_WIDTH,), HEAD_DIM ** -0.5 * LOG2E, F32), jnp.ones((2 * DIFF_WIDTH,), F32),
        jnp.full((DIL_WIDTH,), HEAD_DIM ** -0.5, F32), jnp.ones((2 * DIL_WIDTH,), F32)])
    h = x.reshape(m, d)
    for layer in range(depth):
        row = lambda v: v[layer].reshape(1, -1)
        h = _ffn(h, row(ffn1_norm), _chunked(ffn1_w_gate[layer]), _chunked(ffn1_w_up[layer]),
                 ffn1_w_down[layer].astype(BF16), row(ffn1_norm), final=False)
        qkv = _inproj(h, row(mix_norm), (w_in[layer] * qscale).astype(BF16)).reshape(b, s, -1)
        oa = _diff_attn(qkv, diff_bias, row(lambda_q1), row(lambda_k1), row(lambda_q2), row(lambda_k2),
                        subln_gain[layer].reshape(-1, 1), lam0=_lambda_init(layer))
        obs, lws = [], []
        for br, (_, dil) in enumerate(DILATED_CONFIGS):
            o_b, lw_b = _dilated_branch(qkv, dil_bias[br], dil)
            obs.append(o_b.reshape(m, DIL_WIDTH))
            lws.append(lw_b.reshape(m, LANES))
        h = _outproj(h, oa.reshape(m, DIFF_WIDTH), obs, lws, w_out[layer].astype(BF16))
        last = layer == depth - 1
        h = _ffn(h, row(ffn2_norm), _chunked(ffn2_w_gate[layer]), _chunked(ffn2_w_up[layer]),
                 ffn2_w_down[layer].astype(BF16), final_norm.reshape(1, -1), final=last)
    return h.reshape(b, s, d)
```

```python
import functools
import math

import numpy as np
import jax
import jax.numpy as jnp
from jax import lax
from jax.experimental import pallas as pl
from jax.experimental.pallas import tpu as pltpu

HEAD_DIM = 64
N_DIFF_HEADS = 4
N_DIL_HEADS = 8
DIFF_WIDTH = 512
DIL_WIDTH = 512
DILATED_CONFIGS = ((128, 1), (512, 4), (2048, 16))
NUM_BUCKETS = 32
MAX_DISTANCE = 2048
EPS = 1e-6
NEG = -1e30

LANES = 128
DIL_W = 128
DIL_UNROLL = 8
ATT_TQ = 512
ATT_TK = 256
ATT_R = ATT_TQ // ATT_TK
assert ATT_R % 2 == 0
ATT_PAD = 16
ATT_CW = 256
VMEM_LIMIT = 56 * 1024 * 1024
LOG2E = math.log2(math.e)

F32 = jnp.float32
BF16 = jnp.bfloat16


def _bucket_lower_bounds():
    d = np.arange(0, 2 * MAX_DISTANCE + 1)
    max_exact = NUM_BUCKETS // 2
    v = np.log(np.maximum(d, 1) / max_exact) / math.log(MAX_DISTANCE / max_exact) * (NUM_BUCKETS - max_exact)
    b = np.where(d < max_exact, d, np.minimum(max_exact + v.astype(np.int64), NUM_BUCKETS - 1))
    return [int(np.argmax(b == k)) for k in range(NUM_BUCKETS)]


BUCKET_LO = _bucket_lower_bounds()
N_BIAS_TILES = ATT_R + -(-(BUCKET_LO[-1] + ATT_TK - 1) // ATT_TK)
BIAS_ROLL_W = pl.next_power_of_2(ATT_TQ + ATT_TK)


def _lambda_init(layer):
    return 0.8 - 0.6 * math.exp(-0.3 * layer)


def _bias_of_distance(rb_ref, col, dist):
    val = jnp.full(dist.shape, rb_ref[0, col], F32)
    for b in range(1, NUM_BUCKETS):
        val = jnp.where(dist >= BUCKET_LO[b], rb_ref[b, col], val)
    return val


def _diff_bias_kernel(rb_ref, out_ref):
    h = pl.program_id(0)
    delta = (pl.program_id(1) - (ATT_R - 1)) * ATT_TK
    tq, tk = ATT_TQ, ATT_TK
    u = lax.broadcasted_iota(jnp.int32, (8, BIAS_ROLL_W), 1)
    d = u + (delta - tk)
    for m in range(2):
        val = _bias_of_distance(rb_ref, 2 * h + m, d) * LOG2E
        val = jnp.where(d < 0, NEG, val)
        x = jnp.broadcast_to(val[0:1, :], (tk, BIAS_ROLL_W))
        rolled = pltpu.roll(x, 0, 1, stride=1, stride_axis=0)
        out_ref[0, 0, :, m * tq:(m + 1) * tq] = rolled[:, tk:tk + tq]


def _diff_bias_tiles(rel_bias):
    tq, tk = ATT_TQ, ATT_TK
    return pl.pallas_call(
        _diff_bias_kernel,
        grid=(N_DIFF_HEADS, N_BIAS_TILES),
        in_specs=[pl.BlockSpec(memory_space=pltpu.SMEM)],
        out_specs=pl.BlockSpec((1, 1, tk, 2 * tq), lambda h, d: (h, d, 0, 0)),
        out_shape=jax.ShapeDtypeStruct((N_DIFF_HEADS, N_BIAS_TILES, tk, 2 * tq), F32),
        name="diff_bias_tiles",
    )(rel_bias)


def _dil_bias_kernel(rb_ref, out_ref):
    br = pl.program_id(0)
    h = pl.program_id(1)
    w = DIL_W
    dil = jnp.where(br == 0, DILATED_CONFIGS[0][1],
                    jnp.where(br == 1, DILATED_CONFIGS[1][1], DILATED_CONFIGS[2][1]))
    u = lax.broadcasted_iota(jnp.int32, (8, 4 * w), 1)
    dist = (2 - pl.program_id(2)) * w - u
    val = _bias_of_distance(rb_ref, 2 * N_DIFF_HEADS + h, dist * dil) * LOG2E
    val = jnp.where((dist >= 0) & (dist <= w), val, NEG)
    x = jnp.broadcast_to(val[0:1, :], (w, 4 * w))
    rolled = pltpu.roll(x, 0, 1, stride=1, stride_axis=0)
    out_ref[0, 0, 0] = rolled[:, w:3 * w]


def _dil_bias_tiles(rel_bias):
    w = DIL_W
    nb = len(DILATED_CONFIGS)
    return pl.pallas_call(
        _dil_bias_kernel,
        grid=(nb, N_DIL_HEADS, 2),
        in_specs=[pl.BlockSpec(memory_space=pltpu.SMEM)],
        out_specs=pl.BlockSpec((1, 1, 1, w, 2 * w), lambda b, h, k: (b, h, k, 0, 0)),
        out_shape=jax.ShapeDtypeStruct((nb, N_DIL_HEADS, 2, w, 2 * w), F32),
        name="dil_bias_tiles",
    )(rel_bias)


def _rms(x, g):
    return x * lax.rsqrt(jnp.mean(x * x, axis=-1, keepdims=True) + EPS) * g


def _ffn_kernel(x_ref, g_ref, wg_ref, wu_ref, wd_ref, fg_ref, o_ref, h_ref, *, n_chunks, chunk, final):
    x = x_ref[...]
    xn = _rms(x, g_ref[...]).astype(BF16)
    for c in range(n_chunks):
        gate = jnp.dot(xn, wg_ref[c], preferred_element_type=F32)
        up = jnp.dot(xn, wu_ref[c], preferred_element_type=F32)
        h_ref[:, c * chunk:(c + 1) * chunk] = (gate * jax.nn.sigmoid(gate) * up).astype(BF16)
    y = x + 0.5 * jnp.dot(h_ref[...], wd_ref[...], preferred_element_type=F32)
    if final:
        y = _rms(y, fg_ref[...])
    o_ref[...] = y


def _ffn(x, gain, wg, wu, wd, final_gain, *, final, tm=512):
    m, d = x.shape
    n_chunks, _, chunk = wg.shape
    dff = n_chunks * chunk
    const = lambda *shape: pl.BlockSpec(shape, lambda i: (0,) * len(shape), pipeline_mode=pl.Buffered(1))
    return pl.pallas_call(
        functools.partial(_ffn_kernel, n_chunks=n_chunks, chunk=chunk, final=final),
        grid=(m // tm,),
        in_specs=[
            pl.BlockSpec((tm, d), lambda i: (i, 0)),
            const(1, d),
            const(n_chunks, d, chunk),
            const(n_chunks, d, chunk),
            const(dff, d),
            const(1, d),
        ],
        out_specs=pl.BlockSpec((tm, d), lambda i: (i, 0)),
        out_shape=jax.ShapeDtypeStruct((m, d), F32),
        scratch_shapes=[pltpu.VMEM((tm, dff), BF16)],
        compiler_params=pltpu.CompilerParams(
            dimension_semantics=("arbitrary",), vmem_limit_bytes=VMEM_LIMIT),
        name="ffn",
    )(x, gain, wg, wu, wd, final_gain)


def _inproj_kernel(x_ref, g_ref, w_ref, o_ref, *, n_chunks, chunk):
    xn = _rms(x_ref[...], g_ref[...]).astype(BF16)
    for c in range(n_chunks):
        sl = slice(c * chunk, (c + 1) * chunk)
        o_ref[:, sl] = jnp.dot(xn, w_ref[:, sl], preferred_element_type=F32).astype(BF16)


def _inproj(x, gain, w, *, tm=512, chunk=512):
    m, d = x.shape
    n = w.shape[1]
    return pl.pallas_call(
        functools.partial(_inproj_kernel, n_chunks=n // chunk, chunk=chunk),
        grid=(m // tm,),
        in_specs=[
            pl.BlockSpec((tm, d), lambda i: (i, 0)),
            pl.BlockSpec((1, d), lambda i: (0, 0), pipeline_mode=pl.Buffered(1)),
            pl.BlockSpec((d, n), lambda i: (0, 0), pipeline_mode=pl.Buffered(1)),
        ],
        out_specs=pl.BlockSpec((tm, n), lambda i: (i, 0)),
        out_shape=jax.ShapeDtypeStruct((m, n), BF16),
        compiler_params=pltpu.CompilerParams(
            dimension_semantics=("arbitrary",), vmem_limit_bytes=VMEM_LIMIT),
        name="inproj",
    )(x, gain, w)


def _diff_attn_kernel(q_ref, k_ref, v_ref, bias_ref, lq1_ref, lk1_ref, lq2_ref, lk2_ref, sg_ref,
                      o_ref, vt_ref, qs_ref, acc_ref, m_ref, al_ref, mxa_ref, mxb_ref,
                      sa_ref, sb_ref, ea_ref, eb_ref, *, lam0, n_kv):
    tq, tk = ATT_TQ, ATT_TK
    hw = 2 * HEAD_DIM
    qi = pl.program_id(2)
    n_tiles = ATT_R * (qi + 1)

    @pl.when(qi == 0)
    def _():
        ones_row = (lax.broadcasted_iota(jnp.int32, (ATT_PAD, tk), 0) == 0).astype(BF16)
        for c in range(n_kv):
            vt_ref[c, 0:hw, :] = v_ref[0, c * tk:(c + 1) * tk, :].astype(F32).T.astype(BF16)
            vt_ref[c, hw:, :] = ones_row

    qt = q_ref[0].astype(F32).T
    row = lax.broadcasted_iota(jnp.int32, qt.shape, 0)
    qs_ref[...] = jnp.concatenate(
        [jnp.where(row < HEAD_DIM, qt, 0.0), jnp.where(row >= HEAD_DIM, qt, 0.0)], axis=1).astype(BF16)

    cw = ATT_CW
    chunks = [slice(c * cw, (c + 1) * cw) for c in range(2 * tq // cw)]

    def score_chunk(j, cols, s_ref, mx_ref):
        kblk = k_ref[0, pl.ds(pl.multiple_of(j * tk, tk), tk), :]
        bi = jnp.minimum(ATT_R * (qi + 1) - 1 - j, N_BIAS_TILES - 1)
        t = jnp.dot(kblk, qs_ref[:, cols], preferred_element_type=F32) + bias_ref[0, bi, :, cols]
        s_ref[:, cols] = t
        mx_ref[:, cols] = jnp.max(t, axis=0, keepdims=True)

    def pv_chunk(j, cols, e_ref):
        acc_ref[:, cols] = (al_ref[:, cols] * acc_ref[:, cols]
                            + jnp.dot(vt_ref[j], e_ref[:, cols], preferred_element_type=F32))

    def softmax_chunk(cols, s_ref, mx_ref, e_ref):
        m_old = m_ref[:, cols]
        m_new = jnp.maximum(m_old, mx_ref[:, cols])
        al_ref[:, cols] = jnp.exp2(m_old - m_new)
        m_ref[:, cols] = m_new
        e_ref[:, cols] = jnp.exp2(s_ref[:, cols] - m_new).astype(BF16)

    def phase(j, cur, nxt):
        jn = jnp.minimum(j + 1, n_tiles - 1)
        jp = jnp.maximum(j - 1, 0)
        for cols in chunks:
            score_chunk(jn, cols, nxt[0], nxt[1])
            pv_chunk(jp, cols, nxt[2])
            softmax_chunk(cols, *cur)

    buf_a = (sa_ref, mxa_ref, ea_ref)
    buf_b = (sb_ref, mxb_ref, eb_ref)
    m_ref[...] = jnp.full(m_ref.shape, NEG, F32)
    al_ref[...] = jnp.ones(al_ref.shape, F32)
    acc_ref[...] = jnp.zeros(acc_ref.shape, F32)
    eb_ref[...] = jnp.zeros(eb_ref.shape, BF16)
    for cols in chunks:
        score_chunk(0, cols, sa_ref, mxa_ref)

    def body(i, carry):
        phase(2 * i, buf_a, buf_b)
        phase(2 * i + 1, buf_b, buf_a)
        return carry

    lax.fori_loop(0, n_tiles // 2, body, 0)
    for cols in chunks:
        pv_chunk(n_tiles - 1, cols, eb_ref)

    lam = (jnp.exp(jnp.sum(lq1_ref[...] * lk1_ref[...])) - jnp.exp(jnp.sum(lq2_ref[...] * lk2_ref[...])) + lam0)
    o = acc_ref[0:hw, :] / acc_ref[hw:hw + 1, :]
    o = o[:, :tq] - lam * o[:, tq:]
    y = o * lax.rsqrt(jnp.mean(o * o, axis=0, keepdims=True) + EPS) * sg_ref[...]
    y = y * (1.0 - lam0)
    o_ref[0] = y.T.astype(BF16)


def _diff_attn(qkv, bias_tiles, lq1, lk1, lq2, lk2, sub_gain, *, lam0):
    b, s, _ = qkv.shape
    tq, tk = ATT_TQ, ATT_TK
    hw = 2 * HEAD_DIM
    small = lambda shape: pl.BlockSpec(shape, lambda h, bb, i: (0,) * len(shape))
    return pl.pallas_call(
        functools.partial(_diff_attn_kernel, lam0=lam0, n_kv=s // tk),
        grid=(N_DIFF_HEADS, b, s // tq),
        in_specs=[
            pl.BlockSpec((1, tq, hw), lambda h, bb, i: (bb, i, h)),
            pl.BlockSpec((1, s, hw), lambda h, bb, i: (bb, 0, N_DIFF_HEADS + h)),
            pl.BlockSpec((1, s, hw), lambda h, bb, i: (bb, 0, 2 * N_DIFF_HEADS + h)),
            pl.BlockSpec((1, N_BIAS_TILES, tk, 2 * tq), lambda h, bb, i: (h, 0, 0, 0),
                         pipeline_mode=pl.Buffered(1)),
            small((1, HEAD_DIM)), small((1, HEAD_DIM)), small((1, HEAD_DIM)), small((1, HEAD_DIM)),
            small((hw, 1)),
        ],
        out_specs=pl.BlockSpec((1, tq, hw), lambda h, bb, i: (bb, i, h)),
        out_shape=jax.ShapeDtypeStruct((b, s, DIFF_WIDTH), BF16),
        scratch_shapes=[
            pltpu.VMEM((s // tk, hw + ATT_PAD, tk), BF16),
            pltpu.VMEM((hw, 2 * tq), BF16),
            pltpu.VMEM((hw + ATT_PAD, 2 * tq), F32),
            pltpu.VMEM((1, 2 * tq), F32),
            pltpu.VMEM((1, 2 * tq), F32),
            pltpu.VMEM((1, 2 * tq), F32),
            pltpu.VMEM((1, 2 * tq), F32),
            pltpu.VMEM((tk, 2 * tq), F32),
            pltpu.VMEM((tk, 2 * tq), F32),
            pltpu.VMEM((tk, 2 * tq), BF16),
            pltpu.VMEM((tk, 2 * tq), BF16),
        ],
        compiler_params=pltpu.CompilerParams(
            dimension_semantics=("arbitrary", "arbitrary", "arbitrary"), vmem_limit_bytes=VMEM_LIMIT),
        name="diff_attn",
    )(qkv, qkv, qkv, bias_tiles, lq1, lk1, lq2, lk2, sub_gain)


def _dilated_kernel(q_ref, k_ref, v_ref, bias_ref, o_ref, xf_ref, x4_ref, nat_ref):
    w = DIL_W
    s_len = q_ref.shape[1]
    l4 = s_len // 4
    srcs = (q_ref, k_ref, v_ref)
    cp = 512
    for t in range(3):
        for c in range(s_len // cp):
            xf_ref[t, c * cp:(c + 1) * cp, :] = srcs[t][0, c * cp:(c + 1) * cp, :].astype(F32)
    for t in range(3):
        for r in range(4):
            for c in range(l4 // cp):
                x4_ref[t, r * l4 + c * cp:r * l4 + (c + 1) * cp, :] = (
                    xf_ref[t, pl.ds(r + 4 * c * cp, cp, stride=4), :])

    low = lax.broadcasted_iota(jnp.int32, (w, LANES), 1) < HEAD_DIM
    nt = (((1,), (1,)), ((), ()))

    def rows(t, br, r, j0, n):
        if br == 0:
            return srcs[t][0, pl.ds(pl.multiple_of(j0, w), n), :]
        if br == 1:
            return x4_ref[t, pl.ds(pl.multiple_of(r * l4 + j0, w), n), :].astype(BF16)
        return x4_ref[t, pl.ds((r % 4) * l4 + r // 4 + 4 * j0, n, stride=4), :].astype(BF16)

    def attend(br, r, n):
        k0 = jnp.maximum(n - 1, 0) * w
        kind = jnp.where(n == 0, 1, 0)
        qp = rows(0, br, r, n * w, w)
        kk = rows(1, br, r, k0, 2 * w)
        vv = rows(2, br, r, k0, 2 * w)
        outs, lws = [], []
        for hh in range(2):
            qm = jnp.where(low if hh == 0 else jnp.logical_not(low), qp, jnp.zeros_like(qp))
            s = lax.dot_general(qm, kk, nt, preferred_element_type=F32) + bias_ref[br, hh, kind]
            m = jnp.max(s, axis=1, keepdims=True)
            e = jnp.exp2(s - m)
            l = jnp.sum(e, axis=1, keepdims=True)
            outs.append(jnp.dot(e.astype(BF16), vv, preferred_element_type=F32) / l)
            lws.append(m + jnp.log2(l))
        dil = DILATED_CONFIGS[br][1]
        start = r + dil * w * n
        dst = pl.ds(pl.multiple_of(start, w), w) if dil == 1 else pl.ds(start, w, stride=dil)
        nat_ref[br, 0, dst, :] = jnp.where(low, outs[0], outs[1])
        nat_ref[br, 1, dst, :] = jnp.where(low, lws[0], lws[1])

    for br, (_, dil) in enumerate(DILATED_CONFIGS):
        nblk = s_len // dil // w

        def blocks(i, carry, br=br, nblk=nblk):
            for u in range(DIL_UNROLL):
                idx = i * DIL_UNROLL + u
                attend(br, idx // nblk, idx % nblk)
            return carry

        lax.fori_loop(0, dil * nblk // DIL_UNROLL, blocks, 0)

    for c in range(s_len // cp):
        sl = slice(c * cp, (c + 1) * cp)
        lws = [nat_ref[br, 1, sl, :] for br in range(3)]
        top = jnp.maximum(jnp.maximum(lws[0], lws[1]), lws[2])
        wts = [jnp.exp2(v - top) for v in lws]
        num = wts[0] * nat_ref[0, 0, sl, :] + wts[1] * nat_ref[1, 0, sl, :] + wts[2] * nat_ref[2, 0, sl, :]
        o_ref[0, sl, :] = (num / (wts[0] + wts[1] + wts[2])).astype(BF16)


def _dilated_attn(qkv, bias):
    b, s, _ = qkv.shape
    w = DIL_W
    col0 = 3 * DIFF_WIDTH // LANES
    pairs = DIL_WIDTH // LANES
    part = lambda t: pl.BlockSpec((1, s, LANES), lambda bb, p: (bb, 0, col0 + t * pairs + p))
    return pl.pallas_call(
        _dilated_kernel,
        grid=(b, pairs),
        in_specs=[part(0), part(1), part(2),
                  pl.BlockSpec((len(DILATED_CONFIGS), 2, 2, w, 2 * w), lambda bb, p: (0, p, 0, 0, 0))],
        out_specs=pl.BlockSpec((1, s, LANES), lambda bb, p: (bb, 0, p)),
        out_shape=jax.ShapeDtypeStruct((b, s, DIL_WIDTH), BF16),
        scratch_shapes=[
            pltpu.VMEM((3, s, LANES), F32),
            pltpu.VMEM((3, s, LANES), F32),
            pltpu.VMEM((len(DILATED_CONFIGS), 2, s, LANES), F32),
        ],
        compiler_params=pltpu.CompilerParams(
            dimension_semantics=("arbitrary", "arbitrary"), vmem_limit_bytes=VMEM_LIMIT),
        name="dilated_attn",
    )(qkv, qkv, qkv, bias)


def _outproj_kernel(x_ref, oa_ref, ob_ref, w_ref, o_ref):
    y = jnp.dot(oa_ref[...], w_ref[0:DIFF_WIDTH, :], preferred_element_type=F32)
    y = y + jnp.dot(ob_ref[...], w_ref[DIFF_WIDTH:, :], preferred_element_type=F32)
    o_ref[...] = x_ref[...] + y


def _outproj(x, oa, ob, w, *, tm=512):
    m, d = x.shape
    row = lambda n: pl.BlockSpec((tm, n), lambda i: (i, 0))
    return pl.pallas_call(
        _outproj_kernel,
        grid=(m // tm,),
        in_specs=[row(d), row(DIFF_WIDTH), row(DIL_WIDTH),
                  pl.BlockSpec(w.shape, lambda i: (0, 0), pipeline_mode=pl.Buffered(1))],
        out_specs=row(d),
        out_shape=jax.ShapeDtypeStruct((m, d), F32),
        compiler_params=pltpu.CompilerParams(
            dimension_semantics=("arbitrary",), vmem_limit_bytes=VMEM_LIMIT),
        name="outproj",
    )(x, oa, ob, w)


def _chunked(w, chunk=256):
    d, f = w.shape
    return w.astype(BF16).reshape(d, f // chunk, chunk).transpose(1, 0, 2)


def kernel(x, ffn1_norm, ffn1_w_gate, ffn1_w_up, ffn1_w_down, mix_norm, w_in, lambda_q1, lambda_k1,
           lambda_q2, lambda_k2, subln_gain, w_out, ffn2_norm, ffn2_w_gate, ffn2_w_up, ffn2_w_down,
           rel_bias, final_norm):
    b, s, d = x.shape
    depth = w_in.shape[0]
    m = b * s
    diff_bias = _diff_bias_tiles(rel_bias)
    dil_bias = _dil_bias_tiles(rel_bias)
    qscale = jnp.concatenate([
        jnp.full((DIFF_WIDTH,), HEAD_DIM ** -0.5 * LOG2E, F32), jnp.ones((2 * DIFF_WIDTH,), F32),
        jnp.full((DIL_WIDTH,), HEAD_DIM ** -0.5 * LOG2E, F32), jnp.ones((2 * DIL_WIDTH,), F32)])
    h = x.reshape(m, d)
    for layer in range(depth):
        row = lambda v: v[layer].reshape(1, -1)
        h = _ffn(h, row(ffn1_norm), _chunked(ffn1_w_gate[layer]), _chunked(ffn1_w_up[layer]),
                 ffn1_w_down[layer].astype(BF16), row(ffn1_norm), final=False)
        qkv = _inproj(h, row(mix_norm), (w_in[layer] * qscale).astype(BF16)).reshape(b, s, -1)
        oa = _diff_attn(qkv, diff_bias, row(lambda_q1), row(lambda_k1), row(lambda_q2), row(lambda_k2),
                        subln_gain[layer].reshape(-1, 1), lam0=_lambda_init(layer))
        ob = _dilated_attn(qkv, dil_bias)
        h = _outproj(h, oa.reshape(m, DIFF_WIDTH), ob.reshape(m, DIL_WIDTH), w_out[layer].astype(BF16))
        last = layer == depth - 1
        h = _ffn(h, row(ffn2_norm), _chunked(ffn2_w_gate[layer]), _chunked(ffn2_w_up[layer]),
                 ffn2_w_down[layer].astype(BF16), final_norm.reshape(1, -1), final=last)
    return h.reshape(b, s, d)
```

```python
import functools
import math

import numpy as np
import jax
import jax.numpy as jnp
from jax import lax
from jax.experimental import pallas as pl
from jax.experimental.pallas import tpu as pltpu

HEAD_DIM = 64
N_DIFF_HEADS = 4
N_DIL_HEADS = 8
DIFF_WIDTH = 512
DIL_WIDTH = 512
DILATED_CONFIGS = ((128, 1), (512, 4), (2048, 16))
NUM_BUCKETS = 32
MAX_DISTANCE = 2048
EPS = 1e-6
NEG = -1e30

LANES = 128
DIL_W = 128
DIL_UNROLL = 8
ATT_TQ = 512
ATT_TK = 256
ATT_R = ATT_TQ // ATT_TK
assert ATT_R % 2 == 0
ATT_PAD = 16
ATT_CW = 256
VMEM_LIMIT = 56 * 1024 * 1024
LOG2E = math.log2(math.e)

F32 = jnp.float32
BF16 = jnp.bfloat16


def _bucket_lower_bounds():
    d = np.arange(0, 2 * MAX_DISTANCE + 1)
    max_exact = NUM_BUCKETS // 2
    v = np.log(np.maximum(d, 1) / max_exact) / math.log(MAX_DISTANCE / max_exact) * (NUM_BUCKETS - max_exact)
    b = np.where(d < max_exact, d, np.minimum(max_exact + v.astype(np.int64), NUM_BUCKETS - 1))
    return [int(np.argmax(b == k)) for k in range(NUM_BUCKETS)]


BUCKET_LO = _bucket_lower_bounds()
N_BIAS_TILES = ATT_R + -(-(BUCKET_LO[-1] + ATT_TK - 1) // ATT_TK)
BIAS_ROLL_W = pl.next_power_of_2(ATT_TQ + ATT_TK)


def _lambda_init(layer):
    return 0.8 - 0.6 * math.exp(-0.3 * layer)


def _bias_of_distance(rb_ref, col, dist):
    val = jnp.full(dist.shape, rb_ref[0, col], F32)
    for b in range(1, NUM_BUCKETS):
        val = jnp.where(dist >= BUCKET_LO[b], rb_ref[b, col], val)
    return val


def _diff_bias_kernel(rb_ref, out_ref):
    h = pl.program_id(0)
    delta = (pl.program_id(1) - (ATT_R - 1)) * ATT_TK
    tq, tk = ATT_TQ, ATT_TK
    u = lax.broadcasted_iota(jnp.int32, (8, BIAS_ROLL_W), 1)
    d = u + (delta - tk)
    for m in range(2):
        val = _bias_of_distance(rb_ref, 2 * h + m, d) * LOG2E
        val = jnp.where(d < 0, NEG, val)
        x = jnp.broadcast_to(val[0:1, :], (tk, BIAS_ROLL_W))
        rolled = pltpu.roll(x, 0, 1, stride=1, stride_axis=0)
        out_ref[0, 0, :, m * tq:(m + 1) * tq] = rolled[:, tk:tk + tq]


def _diff_bias_tiles(rel_bias):
    tq, tk = ATT_TQ, ATT_TK
    return pl.pallas_call(
        _diff_bias_kernel,
        grid=(N_DIFF_HEADS, N_BIAS_TILES),
        in_specs=[pl.BlockSpec(memory_space=pltpu.SMEM)],
        out_specs=pl.BlockSpec((1, 1, tk, 2 * tq), lambda h, d: (h, d, 0, 0)),
        out_shape=jax.ShapeDtypeStruct((N_DIFF_HEADS, N_BIAS_TILES, tk, 2 * tq), F32),
        name="diff_bias_tiles",
    )(rel_bias)


def _dil_bias_kernel(rb_ref, out_ref):
    br = pl.program_id(0)
    h = pl.program_id(1)
    w = DIL_W
    dil = jnp.where(br == 0, DILATED_CONFIGS[0][1],
                    jnp.where(br == 1, DILATED_CONFIGS[1][1], DILATED_CONFIGS[2][1]))
    u = lax.broadcasted_iota(jnp.int32, (8, 4 * w), 1)
    dist = (2 - pl.program_id(2)) * w - u
    val = _bias_of_distance(rb_ref, 2 * N_DIFF_HEADS + h, dist * dil) * LOG2E
    val = jnp.where((dist >= 0) & (dist <= w), val, NEG)
    x = jnp.broadcast_to(val[0:1, :], (w, 4 * w))
    rolled = pltpu.roll(x, 0, 1, stride=1, stride_axis=0)
    out_ref[0, 0, 0] = rolled[:, w:3 * w]


def _dil_bias_tiles(rel_bias):
    w = DIL_W
    nb = len(DILATED_CONFIGS)
    return pl.pallas_call(
        _dil_bias_kernel,
        grid=(nb, N_DIL_HEADS, 2),
        in_specs=[pl.BlockSpec(memory_space=pltpu.SMEM)],
        out_specs=pl.BlockSpec((1, 1, 1, w, 2 * w), lambda b, h, k: (b, h, k, 0, 0)),
        out_shape=jax.ShapeDtypeStruct((nb, N_DIL_HEADS, 2, w, 2 * w), F32),
        name="dil_bias_tiles",
    )(rel_bias)


def _rms(x, g):
    return x * lax.rsqrt(jnp.mean(x * x, axis=-1, keepdims=True) + EPS) * g


def _ffn_kernel(x_ref, g_ref, wg_ref, wu_ref, wd_ref, fg_ref, o_ref, h_ref, *, n_chunks, chunk, final):
    x = x_ref[...]
    xn = _rms(x, g_ref[...]).astype(BF16)
    for c in range(n_chunks):
        gate = jnp.dot(xn, wg_ref[c], preferred_element_type=F32)
        up = jnp.dot(xn, wu_ref[c], preferred_element_type=F32)
        h_ref[:, c * chunk:(c + 1) * chunk] = (gate * jax.nn.sigmoid(gate) * up).astype(BF16)
    y = x + 0.5 * jnp.dot(h_ref[...], wd_ref[...], preferred_element_type=F32)
    if final:
        y = _rms(y, fg_ref[...])
    o_ref[...] = y


def _ffn(x, gain, wg, wu, wd, final_gain, *, final, tm=512):
    m, d = x.shape
    n_chunks, _, chunk = wg.shape
    dff = n_chunks * chunk
    const = lambda *shape: pl.BlockSpec(shape, lambda i: (0,) * len(shape), pipeline_mode=pl.Buffered(1))
    return pl.pallas_call(
        functools.partial(_ffn_kernel, n_chunks=n_chunks, chunk=chunk, final=final),
        grid=(m // tm,),
        in_specs=[
            pl.BlockSpec((tm, d), lambda i: (i, 0)),
            const(1, d),
            const(n_chunks, d, chunk),
            const(n_chunks, d, chunk),
            const(dff, d),
            const(1, d),
        ],
        out_specs=pl.BlockSpec((tm, d), lambda i: (i, 0)),
        out_shape=jax.ShapeDtypeStruct((m, d), F32),
        scratch_shapes=[pltpu.VMEM((tm, dff), BF16)],
        compiler_params=pltpu.CompilerParams(
            dimension_semantics=("arbitrary",), vmem_limit_bytes=VMEM_LIMIT),
        name="ffn",
    )(x, gain, wg, wu, wd, final_gain)


def _inproj_kernel(x_ref, g_ref, w_ref, o_ref, *, n_chunks, chunk):
    xn = _rms(x_ref[...], g_ref[...]).astype(BF16)
    for c in range(n_chunks):
        sl = slice(c * chunk, (c + 1) * chunk)
        o_ref[:, sl] = jnp.dot(xn, w_ref[:, sl], preferred_element_type=F32).astype(BF16)


def _inproj(x, gain, w, *, tm=512, chunk=512):
    m, d = x.shape
    n = w.shape[1]
    return pl.pallas_call(
        functools.partial(_inproj_kernel, n_chunks=n // chunk, chunk=chunk),
        grid=(m // tm,),
        in_specs=[
            pl.BlockSpec((tm, d), lambda i: (i, 0)),
            pl.BlockSpec((1, d), lambda i: (0, 0), pipeline_mode=pl.Buffered(1)),
            pl.BlockSpec((d, n), lambda i: (0, 0), pipeline_mode=pl.Buffered(1)),
        ],
        out_specs=pl.BlockSpec((tm, n), lambda i: (i, 0)),
        out_shape=jax.ShapeDtypeStruct((m, n), BF16),
        compiler_params=pltpu.CompilerParams(
            dimension_semantics=("arbitrary",), vmem_limit_bytes=VMEM_LIMIT),
        name="inproj",
    )(x, gain, w)


def _diff_attn_kernel(q_ref, k_ref, v_ref, bias_ref, lq1_ref, lk1_ref, lq2_ref, lk2_ref, sg_ref,
                      o_ref, vt_ref, qs_ref, acc_ref, m_ref, al_ref, mxa_ref, mxb_ref,
                      sa_ref, sb_ref, ea_ref, eb_ref, *, lam0, n_kv):
    tq, tk = ATT_TQ, ATT_TK
    hw = 2 * HEAD_DIM
    qi = pl.program_id(2)
    n_tiles = ATT_R * (qi + 1)

    @pl.when(qi == 0)
    def _():
        ones_row = (lax.broadcasted_iota(jnp.int32, (ATT_PAD, tk), 0) == 0).astype(BF16)
        for c in range(n_kv):
            vt_ref[c, 0:hw, :] = v_ref[0, c * tk:(c + 1) * tk, :].astype(F32).T.astype(BF16)
            vt_ref[c, hw:, :] = ones_row

    qt = q_ref[0].astype(F32).T
    row = lax.broadcasted_iota(jnp.int32, qt.shape, 0)
    qs_ref[...] = jnp.concatenate(
        [jnp.where(row < HEAD_DIM, qt, 0.0), jnp.where(row >= HEAD_DIM, qt, 0.0)], axis=1).astype(BF16)

    cw = ATT_CW
    chunks = [slice(c * cw, (c + 1) * cw) for c in range(2 * tq // cw)]

    def score_chunk(j, cols, s_ref, mx_ref):
        kblk = k_ref[0, pl.ds(pl.multiple_of(j * tk, tk), tk), :]
        bi = jnp.minimum(ATT_R * (qi + 1) - 1 - j, N_BIAS_TILES - 1)
        t = jnp.dot(kblk, qs_ref[:, cols], preferred_element_type=F32) + bias_ref[0, bi, :, cols]
        s_ref[:, cols] = t
        mx_ref[:, cols] = jnp.max(t, axis=0, keepdims=True)

    def pv_chunk(j, cols, e_ref):
        acc_ref[:, cols] = (al_ref[:, cols] * acc_ref[:, cols]
                            + jnp.dot(vt_ref[j], e_ref[:, cols], preferred_element_type=F32))

    def softmax_chunk(cols, s_ref, mx_ref, e_ref):
        m_old = m_ref[:, cols]
        m_new = jnp.maximum(m_old, mx_ref[:, cols])
        al_ref[:, cols] = jnp.exp2(m_old - m_new)
        m_ref[:, cols] = m_new
        e_ref[:, cols] = jnp.exp2((s_ref[:, cols] - m_new).astype(BF16))

    def phase(j, cur, nxt):
        jn = jnp.minimum(j + 1, n_tiles - 1)
        jp = jnp.maximum(j - 1, 0)
        for cols in chunks:
            score_chunk(jn, cols, nxt[0], nxt[1])
            pv_chunk(jp, cols, nxt[2])
            softmax_chunk(cols, *cur)

    buf_a = (sa_ref, mxa_ref, ea_ref)
    buf_b = (sb_ref, mxb_ref, eb_ref)
    m_ref[...] = jnp.full(m_ref.shape, NEG, F32)
    al_ref[...] = jnp.ones(al_ref.shape, F32)
    acc_ref[...] = jnp.zeros(acc_ref.shape, F32)
    eb_ref[...] = jnp.zeros(eb_ref.shape, BF16)
    for cols in chunks:
        score_chunk(0, cols, sa_ref, mxa_ref)

    def body(i, carry):
        phase(2 * i, buf_a, buf_b)
        phase(2 * i + 1, buf_b, buf_a)
        return carry

    lax.fori_loop(0, n_tiles // 2, body, 0)
    for cols in chunks:
        pv_chunk(n_tiles - 1, cols, eb_ref)

    lam = (jnp.exp(jnp.sum(lq1_ref[...] * lk1_ref[...])) - jnp.exp(jnp.sum(lq2_ref[...] * lk2_ref[...])) + lam0)
    o = acc_ref[0:hw, :] / acc_ref[hw:hw + 1, :]
    o = o[:, :tq] - lam * o[:, tq:]
    y = o * lax.rsqrt(jnp.mean(o * o, axis=0, keepdims=True) + EPS) * sg_ref[...]
    y = y * (1.0 - lam0)
    o_ref[0] = y.T.astype(BF16)


def _diff_attn(qkv, bias_tiles, lq1, lk1, lq2, lk2, sub_gain, *, lam0):
    b, s, _ = qkv.shape
    tq, tk = ATT_TQ, ATT_TK
    hw = 2 * HEAD_DIM
    small = lambda shape: pl.BlockSpec(shape, lambda h, bb, i: (0,) * len(shape))
    return pl.pallas_call(
        functools.partial(_diff_attn_kernel, lam0=lam0, n_kv=s // tk),
        grid=(N_DIFF_HEADS, b, s // tq),
        in_specs=[
            pl.BlockSpec((1, tq, hw), lambda h, bb, i: (bb, i, h)),
            pl.BlockSpec((1, s, hw), lambda h, bb, i: (bb, 0, N_DIFF_HEADS + h)),
            pl.BlockSpec((1, s, hw), lambda h, bb, i: (bb, 0, 2 * N_DIFF_HEADS + h)),
            pl.BlockSpec((1, N_BIAS_TILES, tk, 2 * tq), lambda h, bb, i: (h, 0, 0, 0),
                         pipeline_mode=pl.Buffered(1)),
            small((1, HEAD_DIM)), small((1, HEAD_DIM)), small((1, HEAD_DIM)), small((1, HEAD_DIM)),
            small((hw, 1)),
        ],
        out_specs=pl.BlockSpec((1, tq, hw), lambda h, bb, i: (bb, i, h)),
        out_shape=jax.ShapeDtypeStruct((b, s, DIFF_WIDTH), BF16),
        scratch_shapes=[
            pltpu.VMEM((s // tk, hw + ATT_PAD, tk), BF16),
            pltpu.VMEM((hw, 2 * tq), BF16),
            pltpu.VMEM((hw + ATT_PAD, 2 * tq), F32),
            pltpu.VMEM((1, 2 * tq), F32),
            pltpu.VMEM((1, 2 * tq), F32),
            pltpu.VMEM((1, 2 * tq), F32),
            pltpu.VMEM((1, 2 * tq), F32),
            pltpu.VMEM((tk, 2 * tq), F32),
            pltpu.VMEM((tk, 2 * tq), F32),
            pltpu.VMEM((tk, 2 * tq), BF16),
            pltpu.VMEM((tk, 2 * tq), BF16),
        ],
        compiler_params=pltpu.CompilerParams(
            dimension_semantics=("arbitrary", "arbitrary", "arbitrary"), vmem_limit_bytes=VMEM_LIMIT),
        name="diff_attn",
    )(qkv, qkv, qkv, bias_tiles, lq1, lk1, lq2, lk2, sub_gain)


def _dilated_kernel(q_ref, k_ref, v_ref, bias_ref, o_ref, xf_ref, x4_ref, nat_ref):
    w = DIL_W
    s_len = q_ref.shape[1]
    l4 = s_len // 4
    srcs = (q_ref, k_ref, v_ref)
    cp = 512
    for t in range(3):
        for c in range(s_len // cp):
            xf_ref[t, c * cp:(c + 1) * cp, :] = srcs[t][0, c * cp:(c + 1) * cp, :].astype(F32)
    for t in range(3):
        for r in range(4):
            for c in range(l4 // cp):
                x4_ref[t, r * l4 + c * cp:r * l4 + (c + 1) * cp, :] = (
                    xf_ref[t, pl.ds(r + 4 * c * cp, cp, stride=4), :])

    low = lax.broadcasted_iota(jnp.int32, (w, LANES), 1) < HEAD_DIM
    nt = (((1,), (1,)), ((), ()))

    def rows(t, br, r, j0, n):
        if br == 0:
            return srcs[t][0, pl.ds(pl.multiple_of(j0, w), n), :]
        if br == 1:
            return x4_ref[t, pl.ds(pl.multiple_of(r * l4 + j0, w), n), :].astype(BF16)
        return x4_ref[t, pl.ds((r % 4) * l4 + r // 4 + 4 * j0, n, stride=4), :].astype(BF16)

    def attend(br, r, n):
        k0 = jnp.maximum(n - 1, 0) * w
        kind = jnp.where(n == 0, 1, 0)
        qp = rows(0, br, r, n * w, w)
        kk = rows(1, br, r, k0, 2 * w)
        vv = jnp.concatenate([rows(2, br, r, k0, 2 * w), jnp.ones((2 * w, LANES), BF16)], axis=1)
        pvs, ms = [], []
        for hh in range(2):
            qm = jnp.where(low if hh == 0 else jnp.logical_not(low), qp, jnp.zeros_like(qp))
            s = lax.dot_general(qm, kk, nt, preferred_element_type=F32) + bias_ref[br, hh, kind]
            m = jnp.max(s, axis=1, keepdims=True)
            e = jnp.exp2((s - m).astype(BF16))
            pvs.append(jnp.dot(e, vv, preferred_element_type=F32))
            ms.append(m)
        dil = DILATED_CONFIGS[br][1]
        start = r + dil * w * n
        dst = pl.ds(pl.multiple_of(start, w), w) if dil == 1 else pl.ds(start, w, stride=dil)
        nat_ref[br, 0, dst, :] = jnp.where(low, pvs[0][:, :LANES], pvs[1][:, :LANES])
        nat_ref[br, 1, dst, :] = jnp.where(low, ms[0], ms[1])
        nat_ref[br, 2, dst, :] = jnp.where(low, pvs[0][:, LANES:], pvs[1][:, LANES:])

    for br, (_, dil) in enumerate(DILATED_CONFIGS):
        nblk = s_len // dil // w

        def blocks(i, carry, br=br, nblk=nblk):
            for u in range(DIL_UNROLL):
                idx = i * DIL_UNROLL + u
                attend(br, idx // nblk, idx % nblk)
            return carry

        lax.fori_loop(0, dil * nblk // DIL_UNROLL, blocks, 0)

    for c in range(s_len // cp):
        sl = slice(c * cp, (c + 1) * cp)
        ms = [nat_ref[br, 1, sl, :] for br in range(3)]
        top = jnp.maximum(jnp.maximum(ms[0], ms[1]), ms[2])
        wts = [jnp.exp2(v - top) for v in ms]
        num = wts[0] * nat_ref[0, 0, sl, :] + wts[1] * nat_ref[1, 0, sl, :] + wts[2] * nat_ref[2, 0, sl, :]
        den = wts[0] * nat_ref[0, 2, sl, :] + wts[1] * nat_ref[1, 2, sl, :] + wts[2] * nat_ref[2, 2, sl, :]
        o_ref[0, sl, :] = (num / den).astype(BF16)


def _dilated_attn(qkv, bias):
    b, s, _ = qkv.shape
    w = DIL_W
    col0 = 3 * DIFF_WIDTH // LANES
    pairs = DIL_WIDTH // LANES
    part = lambda t: pl.BlockSpec((1, s, LANES), lambda bb, p: (bb, 0, col0 + t * pairs + p))
    return pl.pallas_call(
        _dilated_kernel,
        grid=(b, pairs),
        in_specs=[part(0), part(1), part(2),
                  pl.BlockSpec((len(DILATED_CONFIGS), 2, 2, w, 2 * w), lambda bb, p: (0, p, 0, 0, 0))],
        out_specs=pl.BlockSpec((1, s, LANES), lambda bb, p: (bb, 0, p)),
        out_shape=jax.ShapeDtypeStruct((b, s, DIL_WIDTH), BF16),
        scratch_shapes=[
            pltpu.VMEM((3, s, LANES), F32),
            pltpu.VMEM((3, s, LANES), F32),
            pltpu.VMEM((len(DILATED_CONFIGS), 3, s, LANES), F32),
        ],
        compiler_params=pltpu.CompilerParams(
            dimension_semantics=("arbitrary", "arbitrary"), vmem_limit_bytes=VMEM_LIMIT),
        name="dilated_attn",
    )(qkv, qkv, qkv, bias)


def _outproj_kernel(x_ref, oa_ref, ob_ref, w_ref, o_ref):
    y = jnp.dot(oa_ref[...], w_ref[0:DIFF_WIDTH, :], preferred_element_type=F32)
    y = y + jnp.dot(ob_ref[...], w_ref[DIFF_WIDTH:, :], preferred_element_type=F32)
    o_ref[...] = x_ref[...] + y


def _outproj(x, oa, ob, w, *, tm=512):
    m, d = x.shape
    row = lambda n: pl.BlockSpec((tm, n), lambda i: (i, 0))
    return pl.pallas_call(
        _outproj_kernel,
        grid=(m // tm,),
        in_specs=[row(d), row(DIFF_WIDTH), row(DIL_WIDTH),
                  pl.BlockSpec(w.shape, lambda i: (0, 0), pipeline_mode=pl.Buffered(1))],
        out_specs=row(d),
        out_shape=jax.ShapeDtypeStruct((m, d), F32),
        compiler_params=pltpu.CompilerParams(
            dimension_semantics=("arbitrary",), vmem_limit_bytes=VMEM_LIMIT),
        name="outproj",
    )(x, oa, ob, w)


def _chunked(w, chunk=256):
    d, f = w.shape
    return w.astype(BF16).reshape(d, f // chunk, chunk).transpose(1, 0, 2)


def kernel(x, ffn1_norm, ffn1_w_gate, ffn1_w_up, ffn1_w_down, mix_norm, w_in, lambda_q1, lambda_k1,
           lambda_q2, lambda_k2, subln_gain, w_out, ffn2_norm, ffn2_w_gate, ffn2_w_up, ffn2_w_down,
           rel_bias, final_norm):
    b, s, d = x.shape
    depth = w_in.shape[0]
    m = b * s
    diff_bias = _diff_bias_tiles(rel_bias)
    dil_bias = _dil_bias_tiles(rel_bias)
    qscale = jnp.concatenate([
        jnp.full((DIFF_WIDTH,), HEAD_DIM ** -0.5 * LOG2E, F32), jnp.ones((2 * DIFF_WIDTH,), F32),
        jnp.full((DIL_WIDTH,), HEAD_DIM ** -0.5 * LOG2E, F32), jnp.ones((2 * DIL_WIDTH,), F32)])
    h = x.reshape(m, d)
    for layer in range(depth):
        row = lambda v: v[layer].reshape(1, -1)
        h = _ffn(h, row(ffn1_norm), _chunked(ffn1_w_gate[layer]), _chunked(ffn1_w_up[layer]),
                 ffn1_w_down[layer].astype(BF16), row(ffn1_norm), final=False)
        qkv = _inproj(h, row(mix_norm), (w_in[layer] * qscale).astype(BF16)).reshape(b, s, -1)
        oa = _diff_attn(qkv, diff_bias, row(lambda_q1), row(lambda_k1), row(lambda_q2), row(lambda_k2),
                        subln_gain[layer].reshape(-1, 1), lam0=_lambda_init(layer))
        ob = _dilated_attn(qkv, dil_bias)
        h = _outproj(h, oa.reshape(m, DIFF_WIDTH), ob.reshape(m, DIL_WIDTH), w_out[layer].astype(BF16))
        last = layer == depth - 1
        h = _ffn(h, row(ffn2_norm), _chunked(ffn2_w_gate[layer]), _chunked(ffn2_w_up[layer]),
                 ffn2_w_down[layer].astype(BF16), final_norm.reshape(1, -1), final=last)
    return h.reshape(b, s, d)
```

```python
import functools
import math

import numpy as np
import jax
import jax.numpy as jnp
from jax import lax
from jax.experimental import pallas as pl
from jax.experimental.pallas import tpu as pltpu

HEAD_DIM = 64
N_DIFF_HEADS = 4
N_DIL_HEADS = 8
DIFF_WIDTH = 512
DIL_WIDTH = 512
DILATED_CONFIGS = ((128, 1), (512, 4), (2048, 16))
NUM_BUCKETS = 32
MAX_DISTANCE = 2048
EPS = 1e-6
NEG = -1e30

LANES = 128
DIL_W = 128
DIL_UNROLL = 8
ATT_TQ = 512
ATT_TK = 512
ATT_R = ATT_TQ // ATT_TK
ATT_PAD = 16
ATT_CW = 256
VMEM_LIMIT = 56 * 1024 * 1024
LOG2E = math.log2(math.e)

F32 = jnp.float32
BF16 = jnp.bfloat16


def _bucket_lower_bounds():
    d = np.arange(0, 2 * MAX_DISTANCE + 1)
    max_exact = NUM_BUCKETS // 2
    v = np.log(np.maximum(d, 1) / max_exact) / math.log(MAX_DISTANCE / max_exact) * (NUM_BUCKETS - max_exact)
    b = np.where(d < max_exact, d, np.minimum(max_exact + v.astype(np.int64), NUM_BUCKETS - 1))
    return [int(np.argmax(b == k)) for k in range(NUM_BUCKETS)]


BUCKET_LO = _bucket_lower_bounds()
N_BIAS_TILES = ATT_R + -(-(BUCKET_LO[-1] + ATT_TK - 1) // ATT_TK)
BIAS_ROLL_W = pl.next_power_of_2(ATT_TQ + ATT_TK)


def _lambda_init(layer):
    return 0.8 - 0.6 * math.exp(-0.3 * layer)


def _bias_of_distance(rb_ref, col, dist):
    val = jnp.full(dist.shape, rb_ref[0, col], F32)
    for b in range(1, NUM_BUCKETS):
        val = jnp.where(dist >= BUCKET_LO[b], rb_ref[b, col], val)
    return val


def _diff_bias_kernel(rb_ref, out_ref):
    h = pl.program_id(0)
    delta = (pl.program_id(1) - (ATT_R - 1)) * ATT_TK
    tq, tk = ATT_TQ, ATT_TK
    u = lax.broadcasted_iota(jnp.int32, (8, BIAS_ROLL_W), 1)
    d = u + (delta - tk)
    for m in range(2):
        val = _bias_of_distance(rb_ref, 2 * h + m, d) * LOG2E
        val = jnp.where(d < 0, NEG, val)
        x = jnp.broadcast_to(val[0:1, :], (tk, BIAS_ROLL_W))
        rolled = pltpu.roll(x, 0, 1, stride=1, stride_axis=0)
        out_ref[0, 0, :, m * tq:(m + 1) * tq] = rolled[:, tk:tk + tq]


def _diff_bias_tiles(rel_bias):
    tq, tk = ATT_TQ, ATT_TK
    return pl.pallas_call(
        _diff_bias_kernel,
        grid=(N_DIFF_HEADS, N_BIAS_TILES),
        in_specs=[pl.BlockSpec(memory_space=pltpu.SMEM)],
        out_specs=pl.BlockSpec((1, 1, tk, 2 * tq), lambda h, d: (h, d, 0, 0)),
        out_shape=jax.ShapeDtypeStruct((N_DIFF_HEADS, N_BIAS_TILES, tk, 2 * tq), F32),
        name="diff_bias_tiles",
    )(rel_bias)


def _dil_bias_kernel(rb_ref, out_ref):
    br = pl.program_id(0)
    h = pl.program_id(1)
    w = DIL_W
    dil = jnp.where(br == 0, DILATED_CONFIGS[0][1],
                    jnp.where(br == 1, DILATED_CONFIGS[1][1], DILATED_CONFIGS[2][1]))
    u = lax.broadcasted_iota(jnp.int32, (8, 4 * w), 1)
    dist = (2 - pl.program_id(2)) * w - u
    val = _bias_of_distance(rb_ref, 2 * N_DIFF_HEADS + h, dist * dil) * LOG2E
    val = jnp.where((dist >= 0) & (dist <= w), val, NEG)
    x = jnp.broadcast_to(val[0:1, :], (w, 4 * w))
    rolled = pltpu.roll(x, 0, 1, stride=1, stride_axis=0)
    out_ref[0, 0, 0] = rolled[:, w:3 * w]


def _dil_bias_tiles(rel_bias):
    w = DIL_W
    nb = len(DILATED_CONFIGS)
    return pl.pallas_call(
        _dil_bias_kernel,
        grid=(nb, N_DIL_HEADS, 2),
        in_specs=[pl.BlockSpec(memory_space=pltpu.SMEM)],
        out_specs=pl.BlockSpec((1, 1, 1, w, 2 * w), lambda b, h, k: (b, h, k, 0, 0)),
        out_shape=jax.ShapeDtypeStruct((nb, N_DIL_HEADS, 2, w, 2 * w), F32),
        name="dil_bias_tiles",
    )(rel_bias)


def _rms(x, g):
    return x * lax.rsqrt(jnp.mean(x * x, axis=-1, keepdims=True) + EPS) * g


def _ffn_kernel(x_ref, g_ref, wg_ref, wu_ref, wd_ref, fg_ref, o_ref, h_ref, *, n_chunks, chunk, final):
    x = x_ref[...]
    xn = _rms(x, g_ref[...]).astype(BF16)
    for c in range(n_chunks):
        gate = jnp.dot(xn, wg_ref[c], preferred_element_type=F32)
        up = jnp.dot(xn, wu_ref[c], preferred_element_type=F32)
        h_ref[:, c * chunk:(c + 1) * chunk] = (gate * jax.nn.sigmoid(gate) * up).astype(BF16)
    y = x + 0.5 * jnp.dot(h_ref[...], wd_ref[...], preferred_element_type=F32)
    if final:
        y = _rms(y, fg_ref[...])
    o_ref[...] = y


def _ffn(x, gain, wg, wu, wd, final_gain, *, final, tm=1024):
    m, d = x.shape
    n_chunks, _, chunk = wg.shape
    dff = n_chunks * chunk
    const = lambda *shape: pl.BlockSpec(shape, lambda i: (0,) * len(shape), pipeline_mode=pl.Buffered(1))
    return pl.pallas_call(
        functools.partial(_ffn_kernel, n_chunks=n_chunks, chunk=chunk, final=final),
        grid=(m // tm,),
        in_specs=[
            pl.BlockSpec((tm, d), lambda i: (i, 0)),
            const(1, d),
            const(n_chunks, d, chunk),
            const(n_chunks, d, chunk),
            const(dff, d),
            const(1, d),
        ],
        out_specs=pl.BlockSpec((tm, d), lambda i: (i, 0)),
        out_shape=jax.ShapeDtypeStruct((m, d), F32),
        scratch_shapes=[pltpu.VMEM((tm, dff), BF16)],
        compiler_params=pltpu.CompilerParams(
            dimension_semantics=("arbitrary",), vmem_limit_bytes=VMEM_LIMIT),
        name="ffn",
    )(x, gain, wg, wu, wd, final_gain)


def _inproj_kernel(x_ref, g_ref, w_ref, o_ref, *, n_chunks, chunk):
    xn = _rms(x_ref[...], g_ref[...]).astype(BF16)
    for c in range(n_chunks):
        sl = slice(c * chunk, (c + 1) * chunk)
        o_ref[:, sl] = jnp.dot(xn, w_ref[:, sl], preferred_element_type=F32).astype(BF16)


def _inproj(x, gain, w, *, tm=1024, chunk=512):
    m, d = x.shape
    n = w.shape[1]
    return pl.pallas_call(
        functools.partial(_inproj_kernel, n_chunks=n // chunk, chunk=chunk),
        grid=(m // tm,),
        in_specs=[
            pl.BlockSpec((tm, d), lambda i: (i, 0)),
            pl.BlockSpec((1, d), lambda i: (0, 0), pipeline_mode=pl.Buffered(1)),
            pl.BlockSpec((d, n), lambda i: (0, 0), pipeline_mode=pl.Buffered(1)),
        ],
        out_specs=pl.BlockSpec((tm, n), lambda i: (i, 0)),
        out_shape=jax.ShapeDtypeStruct((m, n), BF16),
        compiler_params=pltpu.CompilerParams(
            dimension_semantics=("arbitrary",), vmem_limit_bytes=VMEM_LIMIT),
        name="inproj",
    )(x, gain, w)


def _diff_attn_kernel(q_ref, k_ref, v_ref, bias_ref, lq1_ref, lk1_ref, lq2_ref, lk2_ref, sg_ref,
                      o_ref, vt_ref, qs_ref, acc_ref, m_ref, al_ref, mxa_ref, mxb_ref,
                      sa_ref, sb_ref, ea_ref, eb_ref, *, lam0, n_kv):
    tq, tk = ATT_TQ, ATT_TK
    hw = 2 * HEAD_DIM
    qi = pl.program_id(2)
    n_tiles = ATT_R * (qi + 1)

    @pl.when(qi == 0)
    def _():
        ones_row = (lax.broadcasted_iota(jnp.int32, (ATT_PAD, tk), 0) == 0).astype(BF16)
        for c in range(n_kv):
            vt_ref[c, 0:hw, :] = v_ref[0, c * tk:(c + 1) * tk, :].astype(F32).T.astype(BF16)
            vt_ref[c, hw:, :] = ones_row

    qt = q_ref[0].astype(F32).T
    row = lax.broadcasted_iota(jnp.int32, qt.shape, 0)
    qs_ref[...] = jnp.concatenate(
        [jnp.where(row < HEAD_DIM, qt, 0.0), jnp.where(row >= HEAD_DIM, qt, 0.0)], axis=1).astype(BF16)

    cw = ATT_CW
    chunks = [slice(c * cw, (c + 1) * cw) for c in range(2 * tq // cw)]

    def score_chunk(j, cols, s_ref, mx_ref):
        kblk = k_ref[0, pl.ds(pl.multiple_of(j * tk, tk), tk), :]
        bi = jnp.minimum(ATT_R * (qi + 1) - 1 - j, N_BIAS_TILES - 1)
        t = jnp.dot(kblk, qs_ref[:, cols], preferred_element_type=F32) + bias_ref[0, bi, :, cols]
        s_ref[:, cols] = t
        mx_ref[:, cols] = jnp.max(t, axis=0, keepdims=True)

    def pv_chunk(j, cols, e_ref):
        acc_ref[:, cols] = (al_ref[:, cols] * acc_ref[:, cols]
                            + jnp.dot(vt_ref[j], e_ref[:, cols], preferred_element_type=F32))

    def softmax_chunk(cols, s_ref, mx_ref, e_ref):
        m_old = m_ref[:, cols]
        m_new = jnp.maximum(m_old, mx_ref[:, cols])
        al_ref[:, cols] = jnp.exp2(m_old - m_new)
        m_ref[:, cols] = m_new
        e_ref[:, cols] = jnp.exp2(s_ref[:, cols] - m_new).astype(BF16)

    def phase(j, cur, nxt, last=False):
        jn = jnp.minimum(j + 1, n_tiles - 1)
        jp = jnp.maximum(j - 1, 0)
        for cols in chunks:
            if not last:
                score_chunk(jn, cols, nxt[0], nxt[1])
            pv_chunk(jp, cols, nxt[2])
            softmax_chunk(cols, *cur)

    buf_a = (sa_ref, mxa_ref, ea_ref)
    buf_b = (sb_ref, mxb_ref, eb_ref)
    m_ref[...] = jnp.full(m_ref.shape, NEG, F32)
    al_ref[...] = jnp.ones(al_ref.shape, F32)
    acc_ref[...] = jnp.zeros(acc_ref.shape, F32)
    eb_ref[...] = jnp.zeros(eb_ref.shape, BF16)
    for cols in chunks:
        score_chunk(0, cols, sa_ref, mxa_ref)

    def body(i, carry):
        phase(2 * i, buf_a, buf_b)
        phase(2 * i + 1, buf_b, buf_a)
        return carry

    lax.fori_loop(0, n_tiles // 2, body, 0)

    @pl.when(n_tiles % 2 == 1)
    def _():
        phase(n_tiles - 1, buf_a, buf_b, last=True)
        for cols in chunks:
            pv_chunk(n_tiles - 1, cols, ea_ref)

    @pl.when(n_tiles % 2 == 0)
    def _():
        for cols in chunks:
            pv_chunk(n_tiles - 1, cols, eb_ref)

    lam = (jnp.exp(jnp.sum(lq1_ref[...] * lk1_ref[...])) - jnp.exp(jnp.sum(lq2_ref[...] * lk2_ref[...])) + lam0)
    o = acc_ref[0:hw, :] / acc_ref[hw:hw + 1, :]
    o = o[:, :tq] - lam * o[:, tq:]
    y = o * lax.rsqrt(jnp.mean(o * o, axis=0, keepdims=True) + EPS) * sg_ref[...]
    y = y * (1.0 - lam0)
    o_ref[0] = y.T.astype(BF16)


def _diff_attn(qkv, bias_tiles, lq1, lk1, lq2, lk2, sub_gain, *, lam0):
    b, s, _ = qkv.shape
    tq, tk = ATT_TQ, ATT_TK
    hw = 2 * HEAD_DIM
    small = lambda shape: pl.BlockSpec(shape, lambda h, bb, i: (0,) * len(shape))
    return pl.pallas_call(
        functools.partial(_diff_attn_kernel, lam0=lam0, n_kv=s // tk),
        grid=(N_DIFF_HEADS, b, s // tq),
        in_specs=[
            pl.BlockSpec((1, tq, hw), lambda h, bb, i: (bb, i, h)),
            pl.BlockSpec((1, s, hw), lambda h, bb, i: (bb, 0, N_DIFF_HEADS + h)),
            pl.BlockSpec((1, s, hw), lambda h, bb, i: (bb, 0, 2 * N_DIFF_HEADS + h)),
            pl.BlockSpec((1, N_BIAS_TILES, tk, 2 * tq), lambda h, bb, i: (h, 0, 0, 0),
                         pipeline_mode=pl.Buffered(1)),
            small((1, HEAD_DIM)), small((1, HEAD_DIM)), small((1, HEAD_DIM)), small((1, HEAD_DIM)),
            small((hw, 1)),
        ],
        out_specs=pl.BlockSpec((1, tq, hw), lambda h, bb, i: (bb, i, h)),
        out_shape=jax.ShapeDtypeStruct((b, s, DIFF_WIDTH), BF16),
        scratch_shapes=[
            pltpu.VMEM((s // tk, hw + ATT_PAD, tk), BF16),
            pltpu.VMEM((hw, 2 * tq), BF16),
            pltpu.VMEM((hw + ATT_PAD, 2 * tq), F32),
            pltpu.VMEM((1, 2 * tq), F32),
            pltpu.VMEM((1, 2 * tq), F32),
            pltpu.VMEM((1, 2 * tq), F32),
            pltpu.VMEM((1, 2 * tq), F32),
            pltpu.VMEM((tk, 2 * tq), F32),
            pltpu.VMEM((tk, 2 * tq), F32),
            pltpu.VMEM((tk, 2 * tq), BF16),
            pltpu.VMEM((tk, 2 * tq), BF16),
        ],
        compiler_params=pltpu.CompilerParams(
            dimension_semantics=("arbitrary", "arbitrary", "arbitrary"), vmem_limit_bytes=VMEM_LIMIT),
        name="diff_attn",
    )(qkv, qkv, qkv, bias_tiles, lq1, lk1, lq2, lk2, sub_gain)


def _dilated_kernel(q_ref, k_ref, v_ref, bias_ref, o_ref, xf_ref, x4_ref, nat_ref):
    w = DIL_W
    s_len = q_ref.shape[1]
    l4 = s_len // 4
    srcs = (q_ref, k_ref, v_ref)
    cp = 512
    for t in range(3):
        for c in range(s_len // cp):
            xf_ref[t, c * cp:(c + 1) * cp, :] = srcs[t][0, c * cp:(c + 1) * cp, :].astype(F32)
    for t in range(3):
        for r in range(4):
            for c in range(l4 // cp):
                x4_ref[t, r * l4 + c * cp:r * l4 + (c + 1) * cp, :] = (
                    xf_ref[t, pl.ds(r + 4 * c * cp, cp, stride=4), :])

    low = lax.broadcasted_iota(jnp.int32, (w, LANES), 1) < HEAD_DIM
    nt = (((1,), (1,)), ((), ()))

    def rows(t, br, r, j0, n):
        if br == 0:
            return srcs[t][0, pl.ds(pl.multiple_of(j0, w), n), :]
        if br == 1:
            return x4_ref[t, pl.ds(pl.multiple_of(r * l4 + j0, w), n), :].astype(BF16)
        return x4_ref[t, pl.ds((r % 4) * l4 + r // 4 + 4 * j0, n, stride=4), :].astype(BF16)

    def attend(br, r, n):
        k0 = jnp.maximum(n - 1, 0) * w
        kind = jnp.where(n == 0, 1, 0)
        qp = rows(0, br, r, n * w, w)
        kk = rows(1, br, r, k0, 2 * w)
        vv = jnp.concatenate([rows(2, br, r, k0, 2 * w), jnp.ones((2 * w, LANES), BF16)], axis=1)
        pvs, ms = [], []
        for hh in range(2):
            qm = jnp.where(low if hh == 0 else jnp.logical_not(low), qp, jnp.zeros_like(qp))
            s = lax.dot_general(qm, kk, nt, preferred_element_type=F32) + bias_ref[br, hh, kind]
            m = jnp.max(s, axis=1, keepdims=True)
            e = jnp.exp2((s - m).astype(BF16))
            pvs.append(jnp.dot(e, vv, preferred_element_type=F32))
            ms.append(m)
        dil = DILATED_CONFIGS[br][1]
        start = r + dil * w * n
        dst = pl.ds(pl.multiple_of(start, w), w) if dil == 1 else pl.ds(start, w, stride=dil)
        nat_ref[br, 0, dst, :] = jnp.where(low, pvs[0][:, :LANES], pvs[1][:, :LANES])
        nat_ref[br, 1, dst, :] = jnp.where(low, ms[0], ms[1])
        nat_ref[br, 2, dst, :] = jnp.where(low, pvs[0][:, LANES:], pvs[1][:, LANES:])

    for br, (_, dil) in enumerate(DILATED_CONFIGS):
        nblk = s_len // dil // w

        def blocks(i, carry, br=br, nblk=nblk):
            for u in range(DIL_UNROLL):
                idx = i * DIL_UNROLL + u
                attend(br, idx // nblk, idx % nblk)
            return carry

        lax.fori_loop(0, dil * nblk // DIL_UNROLL, blocks, 0)

    for c in range(s_len // cp):
        sl = slice(c * cp, (c + 1) * cp)
        ms = [nat_ref[br, 1, sl, :] for br in range(3)]
        top = jnp.maximum(jnp.maximum(ms[0], ms[1]), ms[2])
        wts = [jnp.exp2(v - top) for v in ms]
        num = wts[0] * nat_ref[0, 0, sl, :] + wts[1] * nat_ref[1, 0, sl, :] + wts[2] * nat_ref[2, 0, sl, :]
        den = wts[0] * nat_ref[0, 2, sl, :] + wts[1] * nat_ref[1, 2, sl, :] + wts[2] * nat_ref[2, 2, sl, :]
        o_ref[0, sl, :] = (num / den).astype(BF16)


def _dilated_attn(qkv, bias):
    b, s, _ = qkv.shape
    w = DIL_W
    col0 = 3 * DIFF_WIDTH // LANES
    pairs = DIL_WIDTH // LANES
    part = lambda t: pl.BlockSpec((1, s, LANES), lambda bb, p: (bb, 0, col0 + t * pairs + p))
    return pl.pallas_call(
        _dilated_kernel,
        grid=(b, pairs),
        in_specs=[part(0), part(1), part(2),
                  pl.BlockSpec((len(DILATED_CONFIGS), 2, 2, w, 2 * w), lambda bb, p: (0, p, 0, 0, 0))],
        out_specs=pl.BlockSpec((1, s, LANES), lambda bb, p: (bb, 0, p)),
        out_shape=jax.ShapeDtypeStruct((b, s, DIL_WIDTH), BF16),
        scratch_shapes=[
            pltpu.VMEM((3, s, LANES), F32),
            pltpu.VMEM((3, s, LANES), F32),
            pltpu.VMEM((len(DILATED_CONFIGS), 3, s, LANES), F32),
        ],
        compiler_params=pltpu.CompilerParams(
            dimension_semantics=("arbitrary", "arbitrary"), vmem_limit_bytes=VMEM_LIMIT),
        name="dilated_attn",
    )(qkv, qkv, qkv, bias)


def _outproj_kernel(x_ref, oa_ref, ob_ref, w_ref, o_ref):
    y = jnp.dot(oa_ref[...], w_ref[0:DIFF_WIDTH, :], preferred_element_type=F32)
    y = y + jnp.dot(ob_ref[...], w_ref[DIFF_WIDTH:, :], preferred_element_type=F32)
    o_ref[...] = x_ref[...] + y


def _outproj(x, oa, ob, w, *, tm=512):
    m, d = x.shape
    row = lambda n: pl.BlockSpec((tm, n), lambda i: (i, 0))
    return pl.pallas_call(
        _outproj_kernel,
        grid=(m // tm,),
        in_specs=[row(d), row(DIFF_WIDTH), row(DIL_WIDTH),
                  pl.BlockSpec(w.shape, lambda i: (0, 0), pipeline_mode=pl.Buffered(1))],
        out_specs=row(d),
        out_shape=jax.ShapeDtypeStruct((m, d), F32),
        compiler_params=pltpu.CompilerParams(
            dimension_semantics=("arbitrary",), vmem_limit_bytes=VMEM_LIMIT),
        name="outproj",
    )(x, oa, ob, w)


def _chunked(w, chunk=256):
    d, f = w.shape
    return w.astype(BF16).reshape(d, f // chunk, chunk).transpose(1, 0, 2)


def kernel(x, ffn1_norm, ffn1_w_gate, ffn1_w_up, ffn1_w_down, mix_norm, w_in, lambda_q1, lambda_k1,
           lambda_q2, lambda_k2, subln_gain, w_out, ffn2_norm, ffn2_w_gate, ffn2_w_up, ffn2_w_down,
           rel_bias, final_norm):
    b, s, d = x.shape
    depth = w_in.shape[0]
    m = b * s
    diff_bias = _diff_bias_tiles(rel_bias)
    dil_bias = _dil_bias_tiles(rel_bias)
    qscale = jnp.concatenate([
        jnp.full((DIFF_WIDTH,), HEAD_DIM ** -0.5 * LOG2E, F32), jnp.ones((2 * DIFF_WIDTH,), F32),
        jnp.full((DIL_WIDTH,), HEAD_DIM ** -0.5 * LOG2E, F32), jnp.ones((2 * DIL_WIDTH,), F32)])
    h = x.reshape(m, d)
    for layer in range(depth):
        row = lambda v: v[layer].reshape(1, -1)
        h = _ffn(h, row(ffn1_norm), _chunked(ffn1_w_gate[layer]), _chunked(ffn1_w_up[layer]),
                 ffn1_w_down[layer].astype(BF16), row(ffn1_norm), final=False)
        qkv = _inproj(h, row(mix_norm), (w_in[layer] * qscale).astype(BF16)).reshape(b, s, -1)
        oa = _diff_attn(qkv, diff_bias, row(lambda_q1), row(lambda_k1), row(lambda_q2), row(lambda_k2),
                        subln_gain[layer].reshape(-1, 1), lam0=_lambda_init(layer))
        ob = _dilated_attn(qkv, dil_bias)
        h = _outproj(h, oa.reshape(m, DIFF_WIDTH), ob.reshape(m, DIL_WIDTH), w_out[layer].astype(BF16))
        last = layer == depth - 1
        h = _ffn(h, row(ffn2_norm), _chunked(ffn2_w_gate[layer]), _chunked(ffn2_w_up[layer]),
                 ffn2_w_down[layer].astype(BF16), final_norm.reshape(1, -1), final=last)
    return h.reshape(b, s, d)
```

```python
import functools
import math

import numpy as np
import jax
import jax.numpy as jnp
from jax import lax
from jax.experimental import pallas as pl
from jax.experimental.pallas import tpu as pltpu

HEAD_DIM = 64
N_DIFF_HEADS = 4
N_DIL_HEADS = 8
DIFF_WIDTH = 512
DIL_WIDTH = 512
DILATED_CONFIGS = ((128, 1), (512, 4), (2048, 16))
NUM_BUCKETS = 32
MAX_DISTANCE = 2048
EPS = 1e-6
NEG = -1e30

LANES = 128
DIL_W = 128
DIL_UNROLL = 8
ATT_TQ = 512
ATT_TK = 512
ATT_R = ATT_TQ // ATT_TK
ATT_PAD = 16
ATT_CW = 256
VMEM_LIMIT = 56 * 1024 * 1024
LOG2E = math.log2(math.e)

F32 = jnp.float32
BF16 = jnp.bfloat16


def _bucket_lower_bounds():
    d = np.arange(0, 2 * MAX_DISTANCE + 1)
    max_exact = NUM_BUCKETS // 2
    v = np.log(np.maximum(d, 1) / max_exact) / math.log(MAX_DISTANCE / max_exact) * (NUM_BUCKETS - max_exact)
    b = np.where(d < max_exact, d, np.minimum(max_exact + v.astype(np.int64), NUM_BUCKETS - 1))
    return [int(np.argmax(b == k)) for k in range(NUM_BUCKETS)]


BUCKET_LO = _bucket_lower_bounds()
N_BIAS_TILES = ATT_R + -(-(BUCKET_LO[-1] + ATT_TK - 1) // ATT_TK)
BIAS_ROLL_W = pl.next_power_of_2(ATT_TQ + ATT_TK)


def _lambda_init(layer):
    return 0.8 - 0.6 * math.exp(-0.3 * layer)


def _bias_of_distance(rb_ref, col, dist):
    val = jnp.full(dist.shape, rb_ref[0, col], F32)
    for b in range(1, NUM_BUCKETS):
        val = jnp.where(dist >= BUCKET_LO[b], rb_ref[b, col], val)
    return val


def _diff_bias_kernel(rb_ref, out_ref):
    h = pl.program_id(0)
    delta = (pl.program_id(1) - (ATT_R - 1)) * ATT_TK
    tq, tk = ATT_TQ, ATT_TK
    u = lax.broadcasted_iota(jnp.int32, (8, BIAS_ROLL_W), 1)
    d = u + (delta - tk)
    for m in range(2):
        val = _bias_of_distance(rb_ref, 2 * h + m, d) * LOG2E
        val = jnp.where(d < 0, NEG, val)
        x = jnp.broadcast_to(val[0:1, :], (tk, BIAS_ROLL_W))
        rolled = pltpu.roll(x, 0, 1, stride=1, stride_axis=0)
        out_ref[0, 0, :, m * tq:(m + 1) * tq] = rolled[:, tk:tk + tq]


def _diff_bias_tiles(rel_bias):
    tq, tk = ATT_TQ, ATT_TK
    return pl.pallas_call(
        _diff_bias_kernel,
        grid=(N_DIFF_HEADS, N_BIAS_TILES),
        in_specs=[pl.BlockSpec(memory_space=pltpu.SMEM)],
        out_specs=pl.BlockSpec((1, 1, tk, 2 * tq), lambda h, d: (h, d, 0, 0)),
        out_shape=jax.ShapeDtypeStruct((N_DIFF_HEADS, N_BIAS_TILES, tk, 2 * tq), F32),
        name="diff_bias_tiles",
    )(rel_bias)


def _dil_bias_kernel(rb_ref, out_ref):
    br = pl.program_id(0)
    h = pl.program_id(1)
    w = DIL_W
    dil = jnp.where(br == 0, DILATED_CONFIGS[0][1],
                    jnp.where(br == 1, DILATED_CONFIGS[1][1], DILATED_CONFIGS[2][1]))
    u = lax.broadcasted_iota(jnp.int32, (8, 4 * w), 1)
    dist = (2 - pl.program_id(2)) * w - u
    val = _bias_of_distance(rb_ref, 2 * N_DIFF_HEADS + h, dist * dil) * LOG2E
    val = jnp.where((dist >= 0) & (dist <= w), val, NEG)
    x = jnp.broadcast_to(val[0:1, :], (w, 4 * w))
    rolled = pltpu.roll(x, 0, 1, stride=1, stride_axis=0)
    out_ref[0, 0, 0] = rolled[:, w:3 * w]


def _dil_bias_tiles(rel_bias):
    w = DIL_W
    nb = len(DILATED_CONFIGS)
    return pl.pallas_call(
        _dil_bias_kernel,
        grid=(nb, N_DIL_HEADS, 2),
        in_specs=[pl.BlockSpec(memory_space=pltpu.SMEM)],
        out_specs=pl.BlockSpec((1, 1, 1, w, 2 * w), lambda b, h, k: (b, h, k, 0, 0)),
        out_shape=jax.ShapeDtypeStruct((nb, N_DIL_HEADS, 2, w, 2 * w), F32),
        name="dil_bias_tiles",
    )(rel_bias)


def _rms(x, g):
    return x * lax.rsqrt(jnp.mean(x * x, axis=-1, keepdims=True) + EPS) * g


def _ffn_kernel(*refs, chunk, final, mixed):
    if mixed:
        x_ref, oa_ref, ob_ref, wo_ref, g_ref, wg_ref, wu_ref, wd_ref, fg_ref, o_ref, h_ref = refs
        x = (x_ref[...] + jnp.dot(oa_ref[...], wo_ref[0:DIFF_WIDTH, :], preferred_element_type=F32)
             + jnp.dot(ob_ref[...], wo_ref[DIFF_WIDTH:, :], preferred_element_type=F32))
    else:
        x_ref, g_ref, wg_ref, wu_ref, wd_ref, fg_ref, o_ref, h_ref = refs
        x = x_ref[...]
    xn = _rms(x, g_ref[...]).astype(BF16)
    for c in range(wg_ref.shape[1] // chunk):
        sl = slice(c * chunk, (c + 1) * chunk)
        gate = jnp.dot(xn, wg_ref[:, sl], preferred_element_type=F32)
        up = jnp.dot(xn, wu_ref[:, sl], preferred_element_type=F32)
        h_ref[:, sl] = (gate * jax.nn.sigmoid(gate) * up).astype(BF16)
    y = x + 0.5 * jnp.dot(h_ref[...], wd_ref[...], preferred_element_type=F32)
    if final:
        y = _rms(y, fg_ref[...])
    o_ref[...] = y


def _ffn(x, gain, wg, wu, wd, final_gain, *, final, mix=None, tm=1024, chunk=256):
    m, d = x.shape
    dff = wg.shape[1]
    const = lambda *shape: pl.BlockSpec(shape, lambda i: (0,) * len(shape), pipeline_mode=pl.Buffered(1))
    row = lambda n: pl.BlockSpec((tm, n), lambda i: (i, 0))
    mix_specs = [] if mix is None else [row(mix[0].shape[1]), row(mix[1].shape[1]), const(*mix[2].shape)]
    return pl.pallas_call(
        functools.partial(_ffn_kernel, chunk=chunk, final=final, mixed=mix is not None),
        grid=(m // tm,),
        in_specs=[row(d)] + mix_specs + [const(1, d), const(d, dff), const(d, dff), const(dff, d), const(1, d)],
        out_specs=row(d),
        out_shape=jax.ShapeDtypeStruct((m, d), F32),
        scratch_shapes=[pltpu.VMEM((tm, dff), BF16)],
        compiler_params=pltpu.CompilerParams(
            dimension_semantics=("arbitrary",), vmem_limit_bytes=VMEM_LIMIT),
        name="ffn_mix" if mix is not None else "ffn",
    )(x, *(mix or ()), gain, wg, wu, wd, final_gain)


def _inproj_kernel(x_ref, g_ref, w_ref, o_ref, *, n_chunks, chunk):
    xn = _rms(x_ref[...], g_ref[...]).astype(BF16)
    for c in range(n_chunks):
        sl = slice(c * chunk, (c + 1) * chunk)
        o_ref[:, sl] = jnp.dot(xn, w_ref[:, sl], preferred_element_type=F32).astype(BF16)


def _inproj(x, gain, w, *, tm=1024, chunk=512):
    m, d = x.shape
    n = w.shape[1]
    return pl.pallas_call(
        functools.partial(_inproj_kernel, n_chunks=n // chunk, chunk=chunk),
        grid=(m // tm,),
        in_specs=[
            pl.BlockSpec((tm, d), lambda i: (i, 0)),
            pl.BlockSpec((1, d), lambda i: (0, 0), pipeline_mode=pl.Buffered(1)),
            pl.BlockSpec((d, n), lambda i: (0, 0), pipeline_mode=pl.Buffered(1)),
        ],
        out_specs=pl.BlockSpec((tm, n), lambda i: (i, 0)),
        out_shape=jax.ShapeDtypeStruct((m, n), BF16),
        compiler_params=pltpu.CompilerParams(
            dimension_semantics=("arbitrary",), vmem_limit_bytes=VMEM_LIMIT),
        name="inproj",
    )(x, gain, w)


def _diff_attn_kernel(q_ref, k_ref, v_ref, bias_ref, lq1_ref, lk1_ref, lq2_ref, lk2_ref, sg_ref,
                      o_ref, vt_ref, qs_ref, acc_ref, m_ref, al_ref, mxa_ref, mxb_ref,
                      sa_ref, sb_ref, ea_ref, eb_ref, *, lam0, n_kv):
    tq, tk = ATT_TQ, ATT_TK
    hw = 2 * HEAD_DIM
    qi = pl.program_id(2)
    n_tiles = ATT_R * (qi + 1)

    @pl.when(qi == 0)
    def _():
        ones_row = (lax.broadcasted_iota(jnp.int32, (ATT_PAD, tk), 0) == 0).astype(BF16)
        for c in range(n_kv):
            vt_ref[c, 0:hw, :] = v_ref[0, c * tk:(c + 1) * tk, :].astype(F32).T.astype(BF16)
            vt_ref[c, hw:, :] = ones_row

    qt = q_ref[0].astype(F32).T
    row = lax.broadcasted_iota(jnp.int32, qt.shape, 0)
    qs_ref[...] = jnp.concatenate(
        [jnp.where(row < HEAD_DIM, qt, 0.0), jnp.where(row >= HEAD_DIM, qt, 0.0)], axis=1).astype(BF16)

    cw = ATT_CW
    chunks = [slice(c * cw, (c + 1) * cw) for c in range(2 * tq // cw)]

    def score_chunk(j, cols, s_ref, mx_ref):
        kblk = k_ref[0, pl.ds(pl.multiple_of(j * tk, tk), tk), :]
        bi = jnp.minimum(ATT_R * (qi + 1) - 1 - j, N_BIAS_TILES - 1)
        t = jnp.dot(kblk, qs_ref[:, cols], preferred_element_type=F32) + bias_ref[0, bi, :, cols]
        s_ref[:, cols] = t
        mx_ref[:, cols] = jnp.max(t, axis=0, keepdims=True)

    def pv_chunk(j, cols, e_ref):
        acc_ref[:, cols] = (al_ref[:, cols] * acc_ref[:, cols]
                            + jnp.dot(vt_ref[j], e_ref[:, cols], preferred_element_type=F32))

    def softmax_chunk(cols, s_ref, mx_ref, e_ref):
        m_old = m_ref[:, cols]
        m_new = jnp.maximum(m_old, mx_ref[:, cols])
        al_ref[:, cols] = jnp.exp2(m_old - m_new)
        m_ref[:, cols] = m_new
        e_ref[:, cols] = jnp.exp2(s_ref[:, cols] - m_new).astype(BF16)

    def phase(j, cur, nxt, last=False):
        jn = jnp.minimum(j + 1, n_tiles - 1)
        jp = jnp.maximum(j - 1, 0)
        for cols in chunks:
            if not last:
                score_chunk(jn, cols, nxt[0], nxt[1])
            pv_chunk(jp, cols, nxt[2])
            softmax_chunk(cols, *cur)

    buf_a = (sa_ref, mxa_ref, ea_ref)
    buf_b = (sb_ref, mxb_ref, eb_ref)
    m_ref[...] = jnp.full(m_ref.shape, NEG, F32)
    al_ref[...] = jnp.ones(al_ref.shape, F32)
    acc_ref[...] = jnp.zeros(acc_ref.shape, F32)
    eb_ref[...] = jnp.zeros(eb_ref.shape, BF16)
    for cols in chunks:
        score_chunk(0, cols, sa_ref, mxa_ref)

    def body(i, carry):
        phase(2 * i, buf_a, buf_b)
        phase(2 * i + 1, buf_b, buf_a)
        return carry

    lax.fori_loop(0, n_tiles // 2, body, 0)

    @pl.when(n_tiles % 2 == 1)
    def _():
        phase(n_tiles - 1, buf_a, buf_b, last=True)
        for cols in chunks:
            pv_chunk(n_tiles - 1, cols, ea_ref)

    @pl.when(n_tiles % 2 == 0)
    def _():
        for cols in chunks:
            pv_chunk(n_tiles - 1, cols, eb_ref)

    lam = (jnp.exp(jnp.sum(lq1_ref[...] * lk1_ref[...])) - jnp.exp(jnp.sum(lq2_ref[...] * lk2_ref[...])) + lam0)
    o = acc_ref[0:hw, :] / acc_ref[hw:hw + 1, :]
    o = o[:, :tq] - lam * o[:, tq:]
    y = o * lax.rsqrt(jnp.mean(o * o, axis=0, keepdims=True) + EPS) * sg_ref[...]
    y = y * (1.0 - lam0)
    o_ref[0] = y.T.astype(BF16)


def _diff_attn(qkv, bias_tiles, lq1, lk1, lq2, lk2, sub_gain, *, lam0):
    b, s, _ = qkv.shape
    tq, tk = ATT_TQ, ATT_TK
    hw = 2 * HEAD_DIM
    small = lambda shape: pl.BlockSpec(shape, lambda h, bb, i: (0,) * len(shape))
    return pl.pallas_call(
        functools.partial(_diff_attn_kernel, lam0=lam0, n_kv=s // tk),
        grid=(N_DIFF_HEADS, b, s // tq),
        in_specs=[
            pl.BlockSpec((1, tq, hw), lambda h, bb, i: (bb, i, h)),
            pl.BlockSpec((1, s, hw), lambda h, bb, i: (bb, 0, N_DIFF_HEADS + h)),
            pl.BlockSpec((1, s, hw), lambda h, bb, i: (bb, 0, 2 * N_DIFF_HEADS + h)),
            pl.BlockSpec((1, N_BIAS_TILES, tk, 2 * tq), lambda h, bb, i: (h, 0, 0, 0),
                         pipeline_mode=pl.Buffered(1)),
            small((1, HEAD_DIM)), small((1, HEAD_DIM)), small((1, HEAD_DIM)), small((1, HEAD_DIM)),
            small((hw, 1)),
        ],
        out_specs=pl.BlockSpec((1, tq, hw), lambda h, bb, i: (bb, i, h)),
        out_shape=jax.ShapeDtypeStruct((b, s, DIFF_WIDTH), BF16),
        scratch_shapes=[
            pltpu.VMEM((s // tk, hw + ATT_PAD, tk), BF16),
            pltpu.VMEM((hw, 2 * tq), BF16),
            pltpu.VMEM((hw + ATT_PAD, 2 * tq), F32),
            pltpu.VMEM((1, 2 * tq), F32),
            pltpu.VMEM((1, 2 * tq), F32),
            pltpu.VMEM((1, 2 * tq), F32),
            pltpu.VMEM((1, 2 * tq), F32),
            pltpu.VMEM((tk, 2 * tq), F32),
            pltpu.VMEM((tk, 2 * tq), F32),
            pltpu.VMEM((tk, 2 * tq), BF16),
            pltpu.VMEM((tk, 2 * tq), BF16),
        ],
        compiler_params=pltpu.CompilerParams(
            dimension_semantics=("arbitrary", "arbitrary", "arbitrary"), vmem_limit_bytes=VMEM_LIMIT),
        name="diff_attn",
    )(qkv, qkv, qkv, bias_tiles, lq1, lk1, lq2, lk2, sub_gain)


def _dilated_kernel(q_ref, k_ref, v_ref, bias_ref, o_ref, xf_ref, x4_ref, nat_ref):
    w = DIL_W
    s_len = q_ref.shape[1]
    l4 = s_len // 4
    srcs = (q_ref, k_ref, v_ref)
    cp = 512
    for t in range(3):
        for c in range(s_len // cp):
            xf_ref[t, c * cp:(c + 1) * cp, :] = srcs[t][0, c * cp:(c + 1) * cp, :].astype(F32)
    for t in range(3):
        for r in range(4):
            for c in range(l4 // cp):
                x4_ref[t, r * l4 + c * cp:r * l4 + (c + 1) * cp, :] = (
                    xf_ref[t, pl.ds(r + 4 * c * cp, cp, stride=4), :])

    low = lax.broadcasted_iota(jnp.int32, (w, LANES), 1) < HEAD_DIM
    nt = (((1,), (1,)), ((), ()))

    def rows(t, br, r, j0, n):
        if br == 0:
            return srcs[t][0, pl.ds(pl.multiple_of(j0, w), n), :]
        if br == 1:
            return x4_ref[t, pl.ds(pl.multiple_of(r * l4 + j0, w), n), :].astype(BF16)
        return x4_ref[t, pl.ds((r % 4) * l4 + r // 4 + 4 * j0, n, stride=4), :].astype(BF16)

    def attend(br, r, n):
        k0 = jnp.maximum(n - 1, 0) * w
        kind = jnp.where(n == 0, 1, 0)
        qp = rows(0, br, r, n * w, w)
        kk = rows(1, br, r, k0, 2 * w)
        vv = jnp.concatenate([rows(2, br, r, k0, 2 * w), jnp.ones((2 * w, LANES), BF16)], axis=1)
        pvs, ms = [], []
        for hh in range(2):
            qm = jnp.where(low if hh == 0 else jnp.logical_not(low), qp, jnp.zeros_like(qp))
            s = lax.dot_general(qm, kk, nt, preferred_element_type=F32) + bias_ref[br, hh, kind]
            m = jnp.max(s, axis=1, keepdims=True)
            e = jnp.exp2((s - m).astype(BF16))
            pvs.append(jnp.dot(e, vv, preferred_element_type=F32))
            ms.append(m)
        dil = DILATED_CONFIGS[br][1]
        start = r + dil * w * n
        dst = pl.ds(pl.multiple_of(start, w), w) if dil == 1 else pl.ds(start, w, stride=dil)
        nat_ref[br, 0, dst, :] = jnp.where(low, pvs[0][:, :LANES], pvs[1][:, :LANES])
        nat_ref[br, 1, dst, :] = jnp.where(low, ms[0], ms[1])
        nat_ref[br, 2, dst, :] = jnp.where(low, pvs[0][:, LANES:], pvs[1][:, LANES:])

    for br, (_, dil) in enumerate(DILATED_CONFIGS):
        nblk = s_len // dil // w

        def blocks(i, carry, br=br, nblk=nblk):
            for u in range(DIL_UNROLL):
                idx = i * DIL_UNROLL + u
                attend(br, idx // nblk, idx % nblk)
            return carry

        lax.fori_loop(0, dil * nblk // DIL_UNROLL, blocks, 0)

    for c in range(s_len // cp):
        sl = slice(c * cp, (c + 1) * cp)
        ms = [nat_ref[br, 1, sl, :] for br in range(3)]
        top = jnp.maximum(jnp.maximum(ms[0], ms[1]), ms[2])
        wts = [jnp.exp2(v - top) for v in ms]
        num = wts[0] * nat_ref[0, 0, sl, :] + wts[1] * nat_ref[1, 0, sl, :] + wts[2] * nat_ref[2, 0, sl, :]
        den = wts[0] * nat_ref[0, 2, sl, :] + wts[1] * nat_ref[1, 2, sl, :] + wts[2] * nat_ref[2, 2, sl, :]
        o_ref[0, sl, :] = (num / den).astype(BF16)


def _dilated_attn(qkv, bias):
    b, s, _ = qkv.shape
    w = DIL_W
    col0 = 3 * DIFF_WIDTH // LANES
    pairs = DIL_WIDTH // LANES
    part = lambda t: pl.BlockSpec((1, s, LANES), lambda bb, p: (bb, 0, col0 + t * pairs + p))
    return pl.pallas_call(
        _dilated_kernel,
        grid=(b, pairs),
        in_specs=[part(0), part(1), part(2),
                  pl.BlockSpec((len(DILATED_CONFIGS), 2, 2, w, 2 * w), lambda bb, p: (0, p, 0, 0, 0))],
        out_specs=pl.BlockSpec((1, s, LANES), lambda bb, p: (bb, 0, p)),
        out_shape=jax.ShapeDtypeStruct((b, s, DIL_WIDTH), BF16),
        scratch_shapes=[
            pltpu.VMEM((3, s, LANES), F32),
            pltpu.VMEM((3, s, LANES), F32),
            pltpu.VMEM((len(DILATED_CONFIGS), 3, s, LANES), F32),
        ],
        compiler_params=pltpu.CompilerParams(
            dimension_semantics=("arbitrary", "arbitrary"), vmem_limit_bytes=VMEM_LIMIT),
        name="dilated_attn",
    )(qkv, qkv, qkv, bias)


def kernel(x, ffn1_norm, ffn1_w_gate, ffn1_w_up, ffn1_w_down, mix_norm, w_in, lambda_q1, lambda_k1,
           lambda_q2, lambda_k2, subln_gain, w_out, ffn2_norm, ffn2_w_gate, ffn2_w_up, ffn2_w_down,
           rel_bias, final_norm):
    b, s, d = x.shape
    depth = w_in.shape[0]
    m = b * s
    diff_bias = _diff_bias_tiles(rel_bias)
    dil_bias = _dil_bias_tiles(rel_bias)
    qscale = jnp.concatenate([
        jnp.full((DIFF_WIDTH,), HEAD_DIM ** -0.5 * LOG2E, F32), jnp.ones((2 * DIFF_WIDTH,), F32),
        jnp.full((DIL_WIDTH,), HEAD_DIM ** -0.5 * LOG2E, F32), jnp.ones((2 * DIL_WIDTH,), F32)])
    h = x.reshape(m, d)
    for layer in range(depth):
        row = lambda v: v[layer].reshape(1, -1)
        h = _ffn(h, row(ffn1_norm), ffn1_w_gate[layer].astype(BF16), ffn1_w_up[layer].astype(BF16),
                 ffn1_w_down[layer].astype(BF16), row(ffn1_norm), final=False)
        qkv = _inproj(h, row(mix_norm), (w_in[layer] * qscale).astype(BF16)).reshape(b, s, -1)
        oa = _diff_attn(qkv, diff_bias, row(lambda_q1), row(lambda_k1), row(lambda_q2), row(lambda_k2),
                        subln_gain[layer].reshape(-1, 1), lam0=_lambda_init(layer))
        ob = _dilated_attn(qkv, dil_bias)
        last = layer == depth - 1
        h = _ffn(h, row(ffn2_norm), ffn2_w_gate[layer].astype(BF16), ffn2_w_up[layer].astype(BF16),
                 ffn2_w_down[layer].astype(BF16), final_norm.reshape(1, -1), final=last,
                 mix=(oa.reshape(m, DIFF_WIDTH), ob.reshape(m, DIL_WIDTH), w_out[layer].astype(BF16)))
    return h.reshape(b, s, d)
```

```python
import functools
import math

import numpy as np
import jax
import jax.numpy as jnp
from jax import lax
from jax.experimental import pallas as pl
from jax.experimental.pallas import tpu as pltpu

HEAD_DIM = 64
N_DIFF_HEADS = 4
N_DIL_HEADS = 8
DIFF_WIDTH = 512
DIL_WIDTH = 512
DILATED_CONFIGS = ((128, 1), (512, 4), (2048, 16))
NUM_BUCKETS = 32
MAX_DISTANCE = 2048
EPS = 1e-6
NEG = -1e30

LANES = 128
DIL_W = 128
DIL_UNROLL = 8
ATT_TQ = 512
ATT_TK = 512
ATT_R = ATT_TQ // ATT_TK
ATT_PAD = 16
ATT_CW = 256
VMEM_LIMIT = 56 * 1024 * 1024
LOG2E = math.log2(math.e)

F32 = jnp.float32
BF16 = jnp.bfloat16


def _bucket_lower_bounds():
    d = np.arange(0, 2 * MAX_DISTANCE + 1)
    max_exact = NUM_BUCKETS // 2
    v = np.log(np.maximum(d, 1) / max_exact) / math.log(MAX_DISTANCE / max_exact) * (NUM_BUCKETS - max_exact)
    b = np.where(d < max_exact, d, np.minimum(max_exact + v.astype(np.int64), NUM_BUCKETS - 1))
    return [int(np.argmax(b == k)) for k in range(NUM_BUCKETS)]


BUCKET_LO = _bucket_lower_bounds()
N_BIAS_TILES = ATT_R + -(-(BUCKET_LO[-1] + ATT_TK - 1) // ATT_TK)
BIAS_ROLL_W = pl.next_power_of_2(ATT_TQ + ATT_TK)


def _lambda_init(layer):
    return 0.8 - 0.6 * math.exp(-0.3 * layer)


def _bias_of_distance(rb_ref, col, dist):
    val = jnp.full(dist.shape, rb_ref[0, col], F32)
    for b in range(1, NUM_BUCKETS):
        val = jnp.where(dist >= BUCKET_LO[b], rb_ref[b, col], val)
    return val


def _diff_bias_kernel(rb_ref, out_ref):
    h = pl.program_id(0)
    delta = (pl.program_id(1) - (ATT_R - 1)) * ATT_TK
    tq, tk = ATT_TQ, ATT_TK
    u = lax.broadcasted_iota(jnp.int32, (8, BIAS_ROLL_W), 1)
    d = u + (delta - tk)
    for m in range(2):
        val = _bias_of_distance(rb_ref, 2 * h + m, d) * LOG2E
        val = jnp.where(d < 0, NEG, val)
        x = jnp.broadcast_to(val[0:1, :], (tk, BIAS_ROLL_W))
        rolled = pltpu.roll(x, 0, 1, stride=1, stride_axis=0)
        out_ref[0, 0, :, m * tq:(m + 1) * tq] = rolled[:, tk:tk + tq]


def _diff_bias_tiles(rel_bias):
    tq, tk = ATT_TQ, ATT_TK
    return pl.pallas_call(
        _diff_bias_kernel,
        grid=(N_DIFF_HEADS, N_BIAS_TILES),
        in_specs=[pl.BlockSpec(memory_space=pltpu.SMEM)],
        out_specs=pl.BlockSpec((1, 1, tk, 2 * tq), lambda h, d: (h, d, 0, 0)),
        out_shape=jax.ShapeDtypeStruct((N_DIFF_HEADS, N_BIAS_TILES, tk, 2 * tq), F32),
        name="diff_bias_tiles",
    )(rel_bias)


def _dil_bias_kernel(rb_ref, out_ref):
    br = pl.program_id(0)
    h = pl.program_id(1)
    w = DIL_W
    dil = jnp.where(br == 0, DILATED_CONFIGS[0][1],
                    jnp.where(br == 1, DILATED_CONFIGS[1][1], DILATED_CONFIGS[2][1]))
    u = lax.broadcasted_iota(jnp.int32, (8, 4 * w), 1)
    dist = (2 - pl.program_id(2)) * w - u
    val = _bias_of_distance(rb_ref, 2 * N_DIFF_HEADS + h, dist * dil) * LOG2E
    val = jnp.where((dist >= 0) & (dist <= w), val, NEG)
    x = jnp.broadcast_to(val[0:1, :], (w, 4 * w))
    rolled = pltpu.roll(x, 0, 1, stride=1, stride_axis=0)
    out_ref[0, 0, 0] = rolled[:, w:3 * w]


def _dil_bias_tiles(rel_bias):
    w = DIL_W
    nb = len(DILATED_CONFIGS)
    return pl.pallas_call(
        _dil_bias_kernel,
        grid=(nb, N_DIL_HEADS, 2),
        in_specs=[pl.BlockSpec(memory_space=pltpu.SMEM)],
        out_specs=pl.BlockSpec((1, 1, 1, w, 2 * w), lambda b, h, k: (b, h, k, 0, 0)),
        out_shape=jax.ShapeDtypeStruct((nb, N_DIL_HEADS, 2, w, 2 * w), F32),
        name="dil_bias_tiles",
    )(rel_bias)


def _rms(x, g):
    return x * lax.rsqrt(jnp.mean(x * x, axis=-1, keepdims=True) + EPS) * g


def _ffn_kernel(*refs, chunk, final, mixed):
    if mixed:
        x_ref, oa_ref, ob_ref, wo_ref, g_ref, wg_ref, wu_ref, wd_ref, fg_ref, o_ref, h_ref = refs
        x = (x_ref[...] + jnp.dot(oa_ref[...], wo_ref[0:DIFF_WIDTH, :], preferred_element_type=F32)
             + jnp.dot(ob_ref[...], wo_ref[DIFF_WIDTH:, :], preferred_element_type=F32))
    else:
        x_ref, g_ref, wg_ref, wu_ref, wd_ref, fg_ref, o_ref, h_ref = refs
        x = x_ref[...]
    xn = _rms(x, g_ref[...]).astype(BF16)
    for c in range(wg_ref.shape[1] // chunk):
        sl = slice(c * chunk, (c + 1) * chunk)
        gate = jnp.dot(xn, wg_ref[:, sl], preferred_element_type=F32)
        up = jnp.dot(xn, wu_ref[:, sl], preferred_element_type=F32)
        h_ref[:, sl] = (gate * jax.nn.sigmoid(gate) * up).astype(BF16)
    y = x + 0.5 * jnp.dot(h_ref[...], wd_ref[...], preferred_element_type=F32)
    if final:
        y = _rms(y, fg_ref[...])
    o_ref[...] = y


def _ffn(x, gain, wg, wu, wd, final_gain, *, final, mix=None, tm=1024, chunk=256):
    m, d = x.shape
    dff = wg.shape[1]
    const = lambda *shape: pl.BlockSpec(shape, lambda i: (0,) * len(shape), pipeline_mode=pl.Buffered(1))
    row = lambda n: pl.BlockSpec((tm, n), lambda i: (i, 0))
    mix_specs = [] if mix is None else [row(mix[0].shape[1]), row(mix[1].shape[1]), const(*mix[2].shape)]
    return pl.pallas_call(
        functools.partial(_ffn_kernel, chunk=chunk, final=final, mixed=mix is not None),
        grid=(m // tm,),
        in_specs=[row(d)] + mix_specs + [const(1, d), const(d, dff), const(d, dff), const(dff, d), const(1, d)],
        out_specs=row(d),
        out_shape=jax.ShapeDtypeStruct((m, d), F32),
        scratch_shapes=[pltpu.VMEM((tm, dff), BF16)],
        compiler_params=pltpu.CompilerParams(
            dimension_semantics=("arbitrary",), vmem_limit_bytes=VMEM_LIMIT),
        name="ffn_mix" if mix is not None else "ffn",
    )(x, *(mix or ()), gain, wg, wu, wd, final_gain)


def _inproj_kernel(x_ref, g_ref, w_ref, o_ref, *, n_chunks, chunk):
    xn = _rms(x_ref[...], g_ref[...]).astype(BF16)
    for c in range(n_chunks):
        sl = slice(c * chunk, (c + 1) * chunk)
        o_ref[:, sl] = jnp.dot(xn, w_ref[:, sl], preferred_element_type=F32).astype(BF16)


def _inproj(x, gain, w, *, tm=1024, chunk=512):
    m, d = x.shape
    n = w.shape[1]
    return pl.pallas_call(
        functools.partial(_inproj_kernel, n_chunks=n // chunk, chunk=chunk),
        grid=(m // tm,),
        in_specs=[
            pl.BlockSpec((tm, d), lambda i: (i, 0)),
            pl.BlockSpec((1, d), lambda i: (0, 0), pipeline_mode=pl.Buffered(1)),
            pl.BlockSpec((d, n), lambda i: (0, 0), pipeline_mode=pl.Buffered(1)),
        ],
        out_specs=pl.BlockSpec((tm, n), lambda i: (i, 0)),
        out_shape=jax.ShapeDtypeStruct((m, n), BF16),
        compiler_params=pltpu.CompilerParams(
            dimension_semantics=("arbitrary",), vmem_limit_bytes=VMEM_LIMIT),
        name="inproj",
    )(x, gain, w)


def _diff_attn_kernel(q_ref, k_ref, v_ref, bias_ref, lq1_ref, lk1_ref, lq2_ref, lk2_ref, sg_ref,
                      o_ref, vt_ref, qs_ref, acc_ref, m_ref, al_ref, mxa_ref, mxb_ref,
                      sa_ref, sb_ref, ea_ref, eb_ref, *, lam0, n_q):
    tq, tk = ATT_TQ, ATT_TK
    hw = 2 * HEAD_DIM
    n_tiles = n_q * (n_q + 1) // 2

    ones_row = (lax.broadcasted_iota(jnp.int32, (ATT_PAD, tk), 0) == 0).astype(BF16)
    row = lax.broadcasted_iota(jnp.int32, (hw, tq), 0)
    for c in range(n_q):
        blk = slice(c * tk, (c + 1) * tk)
        vt_ref[c, 0:hw, :] = v_ref[0, blk, :].astype(F32).T.astype(BF16)
        vt_ref[c, hw:, :] = ones_row
        qt = q_ref[0, blk, :].astype(F32).T
        qs_ref[c] = jnp.concatenate(
            [jnp.where(row < HEAD_DIM, qt, 0.0), jnp.where(row >= HEAD_DIM, qt, 0.0)], axis=1).astype(BF16)

    cw = ATT_CW
    chunks = [slice(c * cw, (c + 1) * cw) for c in range(2 * tq // cw)]

    def step_tile(tile):
        qb, j = tile
        wrap = j == qb
        end = wrap & (qb == n_q - 1)
        return (jnp.where(wrap & jnp.logical_not(end), qb + 1, qb),
                jnp.where(end, j, jnp.where(wrap, 0, j + 1)))

    def score_chunk(tile, cols, s_ref, mx_ref):
        qb, j = tile
        kblk = k_ref[0, pl.ds(pl.multiple_of(j * tk, tk), tk), :]
        bi = jnp.minimum(qb - j, N_BIAS_TILES - 1)
        t = jnp.dot(kblk, qs_ref[qb, :, cols], preferred_element_type=F32) + bias_ref[0, bi, :, cols]
        s_ref[:, cols] = t
        mx_ref[:, cols] = jnp.max(t, axis=0, keepdims=True)

    def pv_chunk(tile, cols, e_ref):
        qb, j = tile
        acc_ref[qb, :, cols] = (al_ref[:, cols] * acc_ref[qb, :, cols]
                                + jnp.dot(vt_ref[j], e_ref[:, cols], preferred_element_type=F32))

    def softmax_chunk(tile, cols, s_ref, mx_ref, e_ref):
        m_old = jnp.where(tile[1] == 0, NEG, m_ref[:, cols])
        m_new = jnp.maximum(m_old, mx_ref[:, cols])
        al_ref[:, cols] = jnp.exp2(m_old - m_new)
        m_ref[:, cols] = m_new
        e_ref[:, cols] = jnp.exp2(s_ref[:, cols] - m_new).astype(BF16)

    def phase(prv, cur, nxt, cur_buf, nxt_buf):
        for cols in chunks:
            score_chunk(nxt, cols, nxt_buf[0], nxt_buf[1])
            pv_chunk(prv, cols, nxt_buf[2])
            softmax_chunk(cur, cols, *cur_buf)

    buf_a = (sa_ref, mxa_ref, ea_ref)
    buf_b = (sb_ref, mxb_ref, eb_ref)
    first = (jnp.int32(0), jnp.int32(0))
    m_ref[...] = jnp.full(m_ref.shape, NEG, F32)
    al_ref[...] = jnp.ones(al_ref.shape, F32)
    acc_ref[...] = jnp.zeros(acc_ref.shape, F32)
    eb_ref[...] = jnp.zeros(eb_ref.shape, BF16)
    for cols in chunks:
        score_chunk(first, cols, sa_ref, mxa_ref)

    def body(i, carry):
        prv, t0 = carry
        t1 = step_tile(t0)
        t2 = step_tile(t1)
        phase(prv, t0, t1, buf_a, buf_b)
        phase(t0, t1, t2, buf_b, buf_a)
        return t1, t2

    last, _ = lax.fori_loop(0, n_tiles // 2, body, (first, first))
    for cols in chunks:
        pv_chunk(last, cols, eb_ref)

    lam = (jnp.exp(jnp.sum(lq1_ref[...] * lk1_ref[...])) - jnp.exp(jnp.sum(lq2_ref[...] * lk2_ref[...])) + lam0)
    for c in range(n_q):
        o = acc_ref[c, 0:hw, :] / acc_ref[c, hw:hw + 1, :]
        o = o[:, :tq] - lam * o[:, tq:]
        y = o * lax.rsqrt(jnp.mean(o * o, axis=0, keepdims=True) + EPS) * sg_ref[...]
        y = y * (1.0 - lam0)
        o_ref[0, c * tq:(c + 1) * tq, :] = y.T.astype(BF16)


def _diff_attn(qkv, bias_tiles, lq1, lk1, lq2, lk2, sub_gain, *, lam0):
    b, s, _ = qkv.shape
    tq, tk = ATT_TQ, ATT_TK
    hw = 2 * HEAD_DIM
    n_q = s // tq
    assert ATT_R == 1 and (n_q * (n_q + 1) // 2) % 2 == 0
    small = lambda shape: pl.BlockSpec(shape, lambda h, bb: (0,) * len(shape))
    return pl.pallas_call(
        functools.partial(_diff_attn_kernel, lam0=lam0, n_q=n_q),
        grid=(N_DIFF_HEADS, b),
        in_specs=[
            pl.BlockSpec((1, s, hw), lambda h, bb: (bb, 0, h)),
            pl.BlockSpec((1, s, hw), lambda h, bb: (bb, 0, N_DIFF_HEADS + h)),
            pl.BlockSpec((1, s, hw), lambda h, bb: (bb, 0, 2 * N_DIFF_HEADS + h)),
            pl.BlockSpec((1, N_BIAS_TILES, tk, 2 * tq), lambda h, bb: (h, 0, 0, 0),
                         pipeline_mode=pl.Buffered(1)),
            small((1, HEAD_DIM)), small((1, HEAD_DIM)), small((1, HEAD_DIM)), small((1, HEAD_DIM)),
            small((hw, 1)),
        ],
        out_specs=pl.BlockSpec((1, s, hw), lambda h, bb: (bb, 0, h)),
        out_shape=jax.ShapeDtypeStruct((b, s, DIFF_WIDTH), BF16),
        scratch_shapes=[
            pltpu.VMEM((n_q, hw + ATT_PAD, tk), BF16),
            pltpu.VMEM((n_q, hw, 2 * tq), BF16),
            pltpu.VMEM((n_q, hw + ATT_PAD, 2 * tq), F32),
            pltpu.VMEM((1, 2 * tq), F32),
            pltpu.VMEM((1, 2 * tq), F32),
            pltpu.VMEM((1, 2 * tq), F32),
            pltpu.VMEM((1, 2 * tq), F32),
            pltpu.VMEM((tk, 2 * tq), F32),
            pltpu.VMEM((tk, 2 * tq), F32),
            pltpu.VMEM((tk, 2 * tq), BF16),
            pltpu.VMEM((tk, 2 * tq), BF16),
        ],
        compiler_params=pltpu.CompilerParams(
            dimension_semantics=("arbitrary", "arbitrary"), vmem_limit_bytes=VMEM_LIMIT),
        name="diff_attn",
    )(qkv, qkv, qkv, bias_tiles, lq1, lk1, lq2, lk2, sub_gain)


def _dilated_kernel(q_ref, k_ref, v_ref, bias_ref, o_ref, xf_ref, x4_ref, nat_ref):
    w = DIL_W
    s_len = q_ref.shape[1]
    l4 = s_len // 4
    srcs = (q_ref, k_ref, v_ref)
    cp = 512
    for t in range(3):
        for c in range(s_len // cp):
            xf_ref[t, c * cp:(c + 1) * cp, :] = srcs[t][0, c * cp:(c + 1) * cp, :].astype(F32)
    for t in range(3):
        for r in range(4):
            for c in range(l4 // cp):
                x4_ref[t, r * l4 + c * cp:r * l4 + (c + 1) * cp, :] = (
                    xf_ref[t, pl.ds(r + 4 * c * cp, cp, stride=4), :])

    low = lax.broadcasted_iota(jnp.int32, (w, LANES), 1) < HEAD_DIM
    nt = (((1,), (1,)), ((), ()))

    def rows(t, br, r, j0, n):
        if br == 0:
            return srcs[t][0, pl.ds(pl.multiple_of(j0, w), n), :]
        if br == 1:
            return x4_ref[t, pl.ds(pl.multiple_of(r * l4 + j0, w), n), :].astype(BF16)
        return x4_ref[t, pl.ds((r % 4) * l4 + r // 4 + 4 * j0, n, stride=4), :].astype(BF16)

    def attend(br, r, n):
        k0 = jnp.maximum(n - 1, 0) * w
        kind = jnp.where(n == 0, 1, 0)
        qp = rows(0, br, r, n * w, w)
        kk = rows(1, br, r, k0, 2 * w)
        vv = jnp.concatenate([rows(2, br, r, k0, 2 * w), jnp.ones((2 * w, LANES), BF16)], axis=1)
        pvs, ms = [], []
        for hh in range(2):
            qm = jnp.where(low if hh == 0 else jnp.logical_not(low), qp, jnp.zeros_like(qp))
            s = lax.dot_general(qm, kk, nt, preferred_element_type=F32) + bias_ref[br, hh, kind]
            m = jnp.max(s, axis=1, keepdims=True)
            e = jnp.exp2((s - m).astype(BF16))
            pvs.append(jnp.dot(e, vv, preferred_element_type=F32))
            ms.append(m)
        dil = DILATED_CONFIGS[br][1]
        start = r + dil * w * n
        dst = pl.ds(pl.multiple_of(start, w), w) if dil == 1 else pl.ds(start, w, stride=dil)
        nat_ref[br, 0, dst, :] = jnp.where(low, pvs[0][:, :LANES], pvs[1][:, :LANES])
        nat_ref[br, 1, dst, :] = jnp.where(low, ms[0], ms[1])
        nat_ref[br, 2, dst, :] = jnp.where(low, pvs[0][:, LANES:], pvs[1][:, LANES:])

    for br, (_, dil) in enumerate(DILATED_CONFIGS):
        nblk = s_len // dil // w

        def blocks(i, carry, br=br, nblk=nblk):
            for u in range(DIL_UNROLL):
                idx = i * DIL_UNROLL + u
                attend(br, idx // nblk, idx % nblk)
            return carry

        lax.fori_loop(0, dil * nblk // DIL_UNROLL, blocks, 0)

    for c in range(s_len // cp):
        sl = slice(c * cp, (c + 1) * cp)
        ms = [nat_ref[br, 1, sl, :] for br in range(3)]
        top = jnp.maximum(jnp.maximum(ms[0], ms[1]), ms[2])
        wts = [jnp.exp2(v - top) for v in ms]
        num = wts[0] * nat_ref[0, 0, sl, :] + wts[1] * nat_ref[1, 0, sl, :] + wts[2] * nat_ref[2, 0, sl, :]
        den = wts[0] * nat_ref[0, 2, sl, :] + wts[1] * nat_ref[1, 2, sl, :] + wts[2] * nat_ref[2, 2, sl, :]
        o_ref[0, sl, :] = (num / den).astype(BF16)


def _dilated_attn(qkv, bias):
    b, s, _ = qkv.shape
    w = DIL_W
    col0 = 3 * DIFF_WIDTH // LANES
    pairs = DIL_WIDTH // LANES
    part = lambda t: pl.BlockSpec((1, s, LANES), lambda bb, p: (bb, 0, col0 + t * pairs + p))
    return pl.pallas_call(
        _dilated_kernel,
        grid=(b, pairs),
        in_specs=[part(0), part(1), part(2),
                  pl.BlockSpec((len(DILATED_CONFIGS), 2, 2, w, 2 * w), lambda bb, p: (0, p, 0, 0, 0))],
        out_specs=pl.BlockSpec((1, s, LANES), lambda bb, p: (bb, 0, p)),
        out_shape=jax.ShapeDtypeStruct((b, s, DIL_WIDTH), BF16),
        scratch_shapes=[
            pltpu.VMEM((3, s, LANES), F32),
            pltpu.VMEM((3, s, LANES), F32),
            pltpu.VMEM((len(DILATED_CONFIGS), 3, s, LANES), F32),
        ],
        compiler_params=pltpu.CompilerParams(
            dimension_semantics=("arbitrary", "arbitrary"), vmem_limit_bytes=VMEM_LIMIT),
        name="dilated_attn",
    )(qkv, qkv, qkv, bias)


def kernel(x, ffn1_norm, ffn1_w_gate, ffn1_w_up, ffn1_w_down, mix_norm, w_in, lambda_q1, lambda_k1,
           lambda_q2, lambda_k2, subln_gain, w_out, ffn2_norm, ffn2_w_gate, ffn2_w_up, ffn2_w_down,
           rel_bias, final_norm):
    b, s, d = x.shape
    depth = w_in.shape[0]
    m = b * s
    diff_bias = _diff_bias_tiles(rel_bias)
    dil_bias = _dil_bias_tiles(rel_bias)
    qscale = jnp.concatenate([
        jnp.full((DIFF_WIDTH,), HEAD_DIM ** -0.5 * LOG2E, F32), jnp.ones((2 * DIFF_WIDTH,), F32),
        jnp.full((DIL_WIDTH,), HEAD_DIM ** -0.5 * LOG2E, F32), jnp.ones((2 * DIL_WIDTH,), F32)])
    h = x.reshape(m, d)
    for layer in range(depth):
        row = lambda v: v[layer].reshape(1, -1)
        h = _ffn(h, row(ffn1_norm), ffn1_w_gate[layer].astype(BF16), ffn1_w_up[layer].astype(BF16),
                 ffn1_w_down[layer].astype(BF16), row(ffn1_norm), final=False)
        qkv = _inproj(h, row(mix_norm), (w_in[layer] * qscale).astype(BF16)).reshape(b, s, -1)
        oa = _diff_attn(qkv, diff_bias, row(lambda_q1), row(lambda_k1), row(lambda_q2), row(lambda_k2),
                        subln_gain[layer].reshape(-1, 1), lam0=_lambda_init(layer))
        ob = _dilated_attn(qkv, dil_bias)
        last = layer == depth - 1
        h = _ffn(h, row(ffn2_norm), ffn2_w_gate[layer].astype(BF16), ffn2_w_up[layer].astype(BF16),
                 ffn2_w_down[layer].astype(BF16), final_norm.reshape(1, -1), final=last,
                 mix=(oa.reshape(m, DIFF_WIDTH), ob.reshape(m, DIL_WIDTH), w_out[layer].astype(BF16)))
    return h.reshape(b, s, d)
```

```python
import functools
import math

import numpy as np
import jax
import jax.numpy as jnp
from jax import lax
from jax.experimental import pallas as pl
from jax.experimental.pallas import tpu as pltpu

HEAD_DIM = 64
N_DIFF_HEADS = 4
N_DIL_HEADS = 8
DIFF_WIDTH = 512
DIL_WIDTH = 512
DILATED_CONFIGS = ((128, 1), (512, 4), (2048, 16))
NUM_BUCKETS = 32
MAX_DISTANCE = 2048
EPS = 1e-6
NEG = -1e30

LANES = 128
DIL_W = 128
DIL_UNROLL = 16
ATT_TQ = 512
ATT_TK = 512
ATT_R = ATT_TQ // ATT_TK
ATT_PAD = 16
ATT_CW = 256
VMEM_LIMIT = 56 * 1024 * 1024
LOG2E = math.log2(math.e)

F32 = jnp.float32
BF16 = jnp.bfloat16


def _bucket_lower_bounds():
    d = np.arange(0, 2 * MAX_DISTANCE + 1)
    max_exact = NUM_BUCKETS // 2
    v = np.log(np.maximum(d, 1) / max_exact) / math.log(MAX_DISTANCE / max_exact) * (NUM_BUCKETS - max_exact)
    b = np.where(d < max_exact, d, np.minimum(max_exact + v.astype(np.int64), NUM_BUCKETS - 1))
    return [int(np.argmax(b == k)) for k in range(NUM_BUCKETS)]


BUCKET_LO = _bucket_lower_bounds()
N_BIAS_TILES = ATT_R + -(-(BUCKET_LO[-1] + ATT_TK - 1) // ATT_TK)
BIAS_ROLL_W = pl.next_power_of_2(ATT_TQ + ATT_TK)


def _lambda_init(layer):
    return 0.8 - 0.6 * math.exp(-0.3 * layer)


def _bias_of_distance(rb_ref, col, dist):
    val = jnp.full(dist.shape, rb_ref[0, col], F32)
    for b in range(1, NUM_BUCKETS):
        val = jnp.where(dist >= BUCKET_LO[b], rb_ref[b, col], val)
    return val


def _diff_bias_kernel(rb_ref, out_ref):
    h = pl.program_id(0)
    delta = (pl.program_id(1) - (ATT_R - 1)) * ATT_TK
    tq, tk = ATT_TQ, ATT_TK
    u = lax.broadcasted_iota(jnp.int32, (8, BIAS_ROLL_W), 1)
    d = u + (delta - tk)
    for m in range(2):
        val = _bias_of_distance(rb_ref, 2 * h + m, d) * LOG2E
        val = jnp.where(d < 0, NEG, val)
        x = jnp.broadcast_to(val[0:1, :], (tk, BIAS_ROLL_W))
        rolled = pltpu.roll(x, 0, 1, stride=1, stride_axis=0)
        out_ref[0, 0, :, m * tq:(m + 1) * tq] = rolled[:, tk:tk + tq]


def _diff_bias_tiles(rel_bias):
    tq, tk = ATT_TQ, ATT_TK
    return pl.pallas_call(
        _diff_bias_kernel,
        grid=(N_DIFF_HEADS, N_BIAS_TILES),
        in_specs=[pl.BlockSpec(memory_space=pltpu.SMEM)],
        out_specs=pl.BlockSpec((1, 1, tk, 2 * tq), lambda h, d: (h, d, 0, 0)),
        out_shape=jax.ShapeDtypeStruct((N_DIFF_HEADS, N_BIAS_TILES, tk, 2 * tq), F32),
        name="diff_bias_tiles",
    )(rel_bias)


def _dil_bias_kernel(rb_ref, out_ref):
    br = pl.program_id(0)
    w = DIL_W
    dil = jnp.where(br == 0, DILATED_CONFIGS[0][1],
                    jnp.where(br == 1, DILATED_CONFIGS[1][1], DILATED_CONFIGS[2][1]))
    u = lax.broadcasted_iota(jnp.int32, (8, 4 * w), 1)
    for h in range(N_DIL_HEADS):
        for kind in range(2):
            dist = (2 - kind) * w - u
            val = _bias_of_distance(rb_ref, 2 * N_DIFF_HEADS + h, dist * dil) * LOG2E
            val = jnp.where((dist >= 0) & (dist <= w), val, NEG)
            x = jnp.broadcast_to(val[0:1, :], (w, 4 * w))
            rolled = pltpu.roll(x, 0, 1, stride=1, stride_axis=0)
            out_ref[0, h, kind] = rolled[:, w:3 * w]


def _dil_bias_tiles(rel_bias):
    w = DIL_W
    nb = len(DILATED_CONFIGS)
    return pl.pallas_call(
        _dil_bias_kernel,
        grid=(nb,),
        in_specs=[pl.BlockSpec(memory_space=pltpu.SMEM)],
        out_specs=pl.BlockSpec((1, N_DIL_HEADS, 2, w, 2 * w), lambda b: (b, 0, 0, 0, 0)),
        out_shape=jax.ShapeDtypeStruct((nb, N_DIL_HEADS, 2, w, 2 * w), F32),
        name="dil_bias_tiles",
    )(rel_bias)


def _rms(x, g):
    return x * lax.rsqrt(jnp.mean(x * x, axis=-1, keepdims=True) + EPS) * g


def _ffn_kernel(*refs, chunk, final, mixed):
    if mixed:
        x_ref, oa_ref, ob_ref, wo_ref, g_ref, wg_ref, wu_ref, wd_ref, fg_ref, o_ref, h_ref = refs
        x = (x_ref[...] + jnp.dot(oa_ref[...], wo_ref[0:DIFF_WIDTH, :], preferred_element_type=F32)
             + jnp.dot(ob_ref[...], wo_ref[DIFF_WIDTH:, :], preferred_element_type=F32))
    else:
        x_ref, g_ref, wg_ref, wu_ref, wd_ref, fg_ref, o_ref, h_ref = refs
        x = x_ref[...]
    xn = _rms(x, g_ref[...]).astype(BF16)
    for c in range(wg_ref.shape[1] // chunk):
        sl = slice(c * chunk, (c + 1) * chunk)
        gate = jnp.dot(xn, wg_ref[:, sl], preferred_element_type=F32)
        up = jnp.dot(xn, wu_ref[:, sl], preferred_element_type=F32)
        h_ref[:, sl] = (gate * jax.nn.sigmoid(gate) * up).astype(BF16)
    y = x + 0.5 * jnp.dot(h_ref[...], wd_ref[...], preferred_element_type=F32)
    if final:
        y = _rms(y, fg_ref[...])
    o_ref[...] = y


def _ffn(x, gain, wg, wu, wd, final_gain, *, final, mix=None, tm=1024, chunk=256):
    m, d = x.shape
    dff = wg.shape[1]
    const = lambda *shape: pl.BlockSpec(shape, lambda i: (0,) * len(shape), pipeline_mode=pl.Buffered(1))
    row = lambda n: pl.BlockSpec((tm, n), lambda i: (i, 0))
    mix_specs = [] if mix is None else [row(mix[0].shape[1]), row(mix[1].shape[1]), const(*mix[2].shape)]
    return pl.pallas_call(
        functools.partial(_ffn_kernel, chunk=chunk, final=final, mixed=mix is not None),
        grid=(m // tm,),
        in_specs=[row(d)] + mix_specs + [const(1, d), const(d, dff), const(d, dff), const(dff, d), const(1, d)],
        out_specs=row(d),
        out_shape=jax.ShapeDtypeStruct((m, d), F32),
        scratch_shapes=[pltpu.VMEM((tm, dff), BF16)],
        compiler_params=pltpu.CompilerParams(
            dimension_semantics=("arbitrary",), vmem_limit_bytes=VMEM_LIMIT),
        name="ffn_mix" if mix is not None else "ffn",
    )(x, *(mix or ()), gain, wg, wu, wd, final_gain)


def _inproj_kernel(x_ref, g_ref, w_ref, o_ref, *, n_chunks, chunk):
    xn = _rms(x_ref[...], g_ref[...]).astype(BF16)
    for c in range(n_chunks):
        sl = slice(c * chunk, (c + 1) * chunk)
        o_ref[:, sl] = jnp.dot(xn, w_ref[:, sl], preferred_element_type=F32).astype(BF16)


def _inproj(x, gain, w, *, tm=1024, chunk=512):
    m, d = x.shape
    n = w.shape[1]
    return pl.pallas_call(
        functools.partial(_inproj_kernel, n_chunks=n // chunk, chunk=chunk),
        grid=(m // tm,),
        in_specs=[
            pl.BlockSpec((tm, d), lambda i: (i, 0)),
            pl.BlockSpec((1, d), lambda i: (0, 0), pipeline_mode=pl.Buffered(1)),
            pl.BlockSpec((d, n), lambda i: (0, 0), pipeline_mode=pl.Buffered(1)),
        ],
        out_specs=pl.BlockSpec((tm, n), lambda i: (i, 0)),
        out_shape=jax.ShapeDtypeStruct((m, n), BF16),
        compiler_params=pltpu.CompilerParams(
            dimension_semantics=("arbitrary",), vmem_limit_bytes=VMEM_LIMIT),
        name="inproj",
    )(x, gain, w)


def _diff_attn_kernel(q_ref, k_ref, v_ref, bias_ref, lq1_ref, lk1_ref, lq2_ref, lk2_ref, sg_ref,
                      o_ref, vt_ref, qs_ref, acc_ref, m_ref, al_ref, mxa_ref, mxb_ref,
                      sa_ref, sb_ref, ea_ref, eb_ref, *, lam0, n_q):
    tq, tk = ATT_TQ, ATT_TK
    hw = 2 * HEAD_DIM
    n_tiles = n_q * (n_q + 1) // 2

    ones_row = (lax.broadcasted_iota(jnp.int32, (ATT_PAD, tk), 0) == 0).astype(BF16)
    row = lax.broadcasted_iota(jnp.int32, (hw, tq), 0)
    for c in range(n_q):
        blk = slice(c * tk, (c + 1) * tk)
        vt_ref[c, 0:hw, :] = v_ref[0, blk, :].astype(F32).T.astype(BF16)
        vt_ref[c, hw:, :] = ones_row
        qt = q_ref[0, blk, :].astype(F32).T
        qs_ref[c] = jnp.concatenate(
            [jnp.where(row < HEAD_DIM, qt, 0.0), jnp.where(row >= HEAD_DIM, qt, 0.0)], axis=1).astype(BF16)
        acc_ref[c] = jnp.zeros(acc_ref.shape[1:], F32)

    cw = ATT_CW
    chunks = [slice(c * cw, (c + 1) * cw) for c in range(2 * tq // cw)]

    def step_tile(tile):
        qb, j = tile
        wrap = j == qb
        end = wrap & (qb == n_q - 1)
        return (jnp.where(wrap & jnp.logical_not(end), qb + 1, qb),
                jnp.where(end, j, jnp.where(wrap, 0, j + 1)))

    def score_chunk(tile, cols, s_ref, mx_ref):
        qb, j = tile
        kblk = k_ref[0, pl.ds(pl.multiple_of(j * tk, tk), tk), :]
        bi = jnp.minimum(qb - j, N_BIAS_TILES - 1)
        t = jnp.dot(kblk, qs_ref[qb, :, cols], preferred_element_type=F32) + bias_ref[0, bi, :, cols]
        s_ref[:, cols] = t
        mx_ref[:, cols] = jnp.max(t, axis=0, keepdims=True)

    def pv_chunk(tile, cols, e_ref):
        qb, j = tile
        acc_ref[qb, :, cols] = (al_ref[:, cols] * acc_ref[qb, :, cols]
                                + jnp.dot(vt_ref[j], e_ref[:, cols], preferred_element_type=F32))

    def softmax_chunk(tile, cols, s_ref, mx_ref, e_ref):
        m_old = jnp.where(tile[1] == 0, NEG, m_ref[:, cols])
        m_new = jnp.maximum(m_old, mx_ref[:, cols])
        al_ref[:, cols] = jnp.exp2(m_old - m_new)
        m_ref[:, cols] = m_new
        e_ref[:, cols] = jnp.exp2(s_ref[:, cols] - m_new).astype(BF16)

    def phase(prv, cur, nxt, cur_buf, nxt_buf):
        for cols in chunks:
            score_chunk(nxt, cols, nxt_buf[0], nxt_buf[1])
            pv_chunk(prv, cols, nxt_buf[2])
            softmax_chunk(cur, cols, *cur_buf)

    buf_a = (sa_ref, mxa_ref, ea_ref)
    buf_b = (sb_ref, mxb_ref, eb_ref)
    first = (jnp.int32(0), jnp.int32(0))
    m_ref[...] = jnp.full(m_ref.shape, NEG, F32)
    al_ref[...] = jnp.ones(al_ref.shape, F32)
    eb_ref[...] = jnp.zeros(eb_ref.shape, BF16)
    for cols in chunks:
        score_chunk(first, cols, sa_ref, mxa_ref)

    pairs = 2 if n_tiles % 4 == 0 else 1

    def body(i, carry):
        prv, t0 = carry
        for _ in range(pairs):
            t1 = step_tile(t0)
            t2 = step_tile(t1)
            phase(prv, t0, t1, buf_a, buf_b)
            phase(t0, t1, t2, buf_b, buf_a)
            prv, t0 = t1, t2
        return prv, t0

    last, _ = lax.fori_loop(0, n_tiles // (2 * pairs), body, (first, first))
    for cols in chunks:
        pv_chunk(last, cols, eb_ref)

    lam = (jnp.exp(jnp.sum(lq1_ref[...] * lk1_ref[...])) - jnp.exp(jnp.sum(lq2_ref[...] * lk2_ref[...])) + lam0)
    for c in range(n_q):
        o = acc_ref[c, 0:hw, :] / acc_ref[c, hw:hw + 1, :]
        o = o[:, :tq] - lam * o[:, tq:]
        y = o * lax.rsqrt(jnp.mean(o * o, axis=0, keepdims=True) + EPS) * sg_ref[...]
        y = y * (1.0 - lam0)
        o_ref[0, c * tq:(c + 1) * tq, :] = y.T.astype(BF16)


def _diff_attn(qkv, bias_tiles, lq1, lk1, lq2, lk2, sub_gain, *, lam0):
    b, s, _ = qkv.shape
    tq, tk = ATT_TQ, ATT_TK
    hw = 2 * HEAD_DIM
    n_q = s // tq
    assert ATT_R == 1 and (n_q * (n_q + 1) // 2) % 2 == 0
    small = lambda shape: pl.BlockSpec(shape, lambda h, bb: (0,) * len(shape))
    return pl.pallas_call(
        functools.partial(_diff_attn_kernel, lam0=lam0, n_q=n_q),
        grid=(N_DIFF_HEADS, b),
        in_specs=[
            pl.BlockSpec((1, s, hw), lambda h, bb: (bb, 0, h)),
            pl.BlockSpec((1, s, hw), lambda h, bb: (bb, 0, N_DIFF_HEADS + h)),
            pl.BlockSpec((1, s, hw), lambda h, bb: (bb, 0, 2 * N_DIFF_HEADS + h)),
            pl.BlockSpec((1, N_BIAS_TILES, tk, 2 * tq), lambda h, bb: (h, 0, 0, 0),
                         pipeline_mode=pl.Buffered(1)),
            small((1, HEAD_DIM)), small((1, HEAD_DIM)), small((1, HEAD_DIM)), small((1, HEAD_DIM)),
            small((hw, 1)),
        ],
        out_specs=pl.BlockSpec((1, s, hw), lambda h, bb: (bb, 0, h)),
        out_shape=jax.ShapeDtypeStruct((b, s, DIFF_WIDTH), BF16),
        scratch_shapes=[
            pltpu.VMEM((n_q, hw + ATT_PAD, tk), BF16),
            pltpu.VMEM((n_q, hw, 2 * tq), BF16),
            pltpu.VMEM((n_q, hw + ATT_PAD, 2 * tq), F32),
            pltpu.VMEM((1, 2 * tq), F32),
            pltpu.VMEM((1, 2 * tq), F32),
            pltpu.VMEM((1, 2 * tq), F32),
            pltpu.VMEM((1, 2 * tq), F32),
            pltpu.VMEM((tk, 2 * tq), F32),
            pltpu.VMEM((tk, 2 * tq), F32),
            pltpu.VMEM((tk, 2 * tq), BF16),
            pltpu.VMEM((tk, 2 * tq), BF16),
        ],
        compiler_params=pltpu.CompilerParams(
            dimension_semantics=("arbitrary", "arbitrary"), vmem_limit_bytes=VMEM_LIMIT),
        name="diff_attn",
    )(qkv, qkv, qkv, bias_tiles, lq1, lk1, lq2, lk2, sub_gain)


def _dilated_kernel(q_ref, k_ref, v_ref, bias_ref, o_ref, xf_ref, x4_ref, nat_ref):
    w = DIL_W
    s_len = q_ref.shape[1]
    l4 = s_len // 4
    srcs = (q_ref, k_ref, v_ref)
    cp = 512
    for t in range(3):
        for c in range(s_len // cp):
            xf_ref[t, c * cp:(c + 1) * cp, :] = srcs[t][0, c * cp:(c + 1) * cp, :].astype(F32)
    for t in range(3):
        for r in range(4):
            for c in range(l4 // cp):
                x4_ref[t, r * l4 + c * cp:r * l4 + (c + 1) * cp, :] = (
                    xf_ref[t, pl.ds(r + 4 * c * cp, cp, stride=4), :])

    low = lax.broadcasted_iota(jnp.int32, (w, LANES), 1) < HEAD_DIM
    nt = (((1,), (1,)), ((), ()))

    def rows(t, br, r, j0, n):
        if br == 0:
            return srcs[t][0, pl.ds(pl.multiple_of(j0, w), n), :]
        if br == 1:
            return x4_ref[t, pl.ds(pl.multiple_of(r * l4 + j0, w), n), :].astype(BF16)
        return x4_ref[t, pl.ds((r % 4) * l4 + r // 4 + 4 * j0, n, stride=4), :].astype(BF16)

    def attend(br, r, n):
        k0 = jnp.maximum(n - 1, 0) * w
        kind = jnp.where(n == 0, 1, 0)
        qp = rows(0, br, r, n * w, w)
        kk = rows(1, br, r, k0, 2 * w)
        vv = jnp.concatenate([rows(2, br, r, k0, 2 * w), jnp.ones((2 * w, LANES), BF16)], axis=1)
        pvs, ms = [], []
        for hh in range(2):
            qm = jnp.where(low if hh == 0 else jnp.logical_not(low), qp, jnp.zeros_like(qp))
            s = lax.dot_general(qm, kk, nt, preferred_element_type=F32) + bias_ref[br, hh, kind]
            m = jnp.max(s, axis=1, keepdims=True)
            e = jnp.exp2((s - m).astype(BF16))
            pvs.append(jnp.dot(e, vv, preferred_element_type=F32))
            ms.append(m)
        dil = DILATED_CONFIGS[br][1]
        start = r + dil * w * n
        dst = pl.ds(pl.multiple_of(start, w), w) if dil == 1 else pl.ds(start, w, stride=dil)
        nat_ref[br, 0, dst, :] = jnp.where(low, pvs[0][:, :LANES], pvs[1][:, :LANES])
        nat_ref[br, 1, dst, :] = jnp.where(low, ms[0], ms[1])
        nat_ref[br, 2, dst, :] = jnp.where(low, pvs[0][:, LANES:], pvs[1][:, LANES:])

    for br, (_, dil) in enumerate(DILATED_CONFIGS):
        nblk = s_len // dil // w

        def blocks(i, carry, br=br, nblk=nblk):
            for u in range(DIL_UNROLL):
                idx = i * DIL_UNROLL + u
                attend(br, idx // nblk, idx % nblk)
            return carry

        lax.fori_loop(0, dil * nblk // DIL_UNROLL, blocks, 0)

    for c in range(s_len // cp):
        sl = slice(c * cp, (c + 1) * cp)
        ms = [nat_ref[br, 1, sl, :] for br in range(3)]
        top = jnp.maximum(jnp.maximum(ms[0], ms[1]), ms[2])
        wts = [jnp.exp2(v - top) for v in ms]
        num = wts[0] * nat_ref[0, 0, sl, :] + wts[1] * nat_ref[1, 0, sl, :] + wts[2] * nat_ref[2, 0, sl, :]
        den = wts[0] * nat_ref[0, 2, sl, :] + wts[1] * nat_ref[1, 2, sl, :] + wts[2] * nat_ref[2, 2, sl, :]
        o_ref[0, sl, :] = (num / den).astype(BF16)


def _dilated_attn(qkv, bias):
    b, s, _ = qkv.shape
    w = DIL_W
    col0 = 3 * DIFF_WIDTH // LANES
    pairs = DIL_WIDTH // LANES
    part = lambda t: pl.BlockSpec((1, s, LANES), lambda bb, p: (bb, 0, col0 + t * pairs + p))
    return pl.pallas_call(
        _dilated_kernel,
        grid=(b, pairs),
        in_specs=[part(0), part(1), part(2),
                  pl.BlockSpec((len(DILATED_CONFIGS), 2, 2, w, 2 * w), lambda bb, p: (0, p, 0, 0, 0))],
        out_specs=pl.BlockSpec((1, s, LANES), lambda bb, p: (bb, 0, p)),
        out_shape=jax.ShapeDtypeStruct((b, s, DIL_WIDTH), BF16),
        scratch_shapes=[
            pltpu.VMEM((3, s, LANES), F32),
            pltpu.VMEM((3, s, LANES), F32),
            pltpu.VMEM((len(DILATED_CONFIGS), 3, s, LANES), F32),
        ],
        compiler_params=pltpu.CompilerParams(
            dimension_semantics=("arbitrary", "arbitrary"), vmem_limit_bytes=VMEM_LIMIT),
        name="dilated_attn",
    )(qkv, qkv, qkv, bias)


def kernel(x, ffn1_norm, ffn1_w_gate, ffn1_w_up, ffn1_w_down, mix_norm, w_in, lambda_q1, lambda_k1,
           lambda_q2, lambda_k2, subln_gain, w_out, ffn2_norm, ffn2_w_gate, ffn2_w_up, ffn2_w_down,
           rel_bias, final_norm):
    b, s, d = x.shape
    depth = w_in.shape[0]
    m = b * s
    diff_bias = _diff_bias_tiles(rel_bias)
    dil_bias = _dil_bias_tiles(rel_bias)
    qscale = jnp.concatenate([
        jnp.full((DIFF_WIDTH,), HEAD_DIM ** -0.5 * LOG2E, F32), jnp.ones((2 * DIFF_WIDTH,), F32),
        jnp.full((DIL_WIDTH,), HEAD_DIM ** -0.5 * LOG2E, F32), jnp.ones((2 * DIL_WIDTH,), F32)])
    h = x.reshape(m, d)
    for layer in range(depth):
        row = lambda v: v[layer].reshape(1, -1)
        h = _ffn(h, row(ffn1_norm), ffn1_w_gate[layer].astype(BF16), ffn1_w_up[layer].astype(BF16),
                 ffn1_w_down[layer].astype(BF16), row(ffn1_norm), final=False)
        qkv = _inproj(h, row(mix_norm), (w_in[layer] * qscale).astype(BF16)).reshape(b, s, -1)
        oa = _diff_attn(qkv, diff_bias, row(lambda_q1), row(lambda_k1), row(lambda_q2), row(lambda_k2),
                        subln_gain[layer].reshape(-1, 1), lam0=_lambda_init(layer))
        ob = _dilated_attn(qkv, dil_bias)
        last = layer == depth - 1
        h = _ffn(h, row(ffn2_norm), ffn2_w_gate[layer].astype(BF16), ffn2_w_up[layer].astype(BF16),
                 ffn2_w_down[layer].astype(BF16), final_norm.reshape(1, -1), final=last,
                 mix=(oa.reshape(m, DIFF_WIDTH), ob.reshape(m, DIL_WIDTH), w_out[layer].astype(BF16)))
    return h.reshape(b, s, d)
```

```python
import functools
import math

import numpy as np
import jax
import jax.numpy as jnp
from jax import lax
from jax.experimental import pallas as pl
from jax.experimental.pallas import tpu as pltpu

HEAD_DIM = 64
N_DIFF_HEADS = 4
N_DIL_HEADS = 8
DIFF_WIDTH = 512
DIL_WIDTH = 512
DILATED_CONFIGS = ((128, 1), (512, 4), (2048, 16))
NUM_BUCKETS = 32
MAX_DISTANCE = 2048
EPS = 1e-6
NEG = -1e30

LANES = 128
DIL_W = 128
DIL_UNROLL = 16
ATT_TQ = 512
ATT_TK = 512
ATT_R = ATT_TQ // ATT_TK
ATT_PAD = 16
ATT_CW = 256
VMEM_LIMIT = 56 * 1024 * 1024
LOG2E = math.log2(math.e)

F32 = jnp.float32
BF16 = jnp.bfloat16


def _bucket_lower_bounds():
    d = np.arange(0, 2 * MAX_DISTANCE + 1)
    max_exact = NUM_BUCKETS // 2
    v = np.log(np.maximum(d, 1) / max_exact) / math.log(MAX_DISTANCE / max_exact) * (NUM_BUCKETS - max_exact)
    b = np.where(d < max_exact, d, np.minimum(max_exact + v.astype(np.int64), NUM_BUCKETS - 1))
    return [int(np.argmax(b == k)) for k in range(NUM_BUCKETS)]


BUCKET_LO = _bucket_lower_bounds()
N_BIAS_TILES = ATT_R + -(-(BUCKET_LO[-1] + ATT_TK - 1) // ATT_TK)
BIAS_ROLL_W = pl.next_power_of_2(ATT_TQ + ATT_TK)


def _lambda_init(layer):
    return 0.8 - 0.6 * math.exp(-0.3 * layer)


def _bias_of_distance(rb_ref, col, dist):
    val = jnp.full(dist.shape, rb_ref[0, col], F32)
    for b in range(1, NUM_BUCKETS):
        val = jnp.where(dist >= BUCKET_LO[b], rb_ref[b, col], val)
    return val


def _diff_bias_kernel(rb_ref, out_ref):
    h = pl.program_id(0)
    delta = (pl.program_id(1) - (ATT_R - 1)) * ATT_TK
    tq, tk = ATT_TQ, ATT_TK
    u = lax.broadcasted_iota(jnp.int32, (8, BIAS_ROLL_W), 1)
    d = u + (delta - tk)
    for m in range(2):
        val = _bias_of_distance(rb_ref, 2 * h + m, d) * LOG2E
        val = jnp.where(d < 0, NEG, val)
        x = jnp.broadcast_to(val[0:1, :], (tk, BIAS_ROLL_W))
        rolled = pltpu.roll(x, 0, 1, stride=1, stride_axis=0)
        out_ref[0, 0, :, m * tq:(m + 1) * tq] = rolled[:, tk:tk + tq]


def _diff_bias_tiles(rel_bias):
    tq, tk = ATT_TQ, ATT_TK
    return pl.pallas_call(
        _diff_bias_kernel,
        grid=(N_DIFF_HEADS, N_BIAS_TILES),
        in_specs=[pl.BlockSpec(memory_space=pltpu.SMEM)],
        out_specs=pl.BlockSpec((1, 1, tk, 2 * tq), lambda h, d: (h, d, 0, 0)),
        out_shape=jax.ShapeDtypeStruct((N_DIFF_HEADS, N_BIAS_TILES, tk, 2 * tq), F32),
        name="diff_bias_tiles",
    )(rel_bias)


def _dil_bias_kernel(rb_ref, out_ref):
    br = pl.program_id(0)
    w = DIL_W
    dil = jnp.where(br == 0, DILATED_CONFIGS[0][1],
                    jnp.where(br == 1, DILATED_CONFIGS[1][1], DILATED_CONFIGS[2][1]))
    u = lax.broadcasted_iota(jnp.int32, (8, 4 * w), 1)
    for h in range(N_DIL_HEADS):
        for kind in range(2):
            dist = (2 - kind) * w - u
            val = _bias_of_distance(rb_ref, 2 * N_DIFF_HEADS + h, dist * dil) * LOG2E
            val = jnp.where((dist >= 0) & (dist <= w), val, NEG)
            x = jnp.broadcast_to(val[0:1, :], (w, 4 * w))
            rolled = pltpu.roll(x, 0, 1, stride=1, stride_axis=0)
            out_ref[0, h, kind] = rolled[:, w:3 * w]


def _dil_bias_tiles(rel_bias):
    w = DIL_W
    nb = len(DILATED_CONFIGS)
    return pl.pallas_call(
        _dil_bias_kernel,
        grid=(nb,),
        in_specs=[pl.BlockSpec(memory_space=pltpu.SMEM)],
        out_specs=pl.BlockSpec((1, N_DIL_HEADS, 2, w, 2 * w), lambda b: (b, 0, 0, 0, 0)),
        out_shape=jax.ShapeDtypeStruct((nb, N_DIL_HEADS, 2, w, 2 * w), F32),
        name="dil_bias_tiles",
    )(rel_bias)


def _rms(x, g):
    return x * lax.rsqrt(jnp.mean(x * x, axis=-1, keepdims=True) + EPS) * g


def _ffn_kernel(*refs, chunk, final, mixed):
    if mixed:
        x_ref, oa_ref, ob_ref, wo_ref, g_ref, wg_ref, wu_ref, wd_ref, fg_ref, o_ref, h_ref = refs
        x = (x_ref[...] + jnp.dot(oa_ref[...], wo_ref[0:DIFF_WIDTH, :], preferred_element_type=F32)
             + jnp.dot(ob_ref[...], wo_ref[DIFF_WIDTH:, :], preferred_element_type=F32))
    else:
        x_ref, g_ref, wg_ref, wu_ref, wd_ref, fg_ref, o_ref, h_ref = refs
        x = x_ref[...]
    xn = _rms(x, g_ref[...]).astype(BF16)
    for c in range(wg_ref.shape[1] // chunk):
        sl = slice(c * chunk, (c + 1) * chunk)
        gate = jnp.dot(xn, wg_ref[:, sl].astype(BF16), preferred_element_type=F32)
        up = jnp.dot(xn, wu_ref[:, sl].astype(BF16), preferred_element_type=F32)
        h_ref[:, sl] = (gate * jax.nn.sigmoid(gate) * up).astype(BF16)
    y = x + 0.5 * jnp.dot(h_ref[...], wd_ref[...], preferred_element_type=F32)
    if final:
        y = _rms(y, fg_ref[...])
    o_ref[...] = y


def _ffn(x, gain, wg, wu, wd, final_gain, *, final, mix=None, tm=512, chunk=256):
    m, d = x.shape
    dff = wg.shape[1]
    const = lambda *shape: pl.BlockSpec(shape, lambda i: (0,) * len(shape), pipeline_mode=pl.Buffered(1))
    row = lambda n: pl.BlockSpec((tm, n), lambda i: (i, 0))
    mix_specs = [] if mix is None else [row(mix[0].shape[1]), row(mix[1].shape[1]), const(*mix[2].shape)]
    return pl.pallas_call(
        functools.partial(_ffn_kernel, chunk=chunk, final=final, mixed=mix is not None),
        grid=(m // tm,),
        in_specs=[row(d)] + mix_specs + [const(1, d), const(d, dff), const(d, dff), const(dff, d), const(1, d)],
        out_specs=row(d),
        out_shape=jax.ShapeDtypeStruct((m, d), F32),
        scratch_shapes=[pltpu.VMEM((tm, dff), BF16)],
        compiler_params=pltpu.CompilerParams(
            dimension_semantics=("arbitrary",), vmem_limit_bytes=VMEM_LIMIT),
        name="ffn_mix" if mix is not None else "ffn",
    )(x, *(mix or ()), gain, wg, wu, wd, final_gain)


def _inproj_kernel(x_ref, g_ref, w_ref, o_ref, *, n_chunks, chunk):
    xn = _rms(x_ref[...], g_ref[...]).astype(BF16)
    for c in range(n_chunks):
        sl = slice(c * chunk, (c + 1) * chunk)
        o_ref[:, sl] = jnp.dot(xn, w_ref[:, sl], preferred_element_type=F32).astype(BF16)


def _inproj(x, gain, w, *, tm=1024, chunk=512):
    m, d = x.shape
    n = w.shape[1]
    return pl.pallas_call(
        functools.partial(_inproj_kernel, n_chunks=n // chunk, chunk=chunk),
        grid=(m // tm,),
        in_specs=[
            pl.BlockSpec((tm, d), lambda i: (i, 0)),
            pl.BlockSpec((1, d), lambda i: (0, 0), pipeline_mode=pl.Buffered(1)),
            pl.BlockSpec((d, n), lambda i: (0, 0), pipeline_mode=pl.Buffered(1)),
        ],
        out_specs=pl.BlockSpec((tm, n), lambda i: (i, 0)),
        out_shape=jax.ShapeDtypeStruct((m, n), BF16),
        compiler_params=pltpu.CompilerParams(
            dimension_semantics=("arbitrary",), vmem_limit_bytes=VMEM_LIMIT),
        name="inproj",
    )(x, gain, w)


def _diff_attn_kernel(q_ref, k_ref, v_ref, bias_ref, lq1_ref, lk1_ref, lq2_ref, lk2_ref, sg_ref,
                      o_ref, vt_ref, qs_ref, acc_ref, m_ref, al_ref, mxa_ref, mxb_ref,
                      sa_ref, sb_ref, ea_ref, eb_ref, *, lam0, n_q):
    tq, tk = ATT_TQ, ATT_TK
    hw = 2 * HEAD_DIM
    n_tiles = n_q * (n_q + 1) // 2

    ones_row = (lax.broadcasted_iota(jnp.int32, (ATT_PAD, tk), 0) == 0).astype(BF16)
    row = lax.broadcasted_iota(jnp.int32, (hw, tq), 0)
    for c in range(n_q):
        blk = slice(c * tk, (c + 1) * tk)
        vt_ref[c, 0:hw, :] = v_ref[0, blk, :].astype(F32).T.astype(BF16)
        vt_ref[c, hw:, :] = ones_row
        qt = q_ref[0, blk, :].astype(F32).T
        qs_ref[c] = jnp.concatenate(
            [jnp.where(row < HEAD_DIM, qt, 0.0), jnp.where(row >= HEAD_DIM, qt, 0.0)], axis=1).astype(BF16)
        acc_ref[c] = jnp.zeros(acc_ref.shape[1:], F32)

    cw = ATT_CW
    chunks = [slice(c * cw, (c + 1) * cw) for c in range(2 * tq // cw)]

    def step_tile(tile):
        qb, j = tile
        wrap = j == qb
        end = wrap & (qb == n_q - 1)
        return (jnp.where(wrap & jnp.logical_not(end), qb + 1, qb),
                jnp.where(end, j, jnp.where(wrap, 0, j + 1)))

    def score_chunk(tile, cols, s_ref, mx_ref):
        qb, j = tile
        kblk = k_ref[0, pl.ds(pl.multiple_of(j * tk, tk), tk), :]
        bi = jnp.minimum(qb - j, N_BIAS_TILES - 1)
        t = jnp.dot(kblk, qs_ref[qb, :, cols], preferred_element_type=F32) + bias_ref[0, bi, :, cols]
        s_ref[:, cols] = t
        mx_ref[:, cols] = jnp.max(t, axis=0, keepdims=True)

    def pv_chunk(tile, cols, e_ref):
        qb, j = tile
        acc_ref[qb, :, cols] = (al_ref[:, cols] * acc_ref[qb, :, cols]
                                + jnp.dot(vt_ref[j], e_ref[:, cols], preferred_element_type=F32))

    def softmax_chunk(tile, cols, s_ref, mx_ref, e_ref):
        m_old = jnp.where(tile[1] == 0, NEG, m_ref[:, cols])
        m_new = jnp.maximum(m_old, mx_ref[:, cols])
        al_ref[:, cols] = jnp.exp2(m_old - m_new)
        m_ref[:, cols] = m_new
        e_ref[:, cols] = jnp.exp2(s_ref[:, cols] - m_new).astype(BF16)

    def phase(prv, cur, nxt, cur_buf, nxt_buf):
        for cols in chunks:
            score_chunk(nxt, cols, nxt_buf[0], nxt_buf[1])
            pv_chunk(prv, cols, nxt_buf[2])
            softmax_chunk(cur, cols, *cur_buf)

    buf_a = (sa_ref, mxa_ref, ea_ref)
    buf_b = (sb_ref, mxb_ref, eb_ref)
    first = (jnp.int32(0), jnp.int32(0))
    m_ref[...] = jnp.full(m_ref.shape, NEG, F32)
    al_ref[...] = jnp.ones(al_ref.shape, F32)
    eb_ref[...] = jnp.zeros(eb_ref.shape, BF16)
    for cols in chunks:
        score_chunk(first, cols, sa_ref, mxa_ref)

    pairs = max(p for p in (1, 2, 6) if n_tiles % (2 * p) == 0)

    def body(i, carry):
        prv, t0 = carry
        for _ in range(pairs):
            t1 = step_tile(t0)
            t2 = step_tile(t1)
            phase(prv, t0, t1, buf_a, buf_b)
            phase(t0, t1, t2, buf_b, buf_a)
            prv, t0 = t1, t2
        return prv, t0

    last, _ = lax.fori_loop(0, n_tiles // (2 * pairs), body, (first, first))
    for cols in chunks:
        pv_chunk(last, cols, eb_ref)

    lam = (jnp.exp(jnp.sum(lq1_ref[...] * lk1_ref[...])) - jnp.exp(jnp.sum(lq2_ref[...] * lk2_ref[...])) + lam0)
    for c in range(n_q):
        o = acc_ref[c, 0:hw, :] / acc_ref[c, hw:hw + 1, :]
        o = o[:, :tq] - lam * o[:, tq:]
        y = o * lax.rsqrt(jnp.mean(o * o, axis=0, keepdims=True) + EPS) * sg_ref[...]
        y = y * (1.0 - lam0)
        o_ref[0, c * tq:(c + 1) * tq, :] = y.T.astype(BF16)


def _diff_attn(qkv, bias_tiles, lq1, lk1, lq2, lk2, sub_gain, *, lam0):
    b, s, _ = qkv.shape
    tq, tk = ATT_TQ, ATT_TK
    hw = 2 * HEAD_DIM
    n_q = s // tq
    assert ATT_R == 1 and (n_q * (n_q + 1) // 2) % 2 == 0
    small = lambda shape: pl.BlockSpec(shape, lambda h, bb: (0,) * len(shape))
    return pl.pallas_call(
        functools.partial(_diff_attn_kernel, lam0=lam0, n_q=n_q),
        grid=(N_DIFF_HEADS, b),
        in_specs=[
            pl.BlockSpec((1, s, hw), lambda h, bb: (bb, 0, h)),
            pl.BlockSpec((1, s, hw), lambda h, bb: (bb, 0, N_DIFF_HEADS + h)),
            pl.BlockSpec((1, s, hw), lambda h, bb: (bb, 0, 2 * N_DIFF_HEADS + h)),
            pl.BlockSpec((1, N_BIAS_TILES, tk, 2 * tq), lambda h, bb: (h, 0, 0, 0),
                         pipeline_mode=pl.Buffered(1)),
            small((1, HEAD_DIM)), small((1, HEAD_DIM)), small((1, HEAD_DIM)), small((1, HEAD_DIM)),
            small((hw, 1)),
        ],
        out_specs=pl.BlockSpec((1, s, hw), lambda h, bb: (bb, 0, h)),
        out_shape=jax.ShapeDtypeStruct((b, s, DIFF_WIDTH), BF16),
        scratch_shapes=[
            pltpu.VMEM((n_q, hw + ATT_PAD, tk), BF16),
            pltpu.VMEM((n_q, hw, 2 * tq), BF16),
            pltpu.VMEM((n_q, hw + ATT_PAD, 2 * tq), F32),
            pltpu.VMEM((1, 2 * tq), F32),
            pltpu.VMEM((1, 2 * tq), F32),
            pltpu.VMEM((1, 2 * tq), F32),
            pltpu.VMEM((1, 2 * tq), F32),
            pltpu.VMEM((tk, 2 * tq), F32),
            pltpu.VMEM((tk, 2 * tq), F32),
            pltpu.VMEM((tk, 2 * tq), BF16),
            pltpu.VMEM((tk, 2 * tq), BF16),
        ],
        compiler_params=pltpu.CompilerParams(
            dimension_semantics=("arbitrary", "arbitrary"), vmem_limit_bytes=VMEM_LIMIT),
        name="diff_attn",
    )(qkv, qkv, qkv, bias_tiles, lq1, lk1, lq2, lk2, sub_gain)


def _dilated_kernel(q_ref, k_ref, v_ref, bias_ref, o_ref, xf_ref, x4_ref, nat_ref):
    w = DIL_W
    s_len = q_ref.shape[1]
    l4 = s_len // 4
    srcs = (q_ref, k_ref, v_ref)
    cp = 512
    for t in range(3):
        for c in range(s_len // cp):
            xf_ref[t, c * cp:(c + 1) * cp, :] = srcs[t][0, c * cp:(c + 1) * cp, :].astype(F32)
    for t in range(3):
        for r in range(4):
            for c in range(l4 // cp):
                x4_ref[t, r * l4 + c * cp:r * l4 + (c + 1) * cp, :] = (
                    xf_ref[t, pl.ds(r + 4 * c * cp, cp, stride=4), :])

    low = lax.broadcasted_iota(jnp.int32, (w, LANES), 1) < HEAD_DIM
    nt = (((1,), (1,)), ((), ()))

    def rows(t, br, r, j0, n):
        if br == 0:
            return srcs[t][0, pl.ds(pl.multiple_of(j0, w), n), :]
        if br == 1:
            return x4_ref[t, pl.ds(pl.multiple_of(r * l4 + j0, w), n), :].astype(BF16)
        return x4_ref[t, pl.ds((r % 4) * l4 + r // 4 + 4 * j0, n, stride=4), :].astype(BF16)

    def attend(br, r, n):
        k0 = jnp.maximum(n - 1, 0) * w
        kind = jnp.where(n == 0, 1, 0)
        qp = rows(0, br, r, n * w, w)
        kk = rows(1, br, r, k0, 2 * w)
        vv = jnp.concatenate([rows(2, br, r, k0, 2 * w), jnp.ones((2 * w, LANES), BF16)], axis=1)
        pvs, ms = [], []
        for hh in range(2):
            qm = jnp.where(low if hh == 0 else jnp.logical_not(low), qp, jnp.zeros_like(qp))
            s = lax.dot_general(qm, kk, nt, preferred_element_type=F32) + bias_ref[br, hh, kind]
            m = jnp.max(s, axis=1, keepdims=True)
            e = jnp.exp2((s - m).astype(BF16))
            pvs.append(jnp.dot(e, vv, preferred_element_type=F32))
            ms.append(m)
        dil = DILATED_CONFIGS[br][1]
        start = r + dil * w * n
        dst = pl.ds(pl.multiple_of(start, w), w) if dil == 1 else pl.ds(start, w, stride=dil)
        nat_ref[br, 0, dst, :] = jnp.where(low, pvs[0][:, :LANES], pvs[1][:, :LANES])
        nat_ref[br, 1, dst, :] = jnp.where(low, ms[0], ms[1])
        nat_ref[br, 2, dst, :] = jnp.where(low, pvs[0][:, LANES:], pvs[1][:, LANES:])

    for br, (_, dil) in enumerate(DILATED_CONFIGS):
        nblk = s_len // dil // w

        def blocks(i, carry, br=br, nblk=nblk):
            for u in range(DIL_UNROLL):
                idx = i * DIL_UNROLL + u
                attend(br, idx // nblk, idx % nblk)
            return carry

        lax.fori_loop(0, dil * nblk // DIL_UNROLL, blocks, 0)

    for c in range(s_len // cp):
        sl = slice(c * cp, (c + 1) * cp)
        ms = [nat_ref[br, 1, sl, :] for br in range(3)]
        top = jnp.maximum(jnp.maximum(ms[0], ms[1]), ms[2])
        wts = [jnp.exp2(v - top) for v in ms]
        num = wts[0] * nat_ref[0, 0, sl, :] + wts[1] * nat_ref[1, 0, sl, :] + wts[2] * nat_ref[2, 0, sl, :]
        den = wts[0] * nat_ref[0, 2, sl, :] + wts[1] * nat_ref[1, 2, sl, :] + wts[2] * nat_ref[2, 2, sl, :]
        o_ref[0, sl, :] = (num / den).astype(BF16)


def _dilated_attn(qkv, bias):
    b, s, _ = qkv.shape
    w = DIL_W
    col0 = 3 * DIFF_WIDTH // LANES
    pairs = DIL_WIDTH // LANES
    part = lambda t: pl.BlockSpec((1, s, LANES), lambda bb, p: (bb, 0, col0 + t * pairs + p))
    return pl.pallas_call(
        _dilated_kernel,
        grid=(b, pairs),
        in_specs=[part(0), part(1), part(2),
                  pl.BlockSpec((len(DILATED_CONFIGS), 2, 2, w, 2 * w), lambda bb, p: (0, p, 0, 0, 0))],
        out_specs=pl.BlockSpec((1, s, LANES), lambda bb, p: (bb, 0, p)),
        out_shape=jax.ShapeDtypeStruct((b, s, DIL_WIDTH), BF16),
        scratch_shapes=[
            pltpu.VMEM((3, s, LANES), F32),
            pltpu.VMEM((3, s, LANES), F32),
            pltpu.VMEM((len(DILATED_CONFIGS), 3, s, LANES), F32),
        ],
        compiler_params=pltpu.CompilerParams(
            dimension_semantics=("arbitrary", "arbitrary"), vmem_limit_bytes=VMEM_LIMIT),
        name="dilated_attn",
    )(qkv, qkv, qkv, bias)


def kernel(x, ffn1_norm, ffn1_w_gate, ffn1_w_up, ffn1_w_down, mix_norm, w_in, lambda_q1, lambda_k1,
           lambda_q2, lambda_k2, subln_gain, w_out, ffn2_norm, ffn2_w_gate, ffn2_w_up, ffn2_w_down,
           rel_bias, final_norm):
    b, s, d = x.shape
    depth = w_in.shape[0]
    m = b * s
    diff_bias = _diff_bias_tiles(rel_bias)
    dil_bias = _dil_bias_tiles(rel_bias)
    qscale = jnp.concatenate([
        jnp.full((DIFF_WIDTH,), HEAD_DIM ** -0.5 * LOG2E, F32), jnp.ones((2 * DIFF_WIDTH,), F32),
        jnp.full((DIL_WIDTH,), HEAD_DIM ** -0.5 * LOG2E, F32), jnp.ones((2 * DIL_WIDTH,), F32)])
    h = x.reshape(m, d)
    for layer in range(depth):
        row = lambda v: v[layer].reshape(1, -1)
        h = _ffn(h, row(ffn1_norm), ffn1_w_gate[layer], ffn1_w_up[layer],
                 ffn1_w_down[layer].astype(BF16), row(ffn1_norm), final=False)
        qkv = _inproj(h, row(mix_norm), (w_in[layer] * qscale).astype(BF16)).reshape(b, s, -1)
        oa = _diff_attn(qkv, diff_bias, row(lambda_q1), row(lambda_k1), row(lambda_q2), row(lambda_k2),
                        subln_gain[layer].reshape(-1, 1), lam0=_lambda_init(layer))
        ob = _dilated_attn(qkv, dil_bias)
        last = layer == depth - 1
        h = _ffn(h, row(ffn2_norm), ffn2_w_gate[layer], ffn2_w_up[layer],
                 ffn2_w_down[layer].astype(BF16), final_norm.reshape(1, -1), final=last,
                 mix=(oa.reshape(m, DIFF_WIDTH), ob.reshape(m, DIL_WIDTH), w_out[layer].astype(BF16)))
    return h.reshape(b, s, d)
```

```python
import functools
import math

import numpy as np
import jax
import jax.numpy as jnp
from jax import lax
from jax.experimental import pallas as pl
from jax.experimental.pallas import tpu as pltpu

HEAD_DIM = 64
N_DIFF_HEADS = 4
N_DIL_HEADS = 8
DIFF_WIDTH = 512
DIL_WIDTH = 512
DILATED_CONFIGS = ((128, 1), (512, 4), (2048, 16))
NUM_BUCKETS = 32
MAX_DISTANCE = 2048
EPS = 1e-6
NEG = -1e30

LANES = 128
DIL_W = 128
DIL_UNROLL = 16
ATT_TQ = 512
ATT_TK = 512
ATT_R = ATT_TQ // ATT_TK
ATT_PAD = 16
ATT_CW = 256
VMEM_LIMIT = 56 * 1024 * 1024
LOG2E = math.log2(math.e)

F32 = jnp.float32
BF16 = jnp.bfloat16


def _bucket_lower_bounds():
    d = np.arange(0, 2 * MAX_DISTANCE + 1)
    max_exact = NUM_BUCKETS // 2
    v = np.log(np.maximum(d, 1) / max_exact) / math.log(MAX_DISTANCE / max_exact) * (NUM_BUCKETS - max_exact)
    b = np.where(d < max_exact, d, np.minimum(max_exact + v.astype(np.int64), NUM_BUCKETS - 1))
    return [int(np.argmax(b == k)) for k in range(NUM_BUCKETS)]


BUCKET_LO = _bucket_lower_bounds()
N_BIAS_TILES = ATT_R + -(-(BUCKET_LO[-1] + ATT_TK - 1) // ATT_TK)
BIAS_ROLL_W = pl.next_power_of_2(ATT_TQ + ATT_TK)


def _lambda_init(layer):
    return 0.8 - 0.6 * math.exp(-0.3 * layer)


def _bias_of_distance(rb_ref, col, dist):
    val = jnp.full(dist.shape, rb_ref[0, col], F32)
    for b in range(1, NUM_BUCKETS):
        val = jnp.where(dist >= BUCKET_LO[b], rb_ref[b, col], val)
    return val


def _diff_bias_kernel(rb_ref, out_ref):
    h = pl.program_id(0)
    delta = (pl.program_id(1) - (ATT_R - 1)) * ATT_TK
    tq, tk = ATT_TQ, ATT_TK
    u = lax.broadcasted_iota(jnp.int32, (8, BIAS_ROLL_W), 1)
    d = u + (delta - tk)
    for m in range(2):
        val = _bias_of_distance(rb_ref, 2 * h + m, d) * LOG2E
        val = jnp.where(d < 0, NEG, val)
        x = jnp.broadcast_to(val[0:1, :], (tk, BIAS_ROLL_W))
        rolled = pltpu.roll(x, 0, 1, stride=1, stride_axis=0)
        out_ref[0, 0, :, m * tq:(m + 1) * tq] = rolled[:, tk:tk + tq]


def _diff_bias_tiles(rel_bias):
    tq, tk = ATT_TQ, ATT_TK
    return pl.pallas_call(
        _diff_bias_kernel,
        grid=(N_DIFF_HEADS, N_BIAS_TILES),
        in_specs=[pl.BlockSpec(memory_space=pltpu.SMEM)],
        out_specs=pl.BlockSpec((1, 1, tk, 2 * tq), lambda h, d: (h, d, 0, 0)),
        out_shape=jax.ShapeDtypeStruct((N_DIFF_HEADS, N_BIAS_TILES, tk, 2 * tq), F32),
        name="diff_bias_tiles",
    )(rel_bias)


def _dil_bias_kernel(rb_ref, out_ref):
    br = pl.program_id(0)
    w = DIL_W
    dil = jnp.where(br == 0, DILATED_CONFIGS[0][1],
                    jnp.where(br == 1, DILATED_CONFIGS[1][1], DILATED_CONFIGS[2][1]))
    u = lax.broadcasted_iota(jnp.int32, (8, 4 * w), 1)
    for h in range(N_DIL_HEADS):
        for kind in range(2):
            dist = (2 - kind) * w - u
            val = _bias_of_distance(rb_ref, 2 * N_DIFF_HEADS + h, dist * dil) * LOG2E
            val = jnp.where((dist >= 0) & (dist <= w), val, NEG)
            x = jnp.broadcast_to(val[0:1, :], (w, 4 * w))
            rolled = pltpu.roll(x, 0, 1, stride=1, stride_axis=0)
            out_ref[0, h, kind] = rolled[:, w:3 * w]


def _dil_bias_tiles(rel_bias):
    w = DIL_W
    nb = len(DILATED_CONFIGS)
    return pl.pallas_call(
        _dil_bias_kernel,
        grid=(nb,),
        in_specs=[pl.BlockSpec(memory_space=pltpu.SMEM)],
        out_specs=pl.BlockSpec((1, N_DIL_HEADS, 2, w, 2 * w), lambda b: (b, 0, 0, 0, 0)),
        out_shape=jax.ShapeDtypeStruct((nb, N_DIL_HEADS, 2, w, 2 * w), F32),
        name="dil_bias_tiles",
    )(rel_bias)


def _rms(x, g):
    return x * lax.rsqrt(jnp.mean(x * x, axis=-1, keepdims=True) + EPS) * g


def _ffn_kernel(*refs, chunk, final, mixed):
    if mixed:
        x_ref, oa_ref, ob_ref, wo_ref, g_ref, wg_ref, wu_ref, wd_ref, fg_ref, o_ref, h_ref = refs
        x = (x_ref[...] + jnp.dot(oa_ref[...], wo_ref[0:DIFF_WIDTH, :], preferred_element_type=F32)
             + jnp.dot(ob_ref[...], wo_ref[DIFF_WIDTH:, :], preferred_element_type=F32))
    else:
        x_ref, g_ref, wg_ref, wu_ref, wd_ref, fg_ref, o_ref, h_ref = refs
        x = x_ref[...]
    xn = _rms(x, g_ref[...]).astype(BF16)
    for c in range(wg_ref.shape[1] // chunk):
        sl = slice(c * chunk, (c + 1) * chunk)
        gate = jnp.dot(xn, wg_ref[:, sl].astype(BF16), preferred_element_type=F32)
        up = jnp.dot(xn, wu_ref[:, sl].astype(BF16), preferred_element_type=F32)
        h_ref[:, sl] = (gate * jax.nn.sigmoid(gate) * up).astype(BF16)
    y = x + 0.5 * jnp.dot(h_ref[...], wd_ref[...], preferred_element_type=F32)
    if final:
        y = _rms(y, fg_ref[...])
    o_ref[...] = y


def _layer_spec(arr, layer):
    tail = arr.shape[1:]
    return pl.BlockSpec((None,) + tail, lambda i: (layer,) + (0,) * len(tail), pipeline_mode=pl.Buffered(1))


def _ffn(x, layer, gain, wg, wu, wd, final_gain, *, final, mix=None, tm=512, chunk=256):
    m, d = x.shape
    dff = wg.shape[2]
    row = lambda n: pl.BlockSpec((tm, n), lambda i: (i, 0))
    mix_specs = [] if mix is None else [row(mix[0].shape[1]), row(mix[1].shape[1]), _layer_spec(mix[2], layer)]
    return pl.pallas_call(
        functools.partial(_ffn_kernel, chunk=chunk, final=final, mixed=mix is not None),
        grid=(m // tm,),
        in_specs=[row(d)] + mix_specs + [_layer_spec(gain, layer), _layer_spec(wg, layer), _layer_spec(wu, layer),
                                         _layer_spec(wd, layer), _layer_spec(final_gain, 0)],
        out_specs=row(d),
        out_shape=jax.ShapeDtypeStruct((m, d), F32),
        scratch_shapes=[pltpu.VMEM((tm, dff), BF16)],
        compiler_params=pltpu.CompilerParams(
            dimension_semantics=("arbitrary",), vmem_limit_bytes=VMEM_LIMIT),
        name="ffn_mix" if mix is not None else "ffn",
    )(x, *(mix or ()), gain, wg, wu, wd, final_gain)


def _inproj_kernel(x_ref, g_ref, w_ref, o_ref, *, n_chunks, chunk):
    xn = _rms(x_ref[...], g_ref[...]).astype(BF16)
    for c in range(n_chunks):
        sl = slice(c * chunk, (c + 1) * chunk)
        o_ref[:, sl] = jnp.dot(xn, w_ref[:, sl], preferred_element_type=F32).astype(BF16)


def _inproj(x, layer, gain, w, *, tm=1024, chunk=512):
    m, d = x.shape
    n = w.shape[2]
    return pl.pallas_call(
        functools.partial(_inproj_kernel, n_chunks=n // chunk, chunk=chunk),
        grid=(m // tm,),
        in_specs=[pl.BlockSpec((tm, d), lambda i: (i, 0)), _layer_spec(gain, layer), _layer_spec(w, layer)],
        out_specs=pl.BlockSpec((tm, n), lambda i: (i, 0)),
        out_shape=jax.ShapeDtypeStruct((m, n), BF16),
        compiler_params=pltpu.CompilerParams(
            dimension_semantics=("arbitrary",), vmem_limit_bytes=VMEM_LIMIT),
        name="inproj",
    )(x, gain, w)


def _diff_attn_kernel(q_ref, k_ref, v_ref, bias_ref, lq1_ref, lk1_ref, lq2_ref, lk2_ref, sg_ref,
                      o_ref, vt_ref, qs_ref, acc_ref, m_ref, al_ref, mxa_ref, mxb_ref,
                      sa_ref, sb_ref, ea_ref, eb_ref, *, lam0, n_q):
    tq, tk = ATT_TQ, ATT_TK
    hw = 2 * HEAD_DIM
    n_tiles = n_q * (n_q + 1) // 2

    ones_row = (lax.broadcasted_iota(jnp.int32, (ATT_PAD, tk), 0) == 0).astype(BF16)
    row = lax.broadcasted_iota(jnp.int32, (hw, tq), 0)
    for c in range(n_q):
        blk = slice(c * tk, (c + 1) * tk)
        vt_ref[c, 0:hw, :] = v_ref[0, blk, :].astype(F32).T.astype(BF16)
        vt_ref[c, hw:, :] = ones_row
        qt = q_ref[0, blk, :].astype(F32).T
        qs_ref[c] = jnp.concatenate(
            [jnp.where(row < HEAD_DIM, qt, 0.0), jnp.where(row >= HEAD_DIM, qt, 0.0)], axis=1).astype(BF16)
        acc_ref[c] = jnp.zeros(acc_ref.shape[1:], F32)

    cw = ATT_CW
    chunks = [slice(c * cw, (c + 1) * cw) for c in range(2 * tq // cw)]

    def step_tile(tile):
        qb, j = tile
        wrap = j == qb
        end = wrap & (qb == n_q - 1)
        return (jnp.where(wrap & jnp.logical_not(end), qb + 1, qb),
                jnp.where(end, j, jnp.where(wrap, 0, j + 1)))

    def score_chunk(tile, cols, s_ref, mx_ref):
        qb, j = tile
        kblk = k_ref[0, pl.ds(pl.multiple_of(j * tk, tk), tk), :]
        bi = jnp.minimum(qb - j, N_BIAS_TILES - 1)
        t = jnp.dot(kblk, qs_ref[qb, :, cols], preferred_element_type=F32) + bias_ref[0, bi, :, cols]
        s_ref[:, cols] = t
        mx_ref[:, cols] = jnp.max(t, axis=0, keepdims=True)

    def pv_chunk(tile, cols, e_ref):
        qb, j = tile
        acc_ref[qb, :, cols] = (al_ref[:, cols] * acc_ref[qb, :, cols]
                                + jnp.dot(vt_ref[j], e_ref[:, cols], preferred_element_type=F32))

    def softmax_chunk(tile, cols, s_ref, mx_ref, e_ref):
        m_old = jnp.where(tile[1] == 0, NEG, m_ref[:, cols])
        m_new = jnp.maximum(m_old, mx_ref[:, cols])
        al_ref[:, cols] = jnp.exp2(m_old - m_new)
        m_ref[:, cols] = m_new
        e_ref[:, cols] = jnp.exp2(s_ref[:, cols] - m_new).astype(BF16)

    def phase(prv, cur, nxt, cur_buf, nxt_buf):
        for cols in chunks:
            score_chunk(nxt, cols, nxt_buf[0], nxt_buf[1])
            pv_chunk(prv, cols, nxt_buf[2])
            softmax_chunk(cur, cols, *cur_buf)

    buf_a = (sa_ref, mxa_ref, ea_ref)
    buf_b = (sb_ref, mxb_ref, eb_ref)
    first = (jnp.int32(0), jnp.int32(0))
    m_ref[...] = jnp.full(m_ref.shape, NEG, F32)
    al_ref[...] = jnp.ones(al_ref.shape, F32)
    eb_ref[...] = jnp.zeros(eb_ref.shape, BF16)
    for cols in chunks:
        score_chunk(first, cols, sa_ref, mxa_ref)

    pairs = max(p for p in (1, 2, 6) if n_tiles % (2 * p) == 0)

    def body(i, carry):
        prv, t0 = carry
        for _ in range(pairs):
            t1 = step_tile(t0)
            t2 = step_tile(t1)
            phase(prv, t0, t1, buf_a, buf_b)
            phase(t0, t1, t2, buf_b, buf_a)
            prv, t0 = t1, t2
        return prv, t0

    last, _ = lax.fori_loop(0, n_tiles // (2 * pairs), body, (first, first))
    for cols in chunks:
        pv_chunk(last, cols, eb_ref)

    lam = (jnp.exp(jnp.sum(lq1_ref[...] * lk1_ref[...])) - jnp.exp(jnp.sum(lq2_ref[...] * lk2_ref[...])) + lam0)
    for c in range(n_q):
        o = acc_ref[c, 0:hw, :] / acc_ref[c, hw:hw + 1, :]
        o = o[:, :tq] - lam * o[:, tq:]
        y = o * lax.rsqrt(jnp.mean(o * o, axis=0, keepdims=True) + EPS) * sg_ref[...]
        y = y * (1.0 - lam0)
        o_ref[0, c * tq:(c + 1) * tq, :] = y.T.astype(BF16)


def _diff_attn(qkv, bias_tiles, lq1, lk1, lq2, lk2, sub_gain, *, lam0):
    b, s, _ = qkv.shape
    tq, tk = ATT_TQ, ATT_TK
    hw = 2 * HEAD_DIM
    n_q = s // tq
    assert ATT_R == 1 and (n_q * (n_q + 1) // 2) % 2 == 0
    small = lambda shape: pl.BlockSpec(shape, lambda h, bb: (0,) * len(shape))
    return pl.pallas_call(
        functools.partial(_diff_attn_kernel, lam0=lam0, n_q=n_q),
        grid=(N_DIFF_HEADS, b),
        in_specs=[
            pl.BlockSpec((1, s, hw), lambda h, bb: (bb, 0, h)),
            pl.BlockSpec((1, s, hw), lambda h, bb: (bb, 0, N_DIFF_HEADS + h)),
            pl.BlockSpec((1, s, hw), lambda h, bb: (bb, 0, 2 * N_DIFF_HEADS + h)),
            pl.BlockSpec((1, N_BIAS_TILES, tk, 2 * tq), lambda h, bb: (h, 0, 0, 0),
                         pipeline_mode=pl.Buffered(1)),
            small((1, HEAD_DIM)), small((1, HEAD_DIM)), small((1, HEAD_DIM)), small((1, HEAD_DIM)),
            small((hw, 1)),
        ],
        out_specs=pl.BlockSpec((1, s, hw), lambda h, bb: (bb, 0, h)),
        out_shape=jax.ShapeDtypeStruct((b, s, DIFF_WIDTH), BF16),
        scratch_shapes=[
            pltpu.VMEM((n_q, hw + ATT_PAD, tk), BF16),
            pltpu.VMEM((n_q, hw, 2 * tq), BF16),
            pltpu.VMEM((n_q, hw + ATT_PAD, 2 * tq), F32),
            pltpu.VMEM((1, 2 * tq), F32),
            pltpu.VMEM((1, 2 * tq), F32),
            pltpu.VMEM((1, 2 * tq), F32),
            pltpu.VMEM((1, 2 * tq), F32),
            pltpu.VMEM((tk, 2 * tq), F32),
            pltpu.VMEM((tk, 2 * tq), F32),
            pltpu.VMEM((tk, 2 * tq), BF16),
            pltpu.VMEM((tk, 2 * tq), BF16),
        ],
        compiler_params=pltpu.CompilerParams(
            dimension_semantics=("arbitrary", "arbitrary"), vmem_limit_bytes=VMEM_LIMIT),
        name="diff_attn",
    )(qkv, qkv, qkv, bias_tiles, lq1, lk1, lq2, lk2, sub_gain)


def _dilated_kernel(q_ref, k_ref, v_ref, bias_ref, o_ref, xf_ref, x4_ref, nat_ref):
    w = DIL_W
    s_len = q_ref.shape[1]
    l4 = s_len // 4
    srcs = (q_ref, k_ref, v_ref)
    cp = 512
    for t in range(3):
        for c in range(s_len // cp):
            xf_ref[t, c * cp:(c + 1) * cp, :] = srcs[t][0, c * cp:(c + 1) * cp, :].astype(F32)
    for t in range(3):
        for r in range(4):
            for c in range(l4 // cp):
                x4_ref[t, r * l4 + c * cp:r * l4 + (c + 1) * cp, :] = (
                    xf_ref[t, pl.ds(r + 4 * c * cp, cp, stride=4), :])

    low = lax.broadcasted_iota(jnp.int32, (w, LANES), 1) < HEAD_DIM
    nt = (((1,), (1,)), ((), ()))

    def rows(t, br, r, j0, n):
        if br == 0:
            return srcs[t][0, pl.ds(pl.multiple_of(j0, w), n), :]
        if br == 1:
            return x4_ref[t, pl.ds(pl.multiple_of(r * l4 + j0, w), n), :].astype(BF16)
        return x4_ref[t, pl.ds((r % 4) * l4 + r // 4 + 4 * j0, n, stride=4), :].astype(BF16)

    def attend(br, r, n):
        k0 = jnp.maximum(n - 1, 0) * w
        kind = jnp.where(n == 0, 1, 0)
        qp = rows(0, br, r, n * w, w)
        kk = rows(1, br, r, k0, 2 * w)
        vv = jnp.concatenate([rows(2, br, r, k0, 2 * w), jnp.ones((2 * w, LANES), BF16)], axis=1)
        pvs, ms = [], []
        for hh in range(2):
            qm = jnp.where(low if hh == 0 else jnp.logical_not(low), qp, jnp.zeros_like(qp))
            s = lax.dot_general(qm, kk, nt, preferred_element_type=F32) + bias_ref[br, hh, kind]
            m = jnp.max(s, axis=1, keepdims=True)
            e = jnp.exp2((s - m).astype(BF16))
            pvs.append(jnp.dot(e, vv, preferred_element_type=F32))
            ms.append(m)
        dil = DILATED_CONFIGS[br][1]
        start = r + dil * w * n
        dst = pl.ds(pl.multiple_of(start, w), w) if dil == 1 else pl.ds(start, w, stride=dil)
        nat_ref[br, 0, dst, :] = jnp.where(low, pvs[0][:, :LANES], pvs[1][:, :LANES])
        nat_ref[br, 1, dst, :] = jnp.where(low, ms[0], ms[1])
        nat_ref[br, 2, dst, :] = jnp.where(low, pvs[0][:, LANES:], pvs[1][:, LANES:])

    for br, (_, dil) in enumerate(DILATED_CONFIGS):
        nblk = s_len // dil // w

        def blocks(i, carry, br=br, nblk=nblk):
            for u in range(DIL_UNROLL):
                idx = i * DIL_UNROLL + u
                attend(br, idx // nblk, idx % nblk)
            return carry

        lax.fori_loop(0, dil * nblk // DIL_UNROLL, blocks, 0)

    for c in range(s_len // cp):
        sl = slice(c * cp, (c + 1) * cp)
        ms = [nat_ref[br, 1, sl, :] for br in range(3)]
        top = jnp.maximum(jnp.maximum(ms[0], ms[1]), ms[2])
        wts = [jnp.exp2(v - top) for v in ms]
        num = wts[0] * nat_ref[0, 0, sl, :] + wts[1] * nat_ref[1, 0, sl, :] + wts[2] * nat_ref[2, 0, sl, :]
        den = wts[0] * nat_ref[0, 2, sl, :] + wts[1] * nat_ref[1, 2, sl, :] + wts[2] * nat_ref[2, 2, sl, :]
        o_ref[0, sl, :] = (num / den).astype(BF16)


def _dilated_attn(qkv, bias):
    b, s, _ = qkv.shape
    w = DIL_W
    col0 = 3 * DIFF_WIDTH // LANES
    pairs = DIL_WIDTH // LANES
    part = lambda t: pl.BlockSpec((1, s, LANES), lambda bb, p: (bb, 0, col0 + t * pairs + p))
    return pl.pallas_call(
        _dilated_kernel,
        grid=(b, pairs),
        in_specs=[part(0), part(1), part(2),
                  pl.BlockSpec((len(DILATED_CONFIGS), 2, 2, w, 2 * w), lambda bb, p: (0, p, 0, 0, 0))],
        out_specs=pl.BlockSpec((1, s, LANES), lambda bb, p: (bb, 0, p)),
        out_shape=jax.ShapeDtypeStruct((b, s, DIL_WIDTH), BF16),
        scratch_shapes=[
            pltpu.VMEM((3, s, LANES), F32),
            pltpu.VMEM((3, s, LANES), F32),
            pltpu.VMEM((len(DILATED_CONFIGS), 3, s, LANES), F32),
        ],
        compiler_params=pltpu.CompilerParams(
            dimension_semantics=("arbitrary", "arbitrary"), vmem_limit_bytes=VMEM_LIMIT),
        name="dilated_attn",
    )(qkv, qkv, qkv, bias)


def kernel(x, ffn1_norm, ffn1_w_gate, ffn1_w_up, ffn1_w_down, mix_norm, w_in, lambda_q1, lambda_k1,
           lambda_q2, lambda_k2, subln_gain, w_out, ffn2_norm, ffn2_w_gate, ffn2_w_up, ffn2_w_down,
           rel_bias, final_norm):
    b, s, d = x.shape
    depth = w_in.shape[0]
    m = b * s
    diff_bias = _diff_bias_tiles(rel_bias)
    dil_bias = _dil_bias_tiles(rel_bias)
    qscale = jnp.concatenate([
        jnp.full((DIFF_WIDTH,), HEAD_DIM ** -0.5 * LOG2E, F32), jnp.ones((2 * DIFF_WIDTH,), F32),
        jnp.full((DIL_WIDTH,), HEAD_DIM ** -0.5 * LOG2E, F32), jnp.ones((2 * DIL_WIDTH,), F32)])
    gains = lambda g: g.reshape(depth, 1, d)
    w_in16 = (w_in * qscale).astype(BF16)
    w_out16 = w_out.astype(BF16)
    wd1, wd2 = ffn1_w_down.astype(BF16), ffn2_w_down.astype(BF16)
    fin = final_norm.reshape(1, 1, d)
    h = x.reshape(m, d)
    for layer in range(depth):
        row = lambda v: v[layer].reshape(1, -1)
        h = _ffn(h, layer, gains(ffn1_norm), ffn1_w_gate, ffn1_w_up, wd1, fin, final=False)
        qkv = _inproj(h, layer, gains(mix_norm), w_in16).reshape(b, s, -1)
        oa = _diff_attn(qkv, diff_bias, row(lambda_q1), row(lambda_k1), row(lambda_q2), row(lambda_k2),
                        subln_gain[layer].reshape(-1, 1), lam0=_lambda_init(layer))
        ob = _dilated_attn(qkv, dil_bias)
        h = _ffn(h, layer, gains(ffn2_norm), ffn2_w_gate, ffn2_w_up, wd2, fin, final=layer == depth - 1,
                 mix=(oa.reshape(m, DIFF_WIDTH), ob.reshape(m, DIL_WIDTH), w_out16))
    return h.reshape(b, s, d)
```

```python
import functools
import math

import numpy as np
import jax
import jax.numpy as jnp
from jax import lax
from jax.experimental import pallas as pl
from jax.experimental.pallas import tpu as pltpu

HEAD_DIM = 64
N_DIFF_HEADS = 4
N_DIL_HEADS = 8
DIFF_WIDTH = 512
DIL_WIDTH = 512
DILATED_CONFIGS = ((128, 1), (512, 4), (2048, 16))
NUM_BUCKETS = 32
MAX_DISTANCE = 2048
EPS = 1e-6
NEG = -1e30

LANES = 128
DIL_W = 128
DIL_UNROLL = 16
ATT_TQ = 512
ATT_TK = 512
ATT_R = ATT_TQ // ATT_TK
ATT_PAD = 16
ATT_CW = 256
VMEM_LIMIT = 56 * 1024 * 1024
LOG2E = math.log2(math.e)

F32 = jnp.float32
BF16 = jnp.bfloat16


def _bucket_lower_bounds():
    d = np.arange(0, 2 * MAX_DISTANCE + 1)
    max_exact = NUM_BUCKETS // 2
    v = np.log(np.maximum(d, 1) / max_exact) / math.log(MAX_DISTANCE / max_exact) * (NUM_BUCKETS - max_exact)
    b = np.where(d < max_exact, d, np.minimum(max_exact + v.astype(np.int64), NUM_BUCKETS - 1))
    return [int(np.argmax(b == k)) for k in range(NUM_BUCKETS)]


BUCKET_LO = _bucket_lower_bounds()
N_BIAS_TILES = ATT_R + -(-(BUCKET_LO[-1] + ATT_TK - 1) // ATT_TK)
BIAS_ROLL_W = pl.next_power_of_2(ATT_TQ + ATT_TK)


def _lambda_init(layer):
    return 0.8 - 0.6 * math.exp(-0.3 * layer)


def _bias_of_distance(rb_ref, col, dist):
    val = jnp.full(dist.shape, rb_ref[0, col], F32)
    for b in range(1, NUM_BUCKETS):
        val = jnp.where(dist >= BUCKET_LO[b], rb_ref[b, col], val)
    return val


def _diff_bias_kernel(rb_ref, out_ref):
    h = pl.program_id(0)
    delta = (pl.program_id(1) - (ATT_R - 1)) * ATT_TK
    tq, tk = ATT_TQ, ATT_TK
    u = lax.broadcasted_iota(jnp.int32, (8, BIAS_ROLL_W), 1)
    d = u + (delta - tk)
    for m in range(2):
        val = _bias_of_distance(rb_ref, 2 * h + m, d) * LOG2E
        val = jnp.where(d < 0, NEG, val)
        x = jnp.broadcast_to(val[0:1, :], (tk, BIAS_ROLL_W))
        rolled = pltpu.roll(x, 0, 1, stride=1, stride_axis=0)
        out_ref[0, 0, :, m * tq:(m + 1) * tq] = rolled[:, tk:tk + tq]


def _diff_bias_tiles(rel_bias):
    tq, tk = ATT_TQ, ATT_TK
    return pl.pallas_call(
        _diff_bias_kernel,
        grid=(N_DIFF_HEADS, N_BIAS_TILES),
        in_specs=[pl.BlockSpec(memory_space=pltpu.SMEM)],
        out_specs=pl.BlockSpec((1, 1, tk, 2 * tq), lambda h, d: (h, d, 0, 0)),
        out_shape=jax.ShapeDtypeStruct((N_DIFF_HEADS, N_BIAS_TILES, tk, 2 * tq), F32),
        name="diff_bias_tiles",
    )(rel_bias)


def _dil_bias_kernel(rb_ref, out_ref):
    br = pl.program_id(0)
    w = DIL_W
    dil = jnp.where(br == 0, DILATED_CONFIGS[0][1],
                    jnp.where(br == 1, DILATED_CONFIGS[1][1], DILATED_CONFIGS[2][1]))
    u = lax.broadcasted_iota(jnp.int32, (8, 4 * w), 1)
    for h in range(N_DIL_HEADS):
        for kind in range(2):
            dist = (2 - kind) * w - u
            val = _bias_of_distance(rb_ref, 2 * N_DIFF_HEADS + h, dist * dil) * LOG2E
            val = jnp.where((dist >= 0) & (dist <= w), val, NEG)
            x = jnp.broadcast_to(val[0:1, :], (w, 4 * w))
            rolled = pltpu.roll(x, 0, 1, stride=1, stride_axis=0)
            out_ref[0, h // 2, kind, (h % 2) * w:(h % 2 + 1) * w, :] = rolled[:, w:3 * w]


def _dil_bias_tiles(rel_bias):
    w = DIL_W
    nb = len(DILATED_CONFIGS)
    return pl.pallas_call(
        _dil_bias_kernel,
        grid=(nb,),
        in_specs=[pl.BlockSpec(memory_space=pltpu.SMEM)],
        out_specs=pl.BlockSpec((1, N_DIL_HEADS // 2, 2, 2 * w, 2 * w), lambda b: (b, 0, 0, 0, 0)),
        out_shape=jax.ShapeDtypeStruct((nb, N_DIL_HEADS // 2, 2, 2 * w, 2 * w), F32),
        name="dil_bias_tiles",
    )(rel_bias)


def _rms(x, g):
    return x * lax.rsqrt(jnp.mean(x * x, axis=-1, keepdims=True) + EPS) * g


def _ffn_kernel(*refs, chunk, final, mixed):
    if mixed:
        x_ref, oa_ref, ob_ref, wo_ref, g_ref, wg_ref, wu_ref, wd_ref, fg_ref, o_ref, h_ref = refs
        x = (x_ref[...] + jnp.dot(oa_ref[...], wo_ref[0:DIFF_WIDTH, :], preferred_element_type=F32)
             + jnp.dot(ob_ref[...], wo_ref[DIFF_WIDTH:, :], preferred_element_type=F32))
    else:
        x_ref, g_ref, wg_ref, wu_ref, wd_ref, fg_ref, o_ref, h_ref = refs
        x = x_ref[...]
    xn = _rms(x, g_ref[...]).astype(BF16)
    for c in range(wg_ref.shape[1] // chunk):
        sl = slice(c * chunk, (c + 1) * chunk)
        gate = jnp.dot(xn, wg_ref[:, sl].astype(BF16), preferred_element_type=F32)
        up = jnp.dot(xn, wu_ref[:, sl].astype(BF16), preferred_element_type=F32)
        h_ref[:, sl] = (gate * jax.nn.sigmoid(gate) * up).astype(BF16)
    y = x + 0.5 * jnp.dot(h_ref[...], wd_ref[...], preferred_element_type=F32)
    if final:
        y = _rms(y, fg_ref[...])
    o_ref[...] = y


def _layer_spec(arr, layer):
    tail = arr.shape[1:]
    return pl.BlockSpec((None,) + tail, lambda i: (layer,) + (0,) * len(tail), pipeline_mode=pl.Buffered(1))


def _ffn(x, layer, gain, wg, wu, wd, final_gain, *, final, mix=None, tm=512, chunk=256):
    m, d = x.shape
    dff = wg.shape[2]
    row = lambda n: pl.BlockSpec((tm, n), lambda i: (i, 0))
    mix_specs = [] if mix is None else [row(mix[0].shape[1]), row(mix[1].shape[1]), _layer_spec(mix[2], layer)]
    return pl.pallas_call(
        functools.partial(_ffn_kernel, chunk=chunk, final=final, mixed=mix is not None),
        grid=(m // tm,),
        in_specs=[row(d)] + mix_specs + [_layer_spec(gain, layer), _layer_spec(wg, layer), _layer_spec(wu, layer),
                                         _layer_spec(wd, layer), _layer_spec(final_gain, 0)],
        out_specs=row(d),
        out_shape=jax.ShapeDtypeStruct((m, d), F32),
        scratch_shapes=[pltpu.VMEM((tm, dff), BF16)],
        compiler_params=pltpu.CompilerParams(
            dimension_semantics=("arbitrary",), vmem_limit_bytes=VMEM_LIMIT),
        name="ffn_mix" if mix is not None else "ffn",
    )(x, *(mix or ()), gain, wg, wu, wd, final_gain)


def _inproj_kernel(x_ref, g_ref, w_ref, o_ref, *, n_chunks, chunk):
    xn = _rms(x_ref[...], g_ref[...]).astype(BF16)
    for c in range(n_chunks):
        sl = slice(c * chunk, (c + 1) * chunk)
        o_ref[:, sl] = jnp.dot(xn, w_ref[:, sl], preferred_element_type=F32).astype(BF16)


def _inproj(x, layer, gain, w, *, tm=1024, chunk=512):
    m, d = x.shape
    n = w.shape[2]
    return pl.pallas_call(
        functools.partial(_inproj_kernel, n_chunks=n // chunk, chunk=chunk),
        grid=(m // tm,),
        in_specs=[pl.BlockSpec((tm, d), lambda i: (i, 0)), _layer_spec(gain, layer), _layer_spec(w, layer)],
        out_specs=pl.BlockSpec((tm, n), lambda i: (i, 0)),
        out_shape=jax.ShapeDtypeStruct((m, n), BF16),
        compiler_params=pltpu.CompilerParams(
            dimension_semantics=("arbitrary",), vmem_limit_bytes=VMEM_LIMIT),
        name="inproj",
    )(x, gain, w)


def _diff_attn_kernel(q_ref, k_ref, v_ref, bias_ref, lq1_ref, lk1_ref, lq2_ref, lk2_ref, sg_ref,
                      o_ref, vt_ref, qs_ref, acc_ref, m_ref, al_ref, mxa_ref, mxb_ref,
                      sa_ref, sb_ref, ea_ref, eb_ref, *, lam0, n_q):
    tq, tk = ATT_TQ, ATT_TK
    hw = 2 * HEAD_DIM

    ones_row = (lax.broadcasted_iota(jnp.int32, (ATT_PAD, tk), 0) == 0).astype(BF16)
    row = lax.broadcasted_iota(jnp.int32, (hw, tq), 0)
    for c in range(n_q):
        blk = slice(c * tk, (c + 1) * tk)
        vt_ref[c, 0:hw, :] = v_ref[0, blk, :].astype(F32).T.astype(BF16)
        vt_ref[c, hw:, :] = ones_row
        qt = q_ref[0, blk, :].astype(F32).T
        qs_ref[c] = jnp.concatenate(
            [jnp.where(row < HEAD_DIM, qt, 0.0), jnp.where(row >= HEAD_DIM, qt, 0.0)], axis=1).astype(BF16)

    cw = ATT_CW
    chunks = [slice(c * cw, (c + 1) * cw) for c in range(2 * tq // cw)]

    def key_rows(tile, cols):
        qb, j = tile
        return cw if (qb == j and cols.start % tq + cw <= tk // 2) else tk

    def score_chunk(tile, cols, s_ref, mx_ref):
        qb, j = tile
        kr = key_rows(tile, cols)
        kblk = k_ref[0, j * tk:j * tk + kr, :]
        bi = min(qb - j, N_BIAS_TILES - 1)
        t = jnp.dot(kblk, qs_ref[qb, :, cols], preferred_element_type=F32) + bias_ref[0, bi, 0:kr, cols]
        s_ref[0:kr, cols] = t
        mx_ref[:, cols] = jnp.max(t, axis=0, keepdims=True)

    def pv_chunk(tile, cols, e_ref):
        qb, j = tile
        kr = key_rows(tile, cols)
        pv = jnp.dot(vt_ref[j, :, 0:kr], e_ref[0:kr, cols], preferred_element_type=F32)
        acc_ref[qb, :, cols] = pv if j == 0 else al_ref[:, cols] * acc_ref[qb, :, cols] + pv

    def softmax_chunk(tile, cols, s_ref, mx_ref, e_ref):
        kr = key_rows(tile, cols)
        if tile[1] == 0:
            m_new = mx_ref[:, cols]
        else:
            m_old = m_ref[:, cols]
            m_new = jnp.maximum(m_old, mx_ref[:, cols])
            al_ref[:, cols] = jnp.exp2(m_old - m_new)
        m_ref[:, cols] = m_new
        e_ref[0:kr, cols] = jnp.exp2(s_ref[0:kr, cols] - m_new).astype(BF16)

    tiles = [(qb, j) for qb in range(n_q) for j in range(qb + 1)]
    bufs = ((sa_ref, mxa_ref, ea_ref), (sb_ref, mxb_ref, eb_ref))
    for cols in chunks:
        score_chunk(tiles[0], cols, *bufs[0][:2])
    for t, tile in enumerate(tiles):
        cur, oth = bufs[t % 2], bufs[(t + 1) % 2]
        for cols in chunks:
            if t + 1 < len(tiles):
                score_chunk(tiles[t + 1], cols, *oth[:2])
            if t > 0:
                pv_chunk(tiles[t - 1], cols, oth[2])
            softmax_chunk(tile, cols, *cur)
    for cols in chunks:
        pv_chunk(tiles[-1], cols, bufs[(len(tiles) - 1) % 2][2])

    lam = (jnp.exp(jnp.sum(lq1_ref[...] * lk1_ref[...])) - jnp.exp(jnp.sum(lq2_ref[...] * lk2_ref[...])) + lam0)
    for c in range(n_q):
        o = acc_ref[c, 0:hw, :] / acc_ref[c, hw:hw + 1, :]
        o = o[:, :tq] - lam * o[:, tq:]
        y = o * lax.rsqrt(jnp.mean(o * o, axis=0, keepdims=True) + EPS) * sg_ref[...]
        y = y * (1.0 - lam0)
        o_ref[0, c * tq:(c + 1) * tq, :] = y.T.astype(BF16)


def _diff_attn(qkv, bias_tiles, lq1, lk1, lq2, lk2, sub_gain, *, lam0):
    b, s, _ = qkv.shape
    tq, tk = ATT_TQ, ATT_TK
    hw = 2 * HEAD_DIM
    n_q = s // tq
    assert ATT_R == 1 and ATT_CW * 2 == ATT_TK
    small = lambda shape: pl.BlockSpec(shape, lambda h, bb: (0,) * len(shape))
    return pl.pallas_call(
        functools.partial(_diff_attn_kernel, lam0=lam0, n_q=n_q),
        grid=(N_DIFF_HEADS, b),
        in_specs=[
            pl.BlockSpec((1, s, hw), lambda h, bb: (bb, 0, h)),
            pl.BlockSpec((1, s, hw), lambda h, bb: (bb, 0, N_DIFF_HEADS + h)),
            pl.BlockSpec((1, s, hw), lambda h, bb: (bb, 0, 2 * N_DIFF_HEADS + h)),
            pl.BlockSpec((1, N_BIAS_TILES, tk, 2 * tq), lambda h, bb: (h, 0, 0, 0),
                         pipeline_mode=pl.Buffered(1)),
            small((1, HEAD_DIM)), small((1, HEAD_DIM)), small((1, HEAD_DIM)), small((1, HEAD_DIM)),
            small((hw, 1)),
        ],
        out_specs=pl.BlockSpec((1, s, hw), lambda h, bb: (bb, 0, h)),
        out_shape=jax.ShapeDtypeStruct((b, s, DIFF_WIDTH), BF16),
        scratch_shapes=[
            pltpu.VMEM((n_q, hw + ATT_PAD, tk), BF16),
            pltpu.VMEM((n_q, hw, 2 * tq), BF16),
            pltpu.VMEM((n_q, hw + ATT_PAD, 2 * tq), F32),
            pltpu.VMEM((1, 2 * tq), F32),
            pltpu.VMEM((1, 2 * tq), F32),
            pltpu.VMEM((1, 2 * tq), F32),
            pltpu.VMEM((1, 2 * tq), F32),
            pltpu.VMEM((tk, 2 * tq), F32),
            pltpu.VMEM((tk, 2 * tq), F32),
            pltpu.VMEM((tk, 2 * tq), BF16),
            pltpu.VMEM((tk, 2 * tq), BF16),
        ],
        compiler_params=pltpu.CompilerParams(
            dimension_semantics=("arbitrary", "arbitrary"), vmem_limit_bytes=VMEM_LIMIT),
        name="diff_attn",
    )(qkv, qkv, qkv, bias_tiles, lq1, lk1, lq2, lk2, sub_gain)


def _dilated_kernel(q_ref, k_ref, v_ref, bias_ref, o_ref, xf_ref, x4_ref, nat_ref):
    w = DIL_W
    s_len = q_ref.shape[1]
    l4 = s_len // 4
    srcs = (q_ref, k_ref, v_ref)
    cp = 512
    for t in range(3):
        for c in range(s_len // cp):
            xf_ref[t, c * cp:(c + 1) * cp, :] = srcs[t][0, c * cp:(c + 1) * cp, :].astype(F32)
    for t in range(3):
        for r in range(4):
            for c in range(l4 // cp):
                x4_ref[t, r * l4 + c * cp:r * l4 + (c + 1) * cp, :] = (
                    xf_ref[t, pl.ds(r + 4 * c * cp, cp, stride=4), :])

    low = lax.broadcasted_iota(jnp.int32, (w, LANES), 1) < HEAD_DIM
    nt = (((1,), (1,)), ((), ()))

    def rows(t, br, r, j0, n):
        if br == 0:
            return srcs[t][0, pl.ds(pl.multiple_of(j0, w), n), :]
        if br == 1:
            return x4_ref[t, pl.ds(pl.multiple_of(r * l4 + j0, w), n), :].astype(BF16)
        return x4_ref[t, pl.ds((r % 4) * l4 + r // 4 + 4 * j0, n, stride=4), :].astype(BF16)

    def attend(br, r, n):
        k0 = jnp.maximum(n - 1, 0) * w
        kind = jnp.where(n == 0, 1, 0)
        qp = rows(0, br, r, n * w, w)
        kk = rows(1, br, r, k0, 2 * w)
        vv = jnp.concatenate([rows(2, br, r, k0, 2 * w), jnp.ones((2 * w, LANES), BF16)], axis=1)
        zero = jnp.zeros_like(qp)
        qm = jnp.concatenate([jnp.where(low, qp, zero), jnp.where(low, zero, qp)], axis=0)
        s = lax.dot_general(qm, kk, nt, preferred_element_type=F32) + bias_ref[br, 0, kind]
        m = jnp.max(s, axis=1, keepdims=True)
        e = jnp.exp2((s - m).astype(BF16))
        pv = jnp.dot(e, vv, preferred_element_type=F32)
        dil = DILATED_CONFIGS[br][1]
        start = r + dil * w * n
        dst = pl.ds(pl.multiple_of(start, w), w) if dil == 1 else pl.ds(start, w, stride=dil)
        nat_ref[br, 0, dst, :] = jnp.where(low, pv[:w, :LANES], pv[w:, :LANES])
        nat_ref[br, 1, dst, :] = jnp.where(low, m[:w], m[w:])
        nat_ref[br, 2, dst, :] = jnp.where(low, pv[:w, LANES:], pv[w:, LANES:])

    for br, (_, dil) in enumerate(DILATED_CONFIGS):
        nblk = s_len // dil // w

        def blocks(i, carry, br=br, nblk=nblk):
            for u in range(DIL_UNROLL):
                idx = i * DIL_UNROLL + u
                attend(br, idx // nblk, idx % nblk)
            return carry

        lax.fori_loop(0, dil * nblk // DIL_UNROLL, blocks, 0)

    for c in range(s_len // cp):
        sl = slice(c * cp, (c + 1) * cp)
        ms = [nat_ref[br, 1, sl, :] for br in range(3)]
        top = jnp.maximum(jnp.maximum(ms[0], ms[1]), ms[2])
        wts = [jnp.exp2(v - top) for v in ms]
        num = wts[0] * nat_ref[0, 0, sl, :] + wts[1] * nat_ref[1, 0, sl, :] + wts[2] * nat_ref[2, 0, sl, :]
        den = wts[0] * nat_ref[0, 2, sl, :] + wts[1] * nat_ref[1, 2, sl, :] + wts[2] * nat_ref[2, 2, sl, :]
        o_ref[0, sl, :] = (num / den).astype(BF16)


def _dilated_attn(qkv, bias):
    b, s, _ = qkv.shape
    w = DIL_W
    col0 = 3 * DIFF_WIDTH // LANES
    pairs = DIL_WIDTH // LANES
    part = lambda t: pl.BlockSpec((1, s, LANES), lambda bb, p: (bb, 0, col0 + t * pairs + p))
    return pl.pallas_call(
        _dilated_kernel,
        grid=(b, pairs),
        in_specs=[part(0), part(1), part(2),
                  pl.BlockSpec((len(DILATED_CONFIGS), 1, 2, 2 * w, 2 * w), lambda bb, p: (0, p, 0, 0, 0))],
        out_specs=pl.BlockSpec((1, s, LANES), lambda bb, p: (bb, 0, p)),
        out_shape=jax.ShapeDtypeStruct((b, s, DIL_WIDTH), BF16),
        scratch_shapes=[
            pltpu.VMEM((3, s, LANES), F32),
            pltpu.VMEM((3, s, LANES), F32),
            pltpu.VMEM((len(DILATED_CONFIGS), 3, s, LANES), F32),
        ],
        compiler_params=pltpu.CompilerParams(
            dimension_semantics=("arbitrary", "arbitrary"), vmem_limit_bytes=VMEM_LIMIT),
        name="dilated_attn",
    )(qkv, qkv, qkv, bias)


def kernel(x, ffn1_norm, ffn1_w_gate, ffn1_w_up, ffn1_w_down, mix_norm, w_in, lambda_q1, lambda_k1,
           lambda_q2, lambda_k2, subln_gain, w_out, ffn2_norm, ffn2_w_gate, ffn2_w_up, ffn2_w_down,
           rel_bias, final_norm):
    b, s, d = x.shape
    depth = w_in.shape[0]
    m = b * s
    diff_bias = _diff_bias_tiles(rel_bias)
    dil_bias = _dil_bias_tiles(rel_bias)
    qscale = jnp.concatenate([
        jnp.full((DIFF_WIDTH,), HEAD_DIM ** -0.5 * LOG2E, F32), jnp.ones((2 * DIFF_WIDTH,), F32),
        jnp.full((DIL_WIDTH,), HEAD_DIM ** -0.5 * LOG2E, F32), jnp.ones((2 * DIL_WIDTH,), F32)])
    gains = lambda g: g.reshape(depth, 1, d)
    w_in16 = (w_in * qscale).astype(BF16)
    w_out16 = w_out.astype(BF16)
    wd1, wd2 = ffn1_w_down.astype(BF16), ffn2_w_down.astype(BF16)
    fin = final_norm.reshape(1, 1, d)
    h = x.reshape(m, d)
    for layer in range(depth):
        row = lambda v: v[layer].reshape(1, -1)
        h = _ffn(h, layer, gains(ffn1_norm), ffn1_w_gate, ffn1_w_up, wd1, fin, final=False)
        qkv = _inproj(h, layer, gains(mix_norm), w_in16).reshape(b, s, -1)
        oa = _diff_attn(qkv, diff_bias, row(lambda_q1), row(lambda_k1), row(lambda_q2), row(lambda_k2),
                        subln_gain[layer].reshape(-1, 1), lam0=_lambda_init(layer))
        ob = _dilated_attn(qkv, dil_bias)
        h = _ffn(h, layer, gains(ffn2_norm), ffn2_w_gate, ffn2_w_up, wd2, fin, final=layer == depth - 1,
                 mix=(oa.reshape(m, DIFF_WIDTH), ob.reshape(m, DIL_WIDTH), w_out16))
    return h.reshape(b, s, d)
```

```python
import functools
import math

import numpy as np
import jax
import jax.numpy as jnp
from jax import lax
from jax.experimental import pallas as pl
from jax.experimental.pallas import tpu as pltpu

HEAD_DIM = 64
N_DIFF_HEADS = 4
N_DIL_HEADS = 8
DIFF_WIDTH = 512
DIL_WIDTH = 512
DILATED_CONFIGS = ((128, 1), (512, 4), (2048, 16))
NUM_BUCKETS = 32
MAX_DISTANCE = 2048
EPS = 1e-6
NEG = -1e30

LANES = 128
DIL_W = 128
DIL_UNROLL = 16
ATT_TQ = 512
ATT_TK = 512
ATT_R = ATT_TQ // ATT_TK
ATT_PAD = 16
ATT_CW = 256
VMEM_LIMIT = 56 * 1024 * 1024
LOG2E = math.log2(math.e)

F32 = jnp.float32
BF16 = jnp.bfloat16


def _bucket_lower_bounds():
    d = np.arange(0, 2 * MAX_DISTANCE + 1)
    max_exact = NUM_BUCKETS // 2
    v = np.log(np.maximum(d, 1) / max_exact) / math.log(MAX_DISTANCE / max_exact) * (NUM_BUCKETS - max_exact)
    b = np.where(d < max_exact, d, np.minimum(max_exact + v.astype(np.int64), NUM_BUCKETS - 1))
    return [int(np.argmax(b == k)) for k in range(NUM_BUCKETS)]


BUCKET_LO = _bucket_lower_bounds()
N_BIAS_TILES = ATT_R + -(-(BUCKET_LO[-1] + ATT_TK - 1) // ATT_TK)
BIAS_ROLL_W = pl.next_power_of_2(ATT_TQ + ATT_TK)


def _lambda_init(layer):
    return 0.8 - 0.6 * math.exp(-0.3 * layer)


def _bias_of_distance(rb_ref, col, dist):
    val = jnp.full(dist.shape, rb_ref[0, col], F32)
    for b in range(1, NUM_BUCKETS):
        val = jnp.where(dist >= BUCKET_LO[b], rb_ref[b, col], val)
    return val


def _diff_bias_kernel(rb_ref, out_ref):
    h = pl.program_id(0)
    delta = (pl.program_id(1) - (ATT_R - 1)) * ATT_TK
    tq, tk = ATT_TQ, ATT_TK
    u = lax.broadcasted_iota(jnp.int32, (8, BIAS_ROLL_W), 1)
    d = u + (delta - tk)
    for m in range(2):
        val = _bias_of_distance(rb_ref, 2 * h + m, d) * LOG2E
        val = jnp.where(d < 0, NEG, val)
        x = jnp.broadcast_to(val[0:1, :], (tk, BIAS_ROLL_W))
        rolled = pltpu.roll(x, 0, 1, stride=1, stride_axis=0)
        out_ref[0, 0, :, m * tq:(m + 1) * tq] = rolled[:, tk:tk + tq]


def _diff_bias_tiles(rel_bias):
    tq, tk = ATT_TQ, ATT_TK
    return pl.pallas_call(
        _diff_bias_kernel,
        grid=(N_DIFF_HEADS, N_BIAS_TILES),
        in_specs=[pl.BlockSpec(memory_space=pltpu.SMEM)],
        out_specs=pl.BlockSpec((1, 1, tk, 2 * tq), lambda h, d: (h, d, 0, 0)),
        out_shape=jax.ShapeDtypeStruct((N_DIFF_HEADS, N_BIAS_TILES, tk, 2 * tq), F32),
        name="diff_bias_tiles",
    )(rel_bias)


def _dil_bias_kernel(rb_ref, out_ref):
    br = pl.program_id(0)
    w = DIL_W
    dil = jnp.where(br == 0, DILATED_CONFIGS[0][1],
                    jnp.where(br == 1, DILATED_CONFIGS[1][1], DILATED_CONFIGS[2][1]))
    u = lax.broadcasted_iota(jnp.int32, (8, 4 * w), 1)
    for h in range(N_DIL_HEADS):
        for kind in range(2):
            dist = (2 - kind) * w - u
            val = _bias_of_distance(rb_ref, 2 * N_DIFF_HEADS + h, dist * dil) * LOG2E
            val = jnp.where((dist >= 0) & (dist <= w), val, NEG)
            x = jnp.broadcast_to(val[0:1, :], (w, 4 * w))
            rolled = pltpu.roll(x, 0, 1, stride=1, stride_axis=0)
            out_ref[0, h // 2, kind, (h % 2) * w:(h % 2 + 1) * w, :] = rolled[:, w:3 * w]


def _dil_bias_tiles(rel_bias):
    w = DIL_W
    nb = len(DILATED_CONFIGS)
    return pl.pallas_call(
        _dil_bias_kernel,
        grid=(nb,),
        in_specs=[pl.BlockSpec(memory_space=pltpu.SMEM)],
        out_specs=pl.BlockSpec((1, N_DIL_HEADS // 2, 2, 2 * w, 2 * w), lambda b: (b, 0, 0, 0, 0)),
        out_shape=jax.ShapeDtypeStruct((nb, N_DIL_HEADS // 2, 2, 2 * w, 2 * w), F32),
        name="dil_bias_tiles",
    )(rel_bias)


def _rms(x, g):
    return x * lax.rsqrt(jnp.mean(x * x, axis=-1, keepdims=True) + EPS) * g


def _ffn_kernel(*refs, chunk, final, mixed):
    if mixed:
        x_ref, oa_ref, ob_ref, wo_ref, g_ref, wg_ref, wu_ref, wd_ref, fg_ref, o_ref, h_ref = refs
        x = (x_ref[...] + jnp.dot(oa_ref[...], wo_ref[0:DIFF_WIDTH, :], preferred_element_type=F32)
             + jnp.dot(ob_ref[...], wo_ref[DIFF_WIDTH:, :], preferred_element_type=F32))
    else:
        x_ref, g_ref, wg_ref, wu_ref, wd_ref, fg_ref, o_ref, h_ref = refs
        x = x_ref[...]
    xn = _rms(x, g_ref[...]).astype(BF16)
    for c in range(wg_ref.shape[1] // chunk):
        sl = slice(c * chunk, (c + 1) * chunk)
        gate = jnp.dot(xn, wg_ref[:, sl].astype(BF16), preferred_element_type=F32)
        up = jnp.dot(xn, wu_ref[:, sl].astype(BF16), preferred_element_type=F32)
        h_ref[:, sl] = (gate * jax.nn.sigmoid(gate) * up).astype(BF16)
    y = x + 0.5 * jnp.dot(h_ref[...], wd_ref[...], preferred_element_type=F32)
    if final:
        y = _rms(y, fg_ref[...])
    o_ref[...] = y


def _layer_spec(arr, layer):
    tail = arr.shape[1:]
    return pl.BlockSpec((None,) + tail, lambda i: (layer,) + (0,) * len(tail), pipeline_mode=pl.Buffered(1))


def _ffn(x, layer, gain, wg, wu, wd, final_gain, *, final, mix=None, tm=512, chunk=256):
    m, d = x.shape
    dff = wg.shape[2]
    row = lambda n: pl.BlockSpec((tm, n), lambda i: (i, 0))
    mix_specs = [] if mix is None else [row(mix[0].shape[1]), row(mix[1].shape[1]), _layer_spec(mix[2], layer)]
    return pl.pallas_call(
        functools.partial(_ffn_kernel, chunk=chunk, final=final, mixed=mix is not None),
        grid=(m // tm,),
        in_specs=[row(d)] + mix_specs + [_layer_spec(gain, layer), _layer_spec(wg, layer), _layer_spec(wu, layer),
                                         _layer_spec(wd, layer), _layer_spec(final_gain, 0)],
        out_specs=row(d),
        out_shape=jax.ShapeDtypeStruct((m, d), F32),
        scratch_shapes=[pltpu.VMEM((tm, dff), BF16)],
        compiler_params=pltpu.CompilerParams(
            dimension_semantics=("arbitrary",), vmem_limit_bytes=VMEM_LIMIT),
        name="ffn_mix" if mix is not None else "ffn",
    )(x, *(mix or ()), gain, wg, wu, wd, final_gain)


def _inproj_kernel(x_ref, g_ref, w_ref, o_ref, *, n_chunks, chunk):
    xn = _rms(x_ref[...], g_ref[...]).astype(BF16)
    for c in range(n_chunks):
        sl = slice(c * chunk, (c + 1) * chunk)
        o_ref[:, sl] = jnp.dot(xn, w_ref[:, sl], preferred_element_type=F32).astype(BF16)


def _inproj(x, layer, gain, w, *, tm=1024, chunk=512):
    m, d = x.shape
    n = w.shape[2]
    return pl.pallas_call(
        functools.partial(_inproj_kernel, n_chunks=n // chunk, chunk=chunk),
        grid=(m // tm,),
        in_specs=[pl.BlockSpec((tm, d), lambda i: (i, 0)), _layer_spec(gain, layer), _layer_spec(w, layer)],
        out_specs=pl.BlockSpec((tm, n), lambda i: (i, 0)),
        out_shape=jax.ShapeDtypeStruct((m, n), BF16),
        compiler_params=pltpu.CompilerParams(
            dimension_semantics=("arbitrary",), vmem_limit_bytes=VMEM_LIMIT),
        name="inproj",
    )(x, gain, w)


def _diff_attn_kernel(q_ref, k_ref, v_ref, bias_ref, lq1_ref, lk1_ref, lq2_ref, lk2_ref, sg_ref,
                      o_ref, vt_ref, qs_ref, acc_ref, m_ref, al_ref, mxa_ref, mxb_ref,
                      sa_ref, sb_ref, ea_ref, eb_ref, *, lam0, n_q):
    tq, tk = ATT_TQ, ATT_TK
    hw = 2 * HEAD_DIM

    ones_row = (lax.broadcasted_iota(jnp.int32, (ATT_PAD, tk), 0) == 0).astype(BF16)
    row = lax.broadcasted_iota(jnp.int32, (hw, tq), 0)
    for c in range(n_q):
        blk = slice(c * tk, (c + 1) * tk)
        vt_ref[c, 0:hw, :] = v_ref[0, blk, :].astype(F32).T.astype(BF16)
        vt_ref[c, hw:, :] = ones_row
        qt = q_ref[0, blk, :].astype(F32).T
        qs_ref[c] = jnp.concatenate(
            [jnp.where(row < HEAD_DIM, qt, 0.0), jnp.where(row >= HEAD_DIM, qt, 0.0)], axis=1).astype(BF16)

    cw = ATT_CW
    chunks = [slice(c * cw, (c + 1) * cw) for c in range(2 * tq // cw)]

    def key_rows(tile, cols):
        qb, j = tile
        return cw if (qb == j and cols.start % tq + cw <= tk // 2) else tk

    def far_bias(tile, cols):
        qb, j = tile
        return bias_ref[0, N_BIAS_TILES - 1, 0:1, cols] if qb - j >= N_BIAS_TILES - 1 else None

    def score_chunk(tile, cols, s_ref, mx_ref):
        qb, j = tile
        kr = key_rows(tile, cols)
        kblk = k_ref[0, j * tk:j * tk + kr, :]
        t = jnp.dot(kblk, qs_ref[qb, :, cols], preferred_element_type=F32)
        const = far_bias(tile, cols)
        if const is None:
            t = t + bias_ref[0, qb - j, 0:kr, cols]
            mx_ref[:, cols] = jnp.max(t, axis=0, keepdims=True)
        else:
            mx_ref[:, cols] = jnp.max(t, axis=0, keepdims=True) + const
        s_ref[0:kr, cols] = t

    def pv_chunk(tile, cols, e_ref):
        qb, j = tile
        kr = key_rows(tile, cols)
        pv = jnp.dot(vt_ref[j, :, 0:kr], e_ref[0:kr, cols], preferred_element_type=F32)
        acc_ref[qb, :, cols] = pv if j == 0 else al_ref[:, cols] * acc_ref[qb, :, cols] + pv

    def softmax_chunk(tile, cols, s_ref, mx_ref, e_ref):
        kr = key_rows(tile, cols)
        if tile[1] == 0:
            m_new = mx_ref[:, cols]
        else:
            m_old = m_ref[:, cols]
            m_new = jnp.maximum(m_old, mx_ref[:, cols])
            al_ref[:, cols] = jnp.exp2(m_old - m_new)
        m_ref[:, cols] = m_new
        const = far_bias(tile, cols)
        shift = m_new if const is None else m_new - const
        e_ref[0:kr, cols] = jnp.exp2(s_ref[0:kr, cols] - shift).astype(BF16)

    tiles = [(qb, j) for qb in range(n_q) for j in range(qb + 1)]
    bufs = ((sa_ref, mxa_ref, ea_ref), (sb_ref, mxb_ref, eb_ref))
    for cols in chunks:
        score_chunk(tiles[0], cols, *bufs[0][:2])
    for t, tile in enumerate(tiles):
        cur, oth = bufs[t % 2], bufs[(t + 1) % 2]
        for cols in chunks:
            if t + 1 < len(tiles):
                score_chunk(tiles[t + 1], cols, *oth[:2])
            if t > 0:
                pv_chunk(tiles[t - 1], cols, oth[2])
            softmax_chunk(tile, cols, *cur)
    for cols in chunks:
        pv_chunk(tiles[-1], cols, bufs[(len(tiles) - 1) % 2][2])

    lam = (jnp.exp(jnp.sum(lq1_ref[...] * lk1_ref[...])) - jnp.exp(jnp.sum(lq2_ref[...] * lk2_ref[...])) + lam0)
    for c in range(n_q):
        o = acc_ref[c, 0:hw, :] / acc_ref[c, hw:hw + 1, :]
        o = o[:, :tq] - lam * o[:, tq:]
        y = o * lax.rsqrt(jnp.mean(o * o, axis=0, keepdims=True) + EPS) * sg_ref[...]
        y = y * (1.0 - lam0)
        o_ref[0, c * tq:(c + 1) * tq, :] = y.T.astype(BF16)


def _diff_attn(qkv, bias_tiles, lq1, lk1, lq2, lk2, sub_gain, *, lam0):
    b, s, _ = qkv.shape
    tq, tk = ATT_TQ, ATT_TK
    hw = 2 * HEAD_DIM
    n_q = s // tq
    assert ATT_R == 1 and ATT_CW * 2 == ATT_TK
    small = lambda shape: pl.BlockSpec(shape, lambda h, bb: (0,) * len(shape))
    return pl.pallas_call(
        functools.partial(_diff_attn_kernel, lam0=lam0, n_q=n_q),
        grid=(N_DIFF_HEADS, b),
        in_specs=[
            pl.BlockSpec((1, s, hw), lambda h, bb: (bb, 0, h)),
            pl.BlockSpec((1, s, hw), lambda h, bb: (bb, 0, N_DIFF_HEADS + h)),
            pl.BlockSpec((1, s, hw), lambda h, bb: (bb, 0, 2 * N_DIFF_HEADS + h)),
            pl.BlockSpec((1, N_BIAS_TILES, tk, 2 * tq), lambda h, bb: (h, 0, 0, 0),
                         pipeline_mode=pl.Buffered(1)),
            small((1, HEAD_DIM)), small((1, HEAD_DIM)), small((1, HEAD_DIM)), small((1, HEAD_DIM)),
            small((hw, 1)),
        ],
        out_specs=pl.BlockSpec((1, s, hw), lambda h, bb: (bb, 0, h)),
        out_shape=jax.ShapeDtypeStruct((b, s, DIFF_WIDTH), BF16),
        scratch_shapes=[
            pltpu.VMEM((n_q, hw + ATT_PAD, tk), BF16),
            pltpu.VMEM((n_q, hw, 2 * tq), BF16),
            pltpu.VMEM((n_q, hw + ATT_PAD, 2 * tq), F32),
            pltpu.VMEM((1, 2 * tq), F32),
            pltpu.VMEM((1, 2 * tq), F32),
            pltpu.VMEM((1, 2 * tq), F32),
            pltpu.VMEM((1, 2 * tq), F32),
            pltpu.VMEM((tk, 2 * tq), F32),
            pltpu.VMEM((tk, 2 * tq), F32),
            pltpu.VMEM((tk, 2 * tq), BF16),
            pltpu.VMEM((tk, 2 * tq), BF16),
        ],
        compiler_params=pltpu.CompilerParams(
            dimension_semantics=("arbitrary", "arbitrary"), vmem_limit_bytes=VMEM_LIMIT),
        name="diff_attn",
    )(qkv, qkv, qkv, bias_tiles, lq1, lk1, lq2, lk2, sub_gain)


def _dilated_kernel(q_ref, k_ref, v_ref, bias_ref, o_ref, xf_ref, x4_ref, nat_ref):
    w = DIL_W
    s_len = q_ref.shape[1]
    l4 = s_len // 4
    srcs = (q_ref, k_ref, v_ref)
    cp = 512
    for t in range(3):
        for c in range(s_len // cp):
            xf_ref[t, c * cp:(c + 1) * cp, :] = srcs[t][0, c * cp:(c + 1) * cp, :].astype(F32)
    for t in range(3):
        for r in range(4):
            for c in range(l4 // cp):
                x4_ref[t, r * l4 + c * cp:r * l4 + (c + 1) * cp, :] = (
                    xf_ref[t, pl.ds(r + 4 * c * cp, cp, stride=4), :])

    low = lax.broadcasted_iota(jnp.int32, (w, LANES), 1) < HEAD_DIM
    nt = (((1,), (1,)), ((), ()))

    def rows(t, br, r, j0, n):
        if br == 0:
            return srcs[t][0, pl.ds(pl.multiple_of(j0, w), n), :]
        if br == 1:
            return x4_ref[t, pl.ds(pl.multiple_of(r * l4 + j0, w), n), :].astype(BF16)
        return x4_ref[t, pl.ds((r % 4) * l4 + r // 4 + 4 * j0, n, stride=4), :].astype(BF16)

    def attend(br, r, n):
        k0 = jnp.maximum(n - 1, 0) * w
        kind = jnp.where(n == 0, 1, 0)
        qp = rows(0, br, r, n * w, w)
        kk = rows(1, br, r, k0, 2 * w)
        vv = jnp.concatenate([rows(2, br, r, k0, 2 * w), jnp.ones((2 * w, LANES), BF16)], axis=1)
        zero = jnp.zeros_like(qp)
        qm = jnp.concatenate([jnp.where(low, qp, zero), jnp.where(low, zero, qp)], axis=0)
        s = lax.dot_general(qm, kk, nt, preferred_element_type=F32) + bias_ref[br, 0, kind]
        m = jnp.max(s, axis=1, keepdims=True)
        e = jnp.exp2((s - m).astype(BF16))
        pv = jnp.dot(e, vv, preferred_element_type=F32)
        dil = DILATED_CONFIGS[br][1]
        start = r + dil * w * n
        dst = pl.ds(pl.multiple_of(start, w), w) if dil == 1 else pl.ds(start, w, stride=dil)
        nat_ref[br, 0, dst, :] = jnp.where(low, pv[:w, :LANES], pv[w:, :LANES])
        nat_ref[br, 1, dst, :] = jnp.where(low, m[:w], m[w:])
        nat_ref[br, 2, dst, :] = jnp.where(low, pv[:w, LANES:], pv[w:, LANES:])

    for br, (_, dil) in enumerate(DILATED_CONFIGS):
        nblk = s_len // dil // w

        def blocks(i, carry, br=br, nblk=nblk):
            for u in range(DIL_UNROLL):
                idx = i * DIL_UNROLL + u
                attend(br, idx // nblk, idx % nblk)
            return carry

        lax.fori_loop(0, dil * nblk // DIL_UNROLL, blocks, 0)

    for c in range(s_len // cp):
        sl = slice(c * cp, (c + 1) * cp)
        ms = [nat_ref[br, 1, sl, :] for br in range(3)]
        top = jnp.maximum(jnp.maximum(ms[0], ms[1]), ms[2])
        wts = [jnp.exp2(v - top) for v in ms]
        num = wts[0] * nat_ref[0, 0, sl, :] + wts[1] * nat_ref[1, 0, sl, :] + wts[2] * nat_ref[2, 0, sl, :]
        den = wts[0] * nat_ref[0, 2, sl, :] + wts[1] * nat_ref[1, 2, sl, :] + wts[2] * nat_ref[2, 2, sl, :]
        o_ref[0, sl, :] = (num / den).astype(BF16)


def _dilated_attn(qkv, bias):
    b, s, _ = qkv.shape
    w = DIL_W
    col0 = 3 * DIFF_WIDTH // LANES
    pairs = DIL_WIDTH // LANES
    part = lambda t: pl.BlockSpec((1, s, LANES), lambda bb, p: (bb, 0, col0 + t * pairs + p))
    return pl.pallas_call(
        _dilated_kernel,
        grid=(b, pairs),
        in_specs=[part(0), part(1), part(2),
                  pl.BlockSpec((len(DILATED_CONFIGS), 1, 2, 2 * w, 2 * w), lambda bb, p: (0, p, 0, 0, 0))],
        out_specs=pl.BlockSpec((1, s, LANES), lambda bb, p: (bb, 0, p)),
        out_shape=jax.ShapeDtypeStruct((b, s, DIL_WIDTH), BF16),
        scratch_shapes=[
            pltpu.VMEM((3, s, LANES), F32),
            pltpu.VMEM((3, s, LANES), F32),
            pltpu.VMEM((len(DILATED_CONFIGS), 3, s, LANES), F32),
        ],
        compiler_params=pltpu.CompilerParams(
            dimension_semantics=("arbitrary", "arbitrary"), vmem_limit_bytes=VMEM_LIMIT),
        name="dilated_attn",
    )(qkv, qkv, qkv, bias)


def kernel(x, ffn1_norm, ffn1_w_gate, ffn1_w_up, ffn1_w_down, mix_norm, w_in, lambda_q1, lambda_k1,
           lambda_q2, lambda_k2, subln_gain, w_out, ffn2_norm, ffn2_w_gate, ffn2_w_up, ffn2_w_down,
           rel_bias, final_norm):
    b, s, d = x.shape
    depth = w_in.shape[0]
    m = b * s
    diff_bias = _diff_bias_tiles(rel_bias)
    dil_bias = _dil_bias_tiles(rel_bias)
    qscale = jnp.concatenate([
        jnp.full((DIFF_WIDTH,), HEAD_DIM ** -0.5 * LOG2E, F32), jnp.ones((2 * DIFF_WIDTH,), F32),
        jnp.full((DIL_WIDTH,), HEAD_DIM ** -0.5 * LOG2E, F32), jnp.ones((2 * DIL_WIDTH,), F32)])
    gains = lambda g: g.reshape(depth, 1, d)
    w_in16 = (w_in * qscale).astype(BF16)
    w_out16 = w_out.astype(BF16)
    wd1, wd2 = ffn1_w_down.astype(BF16), ffn2_w_down.astype(BF16)
    fin = final_norm.reshape(1, 1, d)
    h = x.reshape(m, d)
    for layer in range(depth):
        row = lambda v: v[layer].reshape(1, -1)
        h = _ffn(h, layer, gains(ffn1_norm), ffn1_w_gate, ffn1_w_up, wd1, fin, final=False)
        qkv = _inproj(h, layer, gains(mix_norm), w_in16).reshape(b, s, -1)
        oa = _diff_attn(qkv, diff_bias, row(lambda_q1), row(lambda_k1), row(lambda_q2), row(lambda_k2),
                        subln_gain[layer].reshape(-1, 1), lam0=_lambda_init(layer))
        ob = _dilated_attn(qkv, dil_bias)
        h = _ffn(h, layer, gains(ffn2_norm), ffn2_w_gate, ffn2_w_up, wd2, fin, final=layer == depth - 1,
                 mix=(oa.reshape(m, DIFF_WIDTH), ob.reshape(m, DIL_WIDTH), w_out16))
    return h.reshape(b, s, d)
```

```python
import functools
import math

import numpy as np
import jax
import jax.numpy as jnp
from jax import lax
from jax.experimental import pallas as pl
from jax.experimental.pallas import tpu as pltpu

HEAD_DIM = 64
N_DIFF_HEADS = 4
N_DIL_HEADS = 8
DIFF_WIDTH = 512
DIL_WIDTH = 512
DILATED_CONFIGS = ((128, 1), (512, 4), (2048, 16))
NUM_BUCKETS = 32
MAX_DISTANCE = 2048
EPS = 1e-6
NEG = -1e30

LANES = 128
DIL_W = 128
DIL_UNROLL = 16
ATT_TQ = 512
ATT_TK = 512
ATT_R = ATT_TQ // ATT_TK
ATT_PAD = 16
ATT_CW = 256
VMEM_LIMIT = 56 * 1024 * 1024
LOG2E = math.log2(math.e)

F32 = jnp.float32
BF16 = jnp.bfloat16


def _bucket_lower_bounds():
    d = np.arange(0, 2 * MAX_DISTANCE + 1)
    max_exact = NUM_BUCKETS // 2
    v = np.log(np.maximum(d, 1) / max_exact) / math.log(MAX_DISTANCE / max_exact) * (NUM_BUCKETS - max_exact)
    b = np.where(d < max_exact, d, np.minimum(max_exact + v.astype(np.int64), NUM_BUCKETS - 1))
    return [int(np.argmax(b == k)) for k in range(NUM_BUCKETS)]


BUCKET_LO = _bucket_lower_bounds()
N_BIAS_TILES = ATT_R + -(-(BUCKET_LO[-1] + ATT_TK - 1) // ATT_TK)
BIAS_ROLL_W = pl.next_power_of_2(ATT_TQ + ATT_TK)


def _lambda_init(layer):
    return 0.8 - 0.6 * math.exp(-0.3 * layer)


def _bias_of_distance(rb_ref, col, dist):
    val = jnp.full(dist.shape, rb_ref[0, col], F32)
    for b in range(1, NUM_BUCKETS):
        val = jnp.where(dist >= BUCKET_LO[b], rb_ref[b, col], val)
    return val


def _diff_bias_kernel(rb_ref, out_ref):
    h = pl.program_id(0)
    delta = (pl.program_id(1) - (ATT_R - 1)) * ATT_TK
    tq, tk = ATT_TQ, ATT_TK
    u = lax.broadcasted_iota(jnp.int32, (8, BIAS_ROLL_W), 1)
    d = u + (delta - tk)
    for m in range(2):
        val = _bias_of_distance(rb_ref, 2 * h + m, d) * LOG2E
        val = jnp.where(d < 0, NEG, val)
        x = jnp.broadcast_to(val[0:1, :], (tk, BIAS_ROLL_W))
        rolled = pltpu.roll(x, 0, 1, stride=1, stride_axis=0)
        out_ref[0, 0, :, m * tq:(m + 1) * tq] = rolled[:, tk:tk + tq]


def _diff_bias_tiles(rel_bias):
    tq, tk = ATT_TQ, ATT_TK
    return pl.pallas_call(
        _diff_bias_kernel,
        grid=(N_DIFF_HEADS, N_BIAS_TILES),
        in_specs=[pl.BlockSpec(memory_space=pltpu.SMEM)],
        out_specs=pl.BlockSpec((1, 1, tk, 2 * tq), lambda h, d: (h, d, 0, 0)),
        out_shape=jax.ShapeDtypeStruct((N_DIFF_HEADS, N_BIAS_TILES, tk, 2 * tq), F32),
        name="diff_bias_tiles",
    )(rel_bias)


def _dil_bias_kernel(rb_ref, out_ref):
    br = pl.program_id(0)
    w = DIL_W
    dil = jnp.where(br == 0, DILATED_CONFIGS[0][1],
                    jnp.where(br == 1, DILATED_CONFIGS[1][1], DILATED_CONFIGS[2][1]))
    u = lax.broadcasted_iota(jnp.int32, (8, 4 * w), 1)
    for h in range(N_DIL_HEADS):
        for kind in range(2):
            dist = (2 - kind) * w - u
            val = _bias_of_distance(rb_ref, 2 * N_DIFF_HEADS + h, dist * dil) * LOG2E
            val = jnp.where((dist >= 0) & (dist <= w), val, NEG)
            x = jnp.broadcast_to(val[0:1, :], (w, 4 * w))
            rolled = pltpu.roll(x, 0, 1, stride=1, stride_axis=0)
            out_ref[0, h // 2, kind, (h % 2) * w:(h % 2 + 1) * w, :] = rolled[:, w:3 * w]


def _dil_bias_tiles(rel_bias):
    w = DIL_W
    nb = len(DILATED_CONFIGS)
    return pl.pallas_call(
        _dil_bias_kernel,
        grid=(nb,),
        in_specs=[pl.BlockSpec(memory_space=pltpu.SMEM)],
        out_specs=pl.BlockSpec((1, N_DIL_HEADS // 2, 2, 2 * w, 2 * w), lambda b: (b, 0, 0, 0, 0)),
        out_shape=jax.ShapeDtypeStruct((nb, N_DIL_HEADS // 2, 2, 2 * w, 2 * w), F32),
        name="dil_bias_tiles",
    )(rel_bias)


def _rms(x, g):
    return x * lax.rsqrt(jnp.mean(x * x, axis=-1, keepdims=True) + EPS) * g


def _ffn_kernel(*refs, chunk, final, mixed):
    if mixed:
        x_ref, oa_ref, ob_ref, wo_ref, g_ref, wg_ref, wu_ref, wd_ref, fg_ref, o_ref, h_ref = refs
        x = (x_ref[...] + jnp.dot(oa_ref[...], wo_ref[0:DIFF_WIDTH, :], preferred_element_type=F32)
             + jnp.dot(ob_ref[...], wo_ref[DIFF_WIDTH:, :], preferred_element_type=F32))
    else:
        x_ref, g_ref, wg_ref, wu_ref, wd_ref, fg_ref, o_ref, h_ref = refs
        x = x_ref[...]
    xn = _rms(x, g_ref[...]).astype(BF16)
    for c in range(wg_ref.shape[1] // chunk):
        sl = slice(c * chunk, (c + 1) * chunk)
        gate = jnp.dot(xn, wg_ref[:, sl].astype(BF16), preferred_element_type=F32)
        up = jnp.dot(xn, wu_ref[:, sl].astype(BF16), preferred_element_type=F32)
        h_ref[:, sl] = (gate * jax.nn.sigmoid(gate) * up).astype(BF16)
    y = x + 0.5 * jnp.dot(h_ref[...], wd_ref[...], preferred_element_type=F32)
    if final:
        y = _rms(y, fg_ref[...])
    o_ref[...] = y


def _layer_spec(arr, layer):
    tail = arr.shape[1:]
    return pl.BlockSpec((None,) + tail, lambda i: (layer,) + (0,) * len(tail), pipeline_mode=pl.Buffered(1))


def _ffn(x, layer, gain, wg, wu, wd, final_gain, *, final, mix=None, tm=512, chunk=256):
    m, d = x.shape
    dff = wg.shape[2]
    row = lambda n: pl.BlockSpec((tm, n), lambda i: (i, 0))
    mix_specs = [] if mix is None else [row(mix[0].shape[1]), row(mix[1].shape[1]), _layer_spec(mix[2], layer)]
    return pl.pallas_call(
        functools.partial(_ffn_kernel, chunk=chunk, final=final, mixed=mix is not None),
        grid=(m // tm,),
        in_specs=[row(d)] + mix_specs + [_layer_spec(gain, layer), _layer_spec(wg, layer), _layer_spec(wu, layer),
                                         _layer_spec(wd, layer), _layer_spec(final_gain, 0)],
        out_specs=row(d),
        out_shape=jax.ShapeDtypeStruct((m, d), F32),
        scratch_shapes=[pltpu.VMEM((tm, dff), BF16)],
        compiler_params=pltpu.CompilerParams(
            dimension_semantics=("arbitrary",), vmem_limit_bytes=VMEM_LIMIT),
        name="ffn_mix" if mix is not None else "ffn",
    )(x, *(mix or ()), gain, wg, wu, wd, final_gain)


def _inproj_kernel(x_ref, g_ref, w_ref, o_ref, *, n_chunks, chunk):
    xn = _rms(x_ref[...], g_ref[...]).astype(BF16)
    for c in range(n_chunks):
        sl = slice(c * chunk, (c + 1) * chunk)
        o_ref[:, sl] = jnp.dot(xn, w_ref[:, sl], preferred_element_type=F32).astype(BF16)


def _inproj(x, layer, gain, w, *, tm=1024, chunk=512):
    m, d = x.shape
    n = w.shape[2]
    return pl.pallas_call(
        functools.partial(_inproj_kernel, n_chunks=n // chunk, chunk=chunk),
        grid=(m // tm,),
        in_specs=[pl.BlockSpec((tm, d), lambda i: (i, 0)), _layer_spec(gain, layer), _layer_spec(w, layer)],
        out_specs=pl.BlockSpec((tm, n), lambda i: (i, 0)),
        out_shape=jax.ShapeDtypeStruct((m, n), BF16),
        compiler_params=pltpu.CompilerParams(
            dimension_semantics=("arbitrary",), vmem_limit_bytes=VMEM_LIMIT),
        name="inproj",
    )(x, gain, w)


def _diff_attn_kernel(q_ref, k_ref, v_ref, bias_ref, lq1_ref, lk1_ref, lq2_ref, lk2_ref, sg_ref,
                      o_ref, vt_ref, qs_ref, acc_ref, m_ref, al_ref, mxa_ref, mxb_ref,
                      sa_ref, sb_ref, ea_ref, eb_ref, *, lam0, n_q):
    tq, tk = ATT_TQ, ATT_TK
    hw = 2 * HEAD_DIM

    ones_row = (lax.broadcasted_iota(jnp.int32, (ATT_PAD, tk), 0) == 0).astype(BF16)
    row = lax.broadcasted_iota(jnp.int32, (hw, tq), 0)
    for c in range(n_q):
        blk = slice(c * tk, (c + 1) * tk)
        vt_ref[c, 0:hw, :] = v_ref[0, blk, :].astype(F32).T.astype(BF16)
        vt_ref[c, hw:, :] = ones_row
        qt = q_ref[0, blk, :].astype(F32).T
        qs_ref[c] = jnp.concatenate(
            [jnp.where(row < HEAD_DIM, qt, 0.0), jnp.where(row >= HEAD_DIM, qt, 0.0)], axis=1).astype(BF16)

    cw = ATT_CW
    chunks = [slice(c * cw, (c + 1) * cw) for c in range(2 * tq // cw)]

    def key_rows(tile, cols):
        qb, j = tile
        return cw if (qb == j and cols.start % tq + cw <= tk // 2) else tk

    def far_bias(tile, cols):
        qb, j = tile
        return bias_ref[0, N_BIAS_TILES - 1, 0:1, cols] if qb - j >= N_BIAS_TILES - 1 else None

    def score_chunk(tile, cols, s_ref, mx_ref):
        qb, j = tile
        kr = key_rows(tile, cols)
        kblk = k_ref[0, j * tk:j * tk + kr, :]
        t = jnp.dot(kblk, qs_ref[qb, :, cols], preferred_element_type=F32)
        const = far_bias(tile, cols)
        if const is None:
            t = t + bias_ref[0, qb - j, 0:kr, cols]
            mx_ref[:, cols] = jnp.max(t, axis=0, keepdims=True)
        else:
            mx_ref[:, cols] = jnp.max(t, axis=0, keepdims=True) + const
        s_ref[0:kr, cols] = t

    def pv_chunk(tile, cols, e_ref):
        qb, j = tile
        kr = key_rows(tile, cols)
        pv = jnp.dot(vt_ref[j, :, 0:kr], e_ref[0:kr, cols], preferred_element_type=F32)
        acc_ref[qb, :, cols] = pv if j == 0 else al_ref[:, cols] * acc_ref[qb, :, cols] + pv

    def softmax_chunk(tile, cols, s_ref, mx_ref, e_ref):
        kr = key_rows(tile, cols)
        if tile[1] == 0:
            m_new = mx_ref[:, cols]
        else:
            m_old = m_ref[:, cols]
            m_new = jnp.maximum(m_old, mx_ref[:, cols])
            al_ref[:, cols] = jnp.exp2(m_old - m_new)
        m_ref[:, cols] = m_new
        const = far_bias(tile, cols)
        shift = m_new if const is None else m_new - const
        e_ref[0:kr, cols] = jnp.exp2(s_ref[0:kr, cols] - shift).astype(BF16)

    tiles = [(qb, j) for qb in range(n_q) for j in range(qb + 1)]
    bufs = ((sa_ref, mxa_ref, ea_ref), (sb_ref, mxb_ref, eb_ref))
    for cols in chunks:
        score_chunk(tiles[0], cols, *bufs[0][:2])
    for t, tile in enumerate(tiles):
        cur, oth = bufs[t % 2], bufs[(t + 1) % 2]
        for cols in chunks:
            if t + 1 < len(tiles):
                score_chunk(tiles[t + 1], cols, *oth[:2])
            if t > 0:
                pv_chunk(tiles[t - 1], cols, oth[2])
            softmax_chunk(tile, cols, *cur)
    for cols in chunks:
        pv_chunk(tiles[-1], cols, bufs[(len(tiles) - 1) % 2][2])

    lam = (jnp.exp(jnp.sum(lq1_ref[...] * lk1_ref[...])) - jnp.exp(jnp.sum(lq2_ref[...] * lk2_ref[...])) + lam0)
    for c in range(n_q):
        o = acc_ref[c, 0:hw, :] / acc_ref[c, hw:hw + 1, :]
        o = o[:, :tq] - lam * o[:, tq:]
        y = o * lax.rsqrt(jnp.mean(o * o, axis=0, keepdims=True) + EPS) * sg_ref[...]
        y = y * (1.0 - lam0)
        o_ref[0, c * tq:(c + 1) * tq, :] = y.T.astype(BF16)


def _diff_attn(qkv, bias_tiles, lq1, lk1, lq2, lk2, sub_gain, *, lam0):
    b, s, _ = qkv.shape
    tq, tk = ATT_TQ, ATT_TK
    hw = 2 * HEAD_DIM
    n_q = s // tq
    assert ATT_R == 1 and ATT_CW * 2 == ATT_TK
    small = lambda shape: pl.BlockSpec(shape, lambda h, bb: (0,) * len(shape))
    return pl.pallas_call(
        functools.partial(_diff_attn_kernel, lam0=lam0, n_q=n_q),
        grid=(N_DIFF_HEADS, b),
        in_specs=[
            pl.BlockSpec((1, s, hw), lambda h, bb: (bb, 0, h)),
            pl.BlockSpec((1, s, hw), lambda h, bb: (bb, 0, N_DIFF_HEADS + h)),
            pl.BlockSpec((1, s, hw), lambda h, bb: (bb, 0, 2 * N_DIFF_HEADS + h)),
            pl.BlockSpec((1, N_BIAS_TILES, tk, 2 * tq), lambda h, bb: (h, 0, 0, 0),
                         pipeline_mode=pl.Buffered(1)),
            small((1, HEAD_DIM)), small((1, HEAD_DIM)), small((1, HEAD_DIM)), small((1, HEAD_DIM)),
            small((hw, 1)),
        ],
        out_specs=pl.BlockSpec((1, s, hw), lambda h, bb: (bb, 0, h)),
        out_shape=jax.ShapeDtypeStruct((b, s, DIFF_WIDTH), BF16),
        scratch_shapes=[
            pltpu.VMEM((n_q, hw + ATT_PAD, tk), BF16),
            pltpu.VMEM((n_q, hw, 2 * tq), BF16),
            pltpu.VMEM((n_q, hw + ATT_PAD, 2 * tq), F32),
            pltpu.VMEM((1, 2 * tq), F32),
            pltpu.VMEM((1, 2 * tq), F32),
            pltpu.VMEM((1, 2 * tq), F32),
            pltpu.VMEM((1, 2 * tq), F32),
            pltpu.VMEM((tk, 2 * tq), F32),
            pltpu.VMEM((tk, 2 * tq), F32),
            pltpu.VMEM((tk, 2 * tq), BF16),
            pltpu.VMEM((tk, 2 * tq), BF16),
        ],
        compiler_params=pltpu.CompilerParams(
            dimension_semantics=("arbitrary", "arbitrary"), vmem_limit_bytes=VMEM_LIMIT),
        name="diff_attn",
    )(qkv, qkv, qkv, bias_tiles, lq1, lk1, lq2, lk2, sub_gain)


def _dilated_kernel(q_ref, k_ref, v_ref, bias_ref, o_ref, xf_ref, x4_ref, nat_ref):
    w = DIL_W
    s_len = q_ref.shape[1]
    l4 = s_len // 4
    srcs = (q_ref, k_ref, v_ref)
    cp = 512
    for t in range(3):
        for c in range(s_len // cp):
            xf_ref[t, c * cp:(c + 1) * cp, :] = srcs[t][0, c * cp:(c + 1) * cp, :].astype(F32)
    for t in range(3):
        for r in range(4):
            for c in range(l4 // cp):
                x4_ref[t, r * l4 + c * cp:r * l4 + (c + 1) * cp, :] = (
                    xf_ref[t, pl.ds(r + 4 * c * cp, cp, stride=4), :])

    low = lax.broadcasted_iota(jnp.int32, (w, LANES), 1) < HEAD_DIM
    nt = (((1,), (1,)), ((), ()))

    def rows(t, br, r, j0, n):
        if br == 0:
            return srcs[t][0, pl.ds(pl.multiple_of(j0, w), n), :]
        if br == 1:
            return x4_ref[t, pl.ds(pl.multiple_of(r * l4 + j0, w), n), :].astype(BF16)
        return x4_ref[t, pl.ds((r % 4) * l4 + r // 4 + 4 * j0, n, stride=4), :].astype(BF16)

    def attend(br, r, n):
        k0 = jnp.maximum(n - 1, 0) * w
        kind = jnp.where(n == 0, 1, 0)
        qp = rows(0, br, r, n * w, w)
        kk = rows(1, br, r, k0, 2 * w)
        vv = jnp.concatenate([rows(2, br, r, k0, 2 * w), jnp.ones((2 * w, LANES), BF16)], axis=1)
        zero = jnp.zeros_like(qp)
        qm = jnp.concatenate([jnp.where(low, qp, zero), jnp.where(low, zero, qp)], axis=0)
        s = lax.dot_general(qm, kk, nt, preferred_element_type=F32) + bias_ref[br, 0, kind]
        m = jnp.max(s, axis=1, keepdims=True)
        e = jnp.exp2((s - m).astype(BF16))
        pv = jnp.dot(e, vv, preferred_element_type=F32)
        dil = DILATED_CONFIGS[br][1]
        start = r + dil * w * n
        dst = pl.ds(pl.multiple_of(start, w), w) if dil == 1 else pl.ds(start, w, stride=dil)
        l = pv[:, LANES:]
        o = pv[:, :LANES] / l
        lw = m + jnp.log2(l)
        nat_ref[br, 0, dst, :] = jnp.where(low, o[:w], o[w:])
        nat_ref[br, 1, dst, :] = jnp.where(low, lw[:w], lw[w:])

    for br, (_, dil) in enumerate(DILATED_CONFIGS):
        nblk = s_len // dil // w

        def blocks(i, carry, br=br, nblk=nblk):
            for u in range(DIL_UNROLL):
                idx = i * DIL_UNROLL + u
                attend(br, idx // nblk, idx % nblk)
            return carry

        lax.fori_loop(0, dil * nblk // DIL_UNROLL, blocks, 0)

    for c in range(s_len // cp):
        sl = slice(c * cp, (c + 1) * cp)
        lws = [nat_ref[br, 1, sl, :] for br in range(3)]
        top = jnp.maximum(jnp.maximum(lws[0], lws[1]), lws[2])
        wts = [jnp.exp2(v - top) for v in lws]
        num = wts[0] * nat_ref[0, 0, sl, :] + wts[1] * nat_ref[1, 0, sl, :] + wts[2] * nat_ref[2, 0, sl, :]
        o_ref[0, sl, :] = (num / (wts[0] + wts[1] + wts[2])).astype(BF16)


def _dilated_attn(qkv, bias):
    b, s, _ = qkv.shape
    w = DIL_W
    col0 = 3 * DIFF_WIDTH // LANES
    pairs = DIL_WIDTH // LANES
    part = lambda t: pl.BlockSpec((1, s, LANES), lambda bb, p: (bb, 0, col0 + t * pairs + p))
    return pl.pallas_call(
        _dilated_kernel,
        grid=(b, pairs),
        in_specs=[part(0), part(1), part(2),
                  pl.BlockSpec((len(DILATED_CONFIGS), 1, 2, 2 * w, 2 * w), lambda bb, p: (0, p, 0, 0, 0))],
        out_specs=pl.BlockSpec((1, s, LANES), lambda bb, p: (bb, 0, p)),
        out_shape=jax.ShapeDtypeStruct((b, s, DIL_WIDTH), BF16),
        scratch_shapes=[
            pltpu.VMEM((3, s, LANES), F32),
            pltpu.VMEM((3, s, LANES), F32),
            pltpu.VMEM((len(DILATED_CONFIGS), 2, s, LANES), F32),
        ],
        compiler_params=pltpu.CompilerParams(
            dimension_semantics=("arbitrary", "arbitrary"), vmem_limit_bytes=VMEM_LIMIT),
        name="dilated_attn",
    )(qkv, qkv, qkv, bias)


def kernel(x, ffn1_norm, ffn1_w_gate, ffn1_w_up, ffn1_w_down, mix_norm, w_in, lambda_q1, lambda_k1,
           lambda_q2, lambda_k2, subln_gain, w_out, ffn2_norm, ffn2_w_gate, ffn2_w_up, ffn2_w_down,
           rel_bias, final_norm):
    b, s, d = x.shape
    depth = w_in.shape[0]
    m = b * s
    diff_bias = _diff_bias_tiles(rel_bias)
    dil_bias = _dil_bias_tiles(rel_bias)
    qscale = jnp.concatenate([
        jnp.full((DIFF_WIDTH,), HEAD_DIM ** -0.5 * LOG2E, F32), jnp.ones((2 * DIFF_WIDTH,), F32),
        jnp.full((DIL_WIDTH,), HEAD_DIM ** -0.5 * LOG2E, F32), jnp.ones((2 * DIL_WIDTH,), F32)])
    gains = lambda g: g.reshape(depth, 1, d)
    w_in16 = (w_in * qscale).astype(BF16)
    w_out16 = w_out.astype(BF16)
    wd1, wd2 = ffn1_w_down.astype(BF16), ffn2_w_down.astype(BF16)
    fin = final_norm.reshape(1, 1, d)
    h = x.reshape(m, d)
    for layer in range(depth):
        row = lambda v: v[layer].reshape(1, -1)
        h = _ffn(h, layer, gains(ffn1_norm), ffn1_w_gate, ffn1_w_up, wd1, fin, final=False)
        qkv = _inproj(h, layer, gains(mix_norm), w_in16).reshape(b, s, -1)
        oa = _diff_attn(qkv, diff_bias, row(lambda_q1), row(lambda_k1), row(lambda_q2), row(lambda_k2),
                        subln_gain[layer].reshape(-1, 1), lam0=_lambda_init(layer))
        ob = _dilated_attn(qkv, dil_bias)
        h = _ffn(h, layer, gains(ffn2_norm), ffn2_w_gate, ffn2_w_up, wd2, fin, final=layer == depth - 1,
                 mix=(oa.reshape(m, DIFF_WIDTH), ob.reshape(m, DIL_WIDTH), w_out16))
    return h.reshape(b, s, d)
```

```python
import functools
import math

import numpy as np
import jax
import jax.numpy as jnp
from jax import lax
from jax.experimental import pallas as pl
from jax.experimental.pallas import tpu as pltpu

HEAD_DIM = 64
N_DIFF_HEADS = 4
N_DIL_HEADS = 8
DIFF_WIDTH = 512
DIL_WIDTH = 512
DILATED_CONFIGS = ((128, 1), (512, 4), (2048, 16))
NUM_BUCKETS = 32
MAX_DISTANCE = 2048
EPS = 1e-6
NEG = -1e30

LANES = 128
DIL_W = 128
DIL_UNROLL = 16
ATT_TQ = 512
ATT_TK = 512
ATT_R = ATT_TQ // ATT_TK
ATT_PAD = 16
ATT_CW = 256
VMEM_LIMIT = 56 * 1024 * 1024
LOG2E = math.log2(math.e)

F32 = jnp.float32
BF16 = jnp.bfloat16


def _bucket_lower_bounds():
    d = np.arange(0, 2 * MAX_DISTANCE + 1)
    max_exact = NUM_BUCKETS // 2
    v = np.log(np.maximum(d, 1) / max_exact) / math.log(MAX_DISTANCE / max_exact) * (NUM_BUCKETS - max_exact)
    b = np.where(d < max_exact, d, np.minimum(max_exact + v.astype(np.int64), NUM_BUCKETS - 1))
    return [int(np.argmax(b == k)) for k in range(NUM_BUCKETS)]


BUCKET_LO = _bucket_lower_bounds()
N_BIAS_TILES = ATT_R + -(-(BUCKET_LO[-1] + ATT_TK - 1) // ATT_TK)
BIAS_ROLL_W = pl.next_power_of_2(ATT_TQ + ATT_TK)


def _lambda_init(layer):
    return 0.8 - 0.6 * math.exp(-0.3 * layer)


def _bias_of_distance(rb_ref, col, dist):
    val = jnp.full(dist.shape, rb_ref[0, col], F32)
    for b in range(1, NUM_BUCKETS):
        val = jnp.where(dist >= BUCKET_LO[b], rb_ref[b, col], val)
    return val


def _diff_bias_kernel(rb_ref, out_ref):
    h = pl.program_id(0)
    delta = (pl.program_id(1) - (ATT_R - 1)) * ATT_TK
    tq, tk = ATT_TQ, ATT_TK
    u = lax.broadcasted_iota(jnp.int32, (8, BIAS_ROLL_W), 1)
    d = u + (delta - tk)
    for m in range(2):
        val = _bias_of_distance(rb_ref, 2 * h + m, d) * LOG2E
        val = jnp.where(d < 0, NEG, val)
        x = jnp.broadcast_to(val[0:1, :], (tk, BIAS_ROLL_W))
        rolled = pltpu.roll(x, 0, 1, stride=1, stride_axis=0)
        out_ref[0, 0, :, m * tq:(m + 1) * tq] = rolled[:, tk:tk + tq]


def _diff_bias_tiles(rel_bias):
    tq, tk = ATT_TQ, ATT_TK
    return pl.pallas_call(
        _diff_bias_kernel,
        grid=(N_DIFF_HEADS, N_BIAS_TILES),
        in_specs=[pl.BlockSpec(memory_space=pltpu.SMEM)],
        out_specs=pl.BlockSpec((1, 1, tk, 2 * tq), lambda h, d: (h, d, 0, 0)),
        out_shape=jax.ShapeDtypeStruct((N_DIFF_HEADS, N_BIAS_TILES, tk, 2 * tq), F32),
        name="diff_bias_tiles",
    )(rel_bias)


def _dil_bias_kernel(rb_ref, out_ref):
    br = pl.program_id(0)
    w = DIL_W
    dil = jnp.where(br == 0, DILATED_CONFIGS[0][1],
                    jnp.where(br == 1, DILATED_CONFIGS[1][1], DILATED_CONFIGS[2][1]))
    u = lax.broadcasted_iota(jnp.int32, (8, 4 * w), 1)
    for h in range(N_DIL_HEADS):
        for kind in range(2):
            dist = (2 - kind) * w - u
            val = _bias_of_distance(rb_ref, 2 * N_DIFF_HEADS + h, dist * dil) * LOG2E
            val = jnp.where((dist >= 0) & (dist <= w), val, NEG)
            x = jnp.broadcast_to(val[0:1, :], (w, 4 * w))
            rolled = pltpu.roll(x, 0, 1, stride=1, stride_axis=0)
            out_ref[0, h // 2, kind, (h % 2) * w:(h % 2 + 1) * w, :] = rolled[:, w:3 * w]


def _dil_bias_tiles(rel_bias):
    w = DIL_W
    nb = len(DILATED_CONFIGS)
    return pl.pallas_call(
        _dil_bias_kernel,
        grid=(nb,),
        in_specs=[pl.BlockSpec(memory_space=pltpu.SMEM)],
        out_specs=pl.BlockSpec((1, N_DIL_HEADS // 2, 2, 2 * w, 2 * w), lambda b: (b, 0, 0, 0, 0)),
        out_shape=jax.ShapeDtypeStruct((nb, N_DIL_HEADS // 2, 2, 2 * w, 2 * w), F32),
        name="dil_bias_tiles",
    )(rel_bias)


def _rms(x, g):
    return x * lax.rsqrt(jnp.mean(x * x, axis=-1, keepdims=True) + EPS) * g


def _ffn_kernel(*refs, chunk, final, mixed):
    if mixed:
        x_ref, oa_ref, ob_ref, wo_ref, g_ref, wg_ref, wu_ref, wd_ref, fg_ref, o_ref, h_ref = refs
        x = (x_ref[...] + jnp.dot(oa_ref[...], wo_ref[0:DIFF_WIDTH, :], preferred_element_type=F32)
             + jnp.dot(ob_ref[...], wo_ref[DIFF_WIDTH:, :], preferred_element_type=F32))
    else:
        x_ref, g_ref, wg_ref, wu_ref, wd_ref, fg_ref, o_ref, h_ref = refs
        x = x_ref[...]
    xn = _rms(x, g_ref[...]).astype(BF16)
    for c in range(wg_ref.shape[1] // chunk):
        sl = slice(c * chunk, (c + 1) * chunk)
        gate = jnp.dot(xn, wg_ref[:, sl].astype(BF16), preferred_element_type=F32)
        up = jnp.dot(xn, wu_ref[:, sl].astype(BF16), preferred_element_type=F32)
        h_ref[:, sl] = (gate * jax.nn.sigmoid(gate) * up).astype(BF16)
    y = x + 0.5 * jnp.dot(h_ref[...], wd_ref[...], preferred_element_type=F32)
    if final:
        y = _rms(y, fg_ref[...])
    o_ref[...] = y


def _layer_spec(arr, layer):
    tail = arr.shape[1:]
    return pl.BlockSpec((None,) + tail, lambda i: (layer,) + (0,) * len(tail), pipeline_mode=pl.Buffered(1))


def _ffn(x, layer, gain, wg, wu, wd, final_gain, *, final, mix=None, tm=512, chunk=256):
    m, d = x.shape
    dff = wg.shape[2]
    row = lambda n: pl.BlockSpec((tm, n), lambda i: (i, 0))
    mix_specs = [] if mix is None else [row(mix[0].shape[1]), row(mix[1].shape[1]), _layer_spec(mix[2], layer)]
    return pl.pallas_call(
        functools.partial(_ffn_kernel, chunk=chunk, final=final, mixed=mix is not None),
        grid=(m // tm,),
        in_specs=[row(d)] + mix_specs + [_layer_spec(gain, layer), _layer_spec(wg, layer), _layer_spec(wu, layer),
                                         _layer_spec(wd, layer), _layer_spec(final_gain, 0)],
        out_specs=row(d),
        out_shape=jax.ShapeDtypeStruct((m, d), F32),
        scratch_shapes=[pltpu.VMEM((tm, dff), BF16)],
        compiler_params=pltpu.CompilerParams(
            dimension_semantics=("arbitrary",), vmem_limit_bytes=VMEM_LIMIT),
        name="ffn_mix" if mix is not None else "ffn",
    )(x, *(mix or ()), gain, wg, wu, wd, final_gain)


def _inproj_kernel(x_ref, g_ref, w_ref, o_ref, *, n_chunks, chunk):
    xn = _rms(x_ref[...], g_ref[...]).astype(BF16)
    for c in range(n_chunks):
        sl = slice(c * chunk, (c + 1) * chunk)
        o_ref[:, sl] = jnp.dot(xn, w_ref[:, sl], preferred_element_type=F32).astype(BF16)


def _inproj(x, layer, gain, w, *, tm=1024, chunk=512):
    m, d = x.shape
    n = w.shape[2]
    return pl.pallas_call(
        functools.partial(_inproj_kernel, n_chunks=n // chunk, chunk=chunk),
        grid=(m // tm,),
        in_specs=[pl.BlockSpec((tm, d), lambda i: (i, 0)), _layer_spec(gain, layer), _layer_spec(w, layer)],
        out_specs=pl.BlockSpec((tm, n), lambda i: (i, 0)),
        out_shape=jax.ShapeDtypeStruct((m, n), BF16),
        compiler_params=pltpu.CompilerParams(
            dimension_semantics=("arbitrary",), vmem_limit_bytes=VMEM_LIMIT),
        name="inproj",
    )(x, gain, w)


def _diff_attn_kernel(q_ref, k_ref, v_ref, bias_ref, lq1_ref, lk1_ref, lq2_ref, lk2_ref, sg_ref,
                      o_ref, vt_ref, qs_ref, acc_ref, m_ref, al_ref, mxa_ref, mxb_ref,
                      sa_ref, sb_ref, ea_ref, eb_ref, *, lam0, n_q):
    tq, tk = ATT_TQ, ATT_TK
    hw = 2 * HEAD_DIM

    ones_row = (lax.broadcasted_iota(jnp.int32, (ATT_PAD, tk), 0) == 0).astype(BF16)
    low = lax.broadcasted_iota(jnp.int32, (tq, hw), 1) < HEAD_DIM
    for c in range(n_q):
        blk = slice(c * tk, (c + 1) * tk)
        vt_ref[c, 0:hw, :] = v_ref[0, blk, :].astype(F32).T.astype(BF16)
        vt_ref[c, hw:, :] = ones_row
        qblk = q_ref[0, blk, :]
        qs_ref[c, 0:tq, :] = jnp.where(low, qblk, jnp.zeros_like(qblk))
        qs_ref[c, tq:, :] = jnp.where(low, jnp.zeros_like(qblk), qblk)

    cw = ATT_CW
    chunks = [slice(c * cw, (c + 1) * cw) for c in range(2 * tq // cw)]

    def key_rows(tile, cols):
        qb, j = tile
        return cw if (qb == j and cols.start % tq + cw <= tk // 2) else tk

    def far_bias(tile, cols):
        qb, j = tile
        return bias_ref[0, N_BIAS_TILES - 1, 0:1, cols] if qb - j >= N_BIAS_TILES - 1 else None

    def score_chunk(tile, cols, s_ref, mx_ref):
        qb, j = tile
        kr = key_rows(tile, cols)
        kblk = k_ref[0, j * tk:j * tk + kr, :]
        t = lax.dot_general(kblk, qs_ref[qb, cols, :], (((1,), (1,)), ((), ())), preferred_element_type=F32)
        const = far_bias(tile, cols)
        if const is None:
            t = t + bias_ref[0, qb - j, 0:kr, cols]
            mx_ref[:, cols] = jnp.max(t, axis=0, keepdims=True)
        else:
            mx_ref[:, cols] = jnp.max(t, axis=0, keepdims=True) + const
        s_ref[0:kr, cols] = t

    def pv_chunk(tile, cols, e_ref):
        qb, j = tile
        kr = key_rows(tile, cols)
        pv = jnp.dot(vt_ref[j, :, 0:kr], e_ref[0:kr, cols], preferred_element_type=F32)
        acc_ref[qb, :, cols] = pv if j == 0 else al_ref[:, cols] * acc_ref[qb, :, cols] + pv

    def softmax_chunk(tile, cols, s_ref, mx_ref, e_ref):
        kr = key_rows(tile, cols)
        if tile[1] == 0:
            m_new = mx_ref[:, cols]
        else:
            m_old = m_ref[:, cols]
            m_new = jnp.maximum(m_old, mx_ref[:, cols])
            al_ref[:, cols] = jnp.exp2(m_old - m_new)
        m_ref[:, cols] = m_new
        const = far_bias(tile, cols)
        shift = m_new if const is None else m_new - const
        e_ref[0:kr, cols] = jnp.exp2(s_ref[0:kr, cols] - shift).astype(BF16)

    tiles = [(qb, j) for qb in range(n_q) for j in range(qb + 1)]
    bufs = ((sa_ref, mxa_ref, ea_ref), (sb_ref, mxb_ref, eb_ref))
    for cols in chunks:
        score_chunk(tiles[0], cols, *bufs[0][:2])
    for t, tile in enumerate(tiles):
        cur, oth = bufs[t % 2], bufs[(t + 1) % 2]
        for cols in chunks:
            if t + 1 < len(tiles):
                score_chunk(tiles[t + 1], cols, *oth[:2])
            if t > 0:
                pv_chunk(tiles[t - 1], cols, oth[2])
            softmax_chunk(tile, cols, *cur)
    for cols in chunks:
        pv_chunk(tiles[-1], cols, bufs[(len(tiles) - 1) % 2][2])

    lam = (jnp.exp(jnp.sum(lq1_ref[...] * lk1_ref[...])) - jnp.exp(jnp.sum(lq2_ref[...] * lk2_ref[...])) + lam0)
    for c in range(n_q):
        o = acc_ref[c, 0:hw, :] / acc_ref[c, hw:hw + 1, :]
        o = o[:, :tq] - lam * o[:, tq:]
        y = o * lax.rsqrt(jnp.mean(o * o, axis=0, keepdims=True) + EPS) * sg_ref[...]
        y = y * (1.0 - lam0)
        o_ref[0, c * tq:(c + 1) * tq, :] = y.T.astype(BF16)


def _diff_attn(qkv, bias_tiles, lq1, lk1, lq2, lk2, sub_gain, *, lam0):
    b, s, _ = qkv.shape
    tq, tk = ATT_TQ, ATT_TK
    hw = 2 * HEAD_DIM
    n_q = s // tq
    assert ATT_R == 1 and ATT_CW * 2 == ATT_TK
    small = lambda shape: pl.BlockSpec(shape, lambda h, bb: (0,) * len(shape))
    return pl.pallas_call(
        functools.partial(_diff_attn_kernel, lam0=lam0, n_q=n_q),
        grid=(N_DIFF_HEADS, b),
        in_specs=[
            pl.BlockSpec((1, s, hw), lambda h, bb: (bb, 0, h)),
            pl.BlockSpec((1, s, hw), lambda h, bb: (bb, 0, N_DIFF_HEADS + h)),
            pl.BlockSpec((1, s, hw), lambda h, bb: (bb, 0, 2 * N_DIFF_HEADS + h)),
            pl.BlockSpec((1, N_BIAS_TILES, tk, 2 * tq), lambda h, bb: (h, 0, 0, 0),
                         pipeline_mode=pl.Buffered(1)),
            small((1, HEAD_DIM)), small((1, HEAD_DIM)), small((1, HEAD_DIM)), small((1, HEAD_DIM)),
            small((hw, 1)),
        ],
        out_specs=pl.BlockSpec((1, s, hw), lambda h, bb: (bb, 0, h)),
        out_shape=jax.ShapeDtypeStruct((b, s, DIFF_WIDTH), BF16),
        scratch_shapes=[
            pltpu.VMEM((n_q, hw + ATT_PAD, tk), BF16),
            pltpu.VMEM((n_q, 2 * tq, hw), BF16),
            pltpu.VMEM((n_q, hw + ATT_PAD, 2 * tq), F32),
            pltpu.VMEM((1, 2 * tq), F32),
            pltpu.VMEM((1, 2 * tq), F32),
            pltpu.VMEM((1, 2 * tq), F32),
            pltpu.VMEM((1, 2 * tq), F32),
            pltpu.VMEM((tk, 2 * tq), F32),
            pltpu.VMEM((tk, 2 * tq), F32),
            pltpu.VMEM((tk, 2 * tq), BF16),
            pltpu.VMEM((tk, 2 * tq), BF16),
        ],
        compiler_params=pltpu.CompilerParams(
            dimension_semantics=("arbitrary", "arbitrary"), vmem_limit_bytes=VMEM_LIMIT),
        name="diff_attn",
    )(qkv, qkv, qkv, bias_tiles, lq1, lk1, lq2, lk2, sub_gain)


def _dilated_kernel(q_ref, k_ref, v_ref, bias_ref, o_ref, xf_ref, x4_ref, nat_ref):
    w = DIL_W
    s_len = q_ref.shape[1]
    l4 = s_len // 4
    srcs = (q_ref, k_ref, v_ref)
    cp = 512
    for t in range(3):
        for c in range(s_len // cp):
            xf_ref[t, c * cp:(c + 1) * cp, :] = srcs[t][0, c * cp:(c + 1) * cp, :].astype(F32)
    for t in range(3):
        for r in range(4):
            for c in range(l4 // cp):
                x4_ref[t, r * l4 + c * cp:r * l4 + (c + 1) * cp, :] = (
                    xf_ref[t, pl.ds(r + 4 * c * cp, cp, stride=4), :])

    low = lax.broadcasted_iota(jnp.int32, (w, LANES), 1) < HEAD_DIM
    nt = (((1,), (1,)), ((), ()))

    def rows(t, br, r, j0, n):
        if br == 0:
            return srcs[t][0, pl.ds(pl.multiple_of(j0, w), n), :]
        if br == 1:
            return x4_ref[t, pl.ds(pl.multiple_of(r * l4 + j0, w), n), :].astype(BF16)
        return x4_ref[t, pl.ds((r % 4) * l4 + r // 4 + 4 * j0, n, stride=4), :].astype(BF16)

    def attend(br, r, n):
        k0 = jnp.maximum(n - 1, 0) * w
        kind = jnp.where(n == 0, 1, 0)
        qp = rows(0, br, r, n * w, w)
        kk = rows(1, br, r, k0, 2 * w)
        vv = jnp.concatenate([rows(2, br, r, k0, 2 * w), jnp.ones((2 * w, LANES), BF16)], axis=1)
        zero = jnp.zeros_like(qp)
        qm = jnp.concatenate([jnp.where(low, qp, zero), jnp.where(low, zero, qp)], axis=0)
        s = lax.dot_general(qm, kk, nt, preferred_element_type=F32) + bias_ref[br, 0, kind]
        m = jnp.max(s, axis=1, keepdims=True)
        e = jnp.exp2((s - m).astype(BF16))
        pv = jnp.dot(e, vv, preferred_element_type=F32)
        dil = DILATED_CONFIGS[br][1]
        start = r + dil * w * n
        dst = pl.ds(pl.multiple_of(start, w), w) if dil == 1 else pl.ds(start, w, stride=dil)
        nat_ref[br, 0, dst, :] = jnp.where(low, pv[:w, :LANES], pv[w:, :LANES])
        nat_ref[br, 1, dst, :] = jnp.where(low, m[:w], m[w:])
        nat_ref[br, 2, dst, :] = jnp.where(low, pv[:w, LANES:], pv[w:, LANES:])

    for br, (_, dil) in enumerate(DILATED_CONFIGS):
        nblk = s_len // dil // w

        def blocks(i, carry, br=br, nblk=nblk):
            for u in range(DIL_UNROLL):
                idx = i * DIL_UNROLL + u
                attend(br, idx // nblk, idx % nblk)
            return carry

        lax.fori_loop(0, dil * nblk // DIL_UNROLL, blocks, 0)

    for c in range(s_len // cp):
        sl = slice(c * cp, (c + 1) * cp)
        ms = [nat_ref[br, 1, sl, :] for br in range(3)]
        top = jnp.maximum(jnp.maximum(ms[0], ms[1]), ms[2])
        wts = [jnp.exp2(v - top) for v in ms]
        num = wts[0] * nat_ref[0, 0, sl, :] + wts[1] * nat_ref[1, 0, sl, :] + wts[2] * nat_ref[2, 0, sl, :]
        den = wts[0] * nat_ref[0, 2, sl, :] + wts[1] * nat_ref[1, 2, sl, :] + wts[2] * nat_ref[2, 2, sl, :]
        o_ref[0, sl, :] = (num / den).astype(BF16)


def _dilated_attn(qkv, bias):
    b, s, _ = qkv.shape
    w = DIL_W
    col0 = 3 * DIFF_WIDTH // LANES
    pairs = DIL_WIDTH // LANES
    part = lambda t: pl.BlockSpec((1, s, LANES), lambda bb, p: (bb, 0, col0 + t * pairs + p))
    return pl.pallas_call(
        _dilated_kernel,
        grid=(b, pairs),
        in_specs=[part(0), part(1), part(2),
                  pl.BlockSpec((len(DILATED_CONFIGS), 1, 2, 2 * w, 2 * w), lambda bb, p: (0, p, 0, 0, 0))],
        out_specs=pl.BlockSpec((1, s, LANES), lambda bb, p: (bb, 0, p)),
        out_shape=jax.ShapeDtypeStruct((b, s, DIL_WIDTH), BF16),
        scratch_shapes=[
            pltpu.VMEM((3, s, LANES), F32),
            pltpu.VMEM((3, s, LANES), F32),
            pltpu.VMEM((len(DILATED_CONFIGS), 3, s, LANES), F32),
        ],
        compiler_params=pltpu.CompilerParams(
            dimension_semantics=("arbitrary", "arbitrary"), vmem_limit_bytes=VMEM_LIMIT),
        name="dilated_attn",
    )(qkv, qkv, qkv, bias)


def kernel(x, ffn1_norm, ffn1_w_gate, ffn1_w_up, ffn1_w_down, mix_norm, w_in, lambda_q1, lambda_k1,
           lambda_q2, lambda_k2, subln_gain, w_out, ffn2_norm, ffn2_w_gate, ffn2_w_up, ffn2_w_down,
           rel_bias, final_norm):
    b, s, d = x.shape
    depth = w_in.shape[0]
    m = b * s
    diff_bias = _diff_bias_tiles(rel_bias)
    dil_bias = _dil_bias_tiles(rel_bias)
    qscale = jnp.concatenate([
        jnp.full((DIFF_WIDTH,), HEAD_DIM ** -0.5 * LOG2E, F32), jnp.ones((2 * DIFF_WIDTH,), F32),
        jnp.full((DIL_WIDTH,), HEAD_DIM ** -0.5 * LOG2E, F32), jnp.ones((2 * DIL_WIDTH,), F32)])
    gains = lambda g: g.reshape(depth, 1, d)
    w_in16 = (w_in * qscale).astype(BF16)
    w_out16 = w_out.astype(BF16)
    wd1, wd2 = ffn1_w_down.astype(BF16), ffn2_w_down.astype(BF16)
    fin = final_norm.reshape(1, 1, d)
    h = x.reshape(m, d)
    for layer in range(depth):
        row = lambda v: v[layer].reshape(1, -1)
        h = _ffn(h, layer, gains(ffn1_norm), ffn1_w_gate, ffn1_w_up, wd1, fin, final=False)
        qkv = _inproj(h, layer, gains(mix_norm), w_in16).reshape(b, s, -1)
        oa = _diff_attn(qkv, diff_bias, row(lambda_q1), row(lambda_k1), row(lambda_q2), row(lambda_k2),
                        subln_gain[layer].reshape(-1, 1), lam0=_lambda_init(layer))
        ob = _dilated_attn(qkv, dil_bias)
        h = _ffn(h, layer, gains(ffn2_norm), ffn2_w_gate, ffn2_w_up, wd2, fin, final=layer == depth - 1,
                 mix=(oa.reshape(m, DIFF_WIDTH), ob.reshape(m, DIL_WIDTH), w_out16))
    return h.reshape(b, s, d)
```

```python
import functools
import math

import numpy as np
import jax
import jax.numpy as jnp
from jax import lax
from jax.experimental import pallas as pl
from jax.experimental.pallas import tpu as pltpu

HEAD_DIM = 64
N_DIFF_HEADS = 4
N_DIL_HEADS = 8
DIFF_WIDTH = 512
DIL_WIDTH = 512
DILATED_CONFIGS = ((128, 1), (512, 4), (2048, 16))
NUM_BUCKETS = 32
MAX_DISTANCE = 2048
EPS = 1e-6
NEG = -1e30

LANES = 128
DIL_W = 128
DIL_UNROLL = 16
ATT_TQ = 512
ATT_TK = 512
ATT_R = ATT_TQ // ATT_TK
ATT_PAD = 16
ATT_CW = 256
VMEM_LIMIT = 56 * 1024 * 1024
LOG2E = math.log2(math.e)

F32 = jnp.float32
BF16 = jnp.bfloat16


def _bucket_lower_bounds():
    d = np.arange(0, 2 * MAX_DISTANCE + 1)
    max_exact = NUM_BUCKETS // 2
    v = np.log(np.maximum(d, 1) / max_exact) / math.log(MAX_DISTANCE / max_exact) * (NUM_BUCKETS - max_exact)
    b = np.where(d < max_exact, d, np.minimum(max_exact + v.astype(np.int64), NUM_BUCKETS - 1))
    return [int(np.argmax(b == k)) for k in range(NUM_BUCKETS)]


BUCKET_LO = _bucket_lower_bounds()
N_BIAS_TILES = ATT_R + -(-(BUCKET_LO[-1] + ATT_TK - 1) // ATT_TK)
BIAS_ROLL_W = pl.next_power_of_2(ATT_TQ + ATT_TK)


def _lambda_init(layer):
    return 0.8 - 0.6 * math.exp(-0.3 * layer)


def _bias_of_distance(rb_ref, col, dist):
    val = jnp.full(dist.shape, rb_ref[0, col], F32)
    for b in range(1, NUM_BUCKETS):
        val = jnp.where(dist >= BUCKET_LO[b], rb_ref[b, col], val)
    return val


def _diff_bias_kernel(rb_ref, out_ref):
    h = pl.program_id(0)
    delta = (pl.program_id(1) - (ATT_R - 1)) * ATT_TK
    tq, tk = ATT_TQ, ATT_TK
    u = lax.broadcasted_iota(jnp.int32, (8, BIAS_ROLL_W), 1)
    d = u + (delta - tk)
    for m in range(2):
        val = _bias_of_distance(rb_ref, 2 * h + m, d) * LOG2E
        val = jnp.where(d < 0, NEG, val)
        x = jnp.broadcast_to(val[0:1, :], (tk, BIAS_ROLL_W))
        rolled = pltpu.roll(x, 0, 1, stride=1, stride_axis=0)
        out_ref[0, 0, :, m * tq:(m + 1) * tq] = rolled[:, tk:tk + tq]


def _diff_bias_tiles(rel_bias):
    tq, tk = ATT_TQ, ATT_TK
    return pl.pallas_call(
        _diff_bias_kernel,
        grid=(N_DIFF_HEADS, N_BIAS_TILES),
        in_specs=[pl.BlockSpec(memory_space=pltpu.SMEM)],
        out_specs=pl.BlockSpec((1, 1, tk, 2 * tq), lambda h, d: (h, d, 0, 0)),
        out_shape=jax.ShapeDtypeStruct((N_DIFF_HEADS, N_BIAS_TILES, tk, 2 * tq), F32),
        name="diff_bias_tiles",
    )(rel_bias)


def _dil_bias_kernel(rb_ref, out_ref):
    br = pl.program_id(0)
    w = DIL_W
    dil = jnp.where(br == 0, DILATED_CONFIGS[0][1],
                    jnp.where(br == 1, DILATED_CONFIGS[1][1], DILATED_CONFIGS[2][1]))
    u = lax.broadcasted_iota(jnp.int32, (8, 4 * w), 1)
    for h in range(N_DIL_HEADS):
        for kind in range(2):
            dist = (2 - kind) * w - u
            val = _bias_of_distance(rb_ref, 2 * N_DIFF_HEADS + h, dist * dil) * LOG2E
            val = jnp.where((dist >= 0) & (dist <= w), val, NEG)
            x = jnp.broadcast_to(val[0:1, :], (w, 4 * w))
            rolled = pltpu.roll(x, 0, 1, stride=1, stride_axis=0)
            out_ref[0, h // 2, kind, (h % 2) * w:(h % 2 + 1) * w, :] = rolled[:, w:3 * w]


def _dil_bias_tiles(rel_bias):
    w = DIL_W
    nb = len(DILATED_CONFIGS)
    return pl.pallas_call(
        _dil_bias_kernel,
        grid=(nb,),
        in_specs=[pl.BlockSpec(memory_space=pltpu.SMEM)],
        out_specs=pl.BlockSpec((1, N_DIL_HEADS // 2, 2, 2 * w, 2 * w), lambda b: (b, 0, 0, 0, 0)),
        out_shape=jax.ShapeDtypeStruct((nb, N_DIL_HEADS // 2, 2, 2 * w, 2 * w), F32),
        name="dil_bias_tiles",
    )(rel_bias)


def _rms(x, g):
    return x * lax.rsqrt(jnp.mean(x * x, axis=-1, keepdims=True) + EPS) * g


def _ffn_kernel(*refs, chunk, final, mixed):
    if mixed:
        x_ref, oa_ref, ob_ref, wo_ref, g_ref, wg_ref, wu_ref, wd_ref, fg_ref, o_ref, h_ref = refs
        x = (x_ref[...] + jnp.dot(oa_ref[...], wo_ref[0:DIFF_WIDTH, :], preferred_element_type=F32)
             + jnp.dot(ob_ref[...], wo_ref[DIFF_WIDTH:, :], preferred_element_type=F32))
    else:
        x_ref, g_ref, wg_ref, wu_ref, wd_ref, fg_ref, o_ref, h_ref = refs
        x = x_ref[...]
    xn = _rms(x, g_ref[...]).astype(BF16)
    for c in range(wg_ref.shape[1] // chunk):
        sl = slice(c * chunk, (c + 1) * chunk)
        gate = jnp.dot(xn, wg_ref[:, sl].astype(BF16), preferred_element_type=F32)
        up = jnp.dot(xn, wu_ref[:, sl].astype(BF16), preferred_element_type=F32)
        h_ref[:, sl] = (gate * jax.nn.sigmoid(gate) * up).astype(BF16)
    y = x + 0.5 * jnp.dot(h_ref[...], wd_ref[...], preferred_element_type=F32)
    if final:
        y = _rms(y, fg_ref[...])
    o_ref[...] = y


def _layer_spec(arr, layer):
    tail = arr.shape[1:]
    return pl.BlockSpec((None,) + tail, lambda i: (layer,) + (0,) * len(tail), pipeline_mode=pl.Buffered(1))


def _ffn(x, layer, gain, wg, wu, wd, final_gain, *, final, mix=None, tm=512, chunk=256):
    m, d = x.shape
    dff = wg.shape[2]
    row = lambda n: pl.BlockSpec((tm, n), lambda i: (i, 0))
    mix_specs = [] if mix is None else [row(mix[0].shape[1]), row(mix[1].shape[1]), _layer_spec(mix[2], layer)]
    return pl.pallas_call(
        functools.partial(_ffn_kernel, chunk=chunk, final=final, mixed=mix is not None),
        grid=(m // tm,),
        in_specs=[row(d)] + mix_specs + [_layer_spec(gain, layer), _layer_spec(wg, layer), _layer_spec(wu, layer),
                                         _layer_spec(wd, layer), _layer_spec(final_gain, 0)],
        out_specs=row(d),
        out_shape=jax.ShapeDtypeStruct((m, d), F32),
        scratch_shapes=[pltpu.VMEM((tm, dff), BF16)],
        compiler_params=pltpu.CompilerParams(
            dimension_semantics=("arbitrary",), vmem_limit_bytes=VMEM_LIMIT),
        name="ffn_mix" if mix is not None else "ffn",
    )(x, *(mix or ()), gain, wg, wu, wd, final_gain)


def _inproj_kernel(x_ref, g_ref, w_ref, o_ref, vt_ref, *, n_chunks, chunk):
    xn = _rms(x_ref[...], g_ref[...]).astype(BF16)
    hw = 2 * HEAD_DIM
    for c in range(n_chunks):
        sl = slice(c * chunk, (c + 1) * chunk)
        r = jnp.dot(xn, w_ref[:, sl], preferred_element_type=F32)
        o_ref[:, sl] = r.astype(BF16)
        for h in range(N_DIFF_HEADS):
            col = 2 * DIFF_WIDTH + h * hw
            if sl.start <= col < sl.stop:
                vt_ref[0, h] = r[:, col - sl.start:col - sl.start + hw].T.astype(BF16)


def _inproj(x, layer, gain, w, *, seq, tm=1024, chunk=512):
    m, d = x.shape
    n = w.shape[2]
    per_seq = seq // tm
    return pl.pallas_call(
        functools.partial(_inproj_kernel, n_chunks=n // chunk, chunk=chunk),
        grid=(m // tm,),
        in_specs=[pl.BlockSpec((tm, d), lambda i: (i, 0)), _layer_spec(gain, layer), _layer_spec(w, layer)],
        out_specs=[pl.BlockSpec((tm, n), lambda i: (i, 0)),
                   pl.BlockSpec((1, N_DIFF_HEADS, 2 * HEAD_DIM, tm), lambda i: (i // per_seq, 0, 0, i % per_seq))],
        out_shape=[jax.ShapeDtypeStruct((m, n), BF16),
                   jax.ShapeDtypeStruct((m // seq, N_DIFF_HEADS, 2 * HEAD_DIM, seq), BF16)],
        compiler_params=pltpu.CompilerParams(
            dimension_semantics=("arbitrary",), vmem_limit_bytes=VMEM_LIMIT),
        name="inproj",
    )(x, gain, w)


def _diff_attn_kernel(q_ref, k_ref, v_ref, bias_ref, lq1_ref, lk1_ref, lq2_ref, lk2_ref, sg_ref,
                      o_ref, vt_ref, qs_ref, acc_ref, m_ref, al_ref, mxa_ref, mxb_ref,
                      sa_ref, sb_ref, ea_ref, eb_ref, *, lam0, n_q):
    tq, tk = ATT_TQ, ATT_TK
    hw = 2 * HEAD_DIM

    ones_row = (lax.broadcasted_iota(jnp.int32, (ATT_PAD, tk), 0) == 0).astype(BF16)
    low = lax.broadcasted_iota(jnp.int32, (tq, hw), 1) < HEAD_DIM
    for c in range(n_q):
        blk = slice(c * tk, (c + 1) * tk)
        vt_ref[c, 0:hw, :] = v_ref[0, 0, :, blk]
        vt_ref[c, hw:, :] = ones_row
        qblk = q_ref[0, blk, :]
        qs_ref[c, 0:tq, :] = jnp.where(low, qblk, jnp.zeros_like(qblk))
        qs_ref[c, tq:, :] = jnp.where(low, jnp.zeros_like(qblk), qblk)

    cw = ATT_CW
    chunks = [slice(c * cw, (c + 1) * cw) for c in range(2 * tq // cw)]

    def key_rows(tile, cols):
        qb, j = tile
        return cw if (qb == j and cols.start % tq + cw <= tk // 2) else tk

    def far_bias(tile, cols):
        qb, j = tile
        return bias_ref[0, N_BIAS_TILES - 1, 0:1, cols] if qb - j >= N_BIAS_TILES - 1 else None

    def score_chunk(tile, cols, s_ref, mx_ref):
        qb, j = tile
        kr = key_rows(tile, cols)
        kblk = k_ref[0, j * tk:j * tk + kr, :]
        t = lax.dot_general(kblk, qs_ref[qb, cols, :], (((1,), (1,)), ((), ())), preferred_element_type=F32)
        const = far_bias(tile, cols)
        if const is None:
            t = t + bias_ref[0, qb - j, 0:kr, cols]
            mx_ref[:, cols] = jnp.max(t, axis=0, keepdims=True)
        else:
            mx_ref[:, cols] = jnp.max(t, axis=0, keepdims=True) + const
        s_ref[0:kr, cols] = t

    def pv_chunk(tile, cols, e_ref):
        qb, j = tile
        kr = key_rows(tile, cols)
        pv = jnp.dot(vt_ref[j, :, 0:kr], e_ref[0:kr, cols], preferred_element_type=F32)
        acc_ref[qb, :, cols] = pv if j == 0 else al_ref[:, cols] * acc_ref[qb, :, cols] + pv

    def softmax_chunk(tile, cols, s_ref, mx_ref, e_ref):
        kr = key_rows(tile, cols)
        if tile[1] == 0:
            m_new = mx_ref[:, cols]
        else:
            m_old = m_ref[:, cols]
            m_new = jnp.maximum(m_old, mx_ref[:, cols])
            al_ref[:, cols] = jnp.exp2(m_old - m_new)
        m_ref[:, cols] = m_new
        const = far_bias(tile, cols)
        shift = m_new if const is None else m_new - const
        e_ref[0:kr, cols] = jnp.exp2(s_ref[0:kr, cols] - shift).astype(BF16)

    tiles = [(qb, j) for qb in range(n_q) for j in range(qb + 1)]
    bufs = ((sa_ref, mxa_ref, ea_ref), (sb_ref, mxb_ref, eb_ref))
    for cols in chunks:
        score_chunk(tiles[0], cols, *bufs[0][:2])
    for t, tile in enumerate(tiles):
        cur, oth = bufs[t % 2], bufs[(t + 1) % 2]
        for cols in chunks:
            if t + 1 < len(tiles):
                score_chunk(tiles[t + 1], cols, *oth[:2])
            if t > 0:
                pv_chunk(tiles[t - 1], cols, oth[2])
            softmax_chunk(tile, cols, *cur)
    for cols in chunks:
        pv_chunk(tiles[-1], cols, bufs[(len(tiles) - 1) % 2][2])

    lam = (jnp.exp(jnp.sum(lq1_ref[...] * lk1_ref[...])) - jnp.exp(jnp.sum(lq2_ref[...] * lk2_ref[...])) + lam0)
    for c in range(n_q):
        o = acc_ref[c, 0:hw, :] / acc_ref[c, hw:hw + 1, :]
        o = o[:, :tq] - lam * o[:, tq:]
        y = o * lax.rsqrt(jnp.mean(o * o, axis=0, keepdims=True) + EPS) * sg_ref[...]
        y = y * (1.0 - lam0)
        o_ref[0, c * tq:(c + 1) * tq, :] = y.T.astype(BF16)


def _diff_attn(qkv, vt, bias_tiles, lq1, lk1, lq2, lk2, sub_gain, *, lam0):
    b, s, _ = qkv.shape
    tq, tk = ATT_TQ, ATT_TK
    hw = 2 * HEAD_DIM
    n_q = s // tq
    assert ATT_R == 1 and ATT_CW * 2 == ATT_TK
    small = lambda shape: pl.BlockSpec(shape, lambda h, bb: (0,) * len(shape))
    return pl.pallas_call(
        functools.partial(_diff_attn_kernel, lam0=lam0, n_q=n_q),
        grid=(N_DIFF_HEADS, b),
        in_specs=[
            pl.BlockSpec((1, s, hw), lambda h, bb: (bb, 0, h)),
            pl.BlockSpec((1, s, hw), lambda h, bb: (bb, 0, N_DIFF_HEADS + h)),
            pl.BlockSpec((1, 1, hw, s), lambda h, bb: (bb, h, 0, 0)),
            pl.BlockSpec((1, N_BIAS_TILES, tk, 2 * tq), lambda h, bb: (h, 0, 0, 0),
                         pipeline_mode=pl.Buffered(1)),
            small((1, HEAD_DIM)), small((1, HEAD_DIM)), small((1, HEAD_DIM)), small((1, HEAD_DIM)),
            small((hw, 1)),
        ],
        out_specs=pl.BlockSpec((1, s, hw), lambda h, bb: (bb, 0, h)),
        out_shape=jax.ShapeDtypeStruct((b, s, DIFF_WIDTH), BF16),
        scratch_shapes=[
            pltpu.VMEM((n_q, hw + ATT_PAD, tk), BF16),
            pltpu.VMEM((n_q, 2 * tq, hw), BF16),
            pltpu.VMEM((n_q, hw + ATT_PAD, 2 * tq), F32),
            pltpu.VMEM((1, 2 * tq), F32),
            pltpu.VMEM((1, 2 * tq), F32),
            pltpu.VMEM((1, 2 * tq), F32),
            pltpu.VMEM((1, 2 * tq), F32),
            pltpu.VMEM((tk, 2 * tq), F32),
            pltpu.VMEM((tk, 2 * tq), F32),
            pltpu.VMEM((tk, 2 * tq), BF16),
            pltpu.VMEM((tk, 2 * tq), BF16),
        ],
        compiler_params=pltpu.CompilerParams(
            dimension_semantics=("arbitrary", "arbitrary"), vmem_limit_bytes=VMEM_LIMIT),
        name="diff_attn",
    )(qkv, qkv, vt, bias_tiles, lq1, lk1, lq2, lk2, sub_gain)


def _dilated_kernel(q_ref, k_ref, v_ref, bias_ref, o_ref, xf_ref, x4_ref, nat_ref):
    w = DIL_W
    s_len = q_ref.shape[1]
    l4 = s_len // 4
    srcs = (q_ref, k_ref, v_ref)
    cp = 512
    for t in range(3):
        for c in range(s_len // cp):
            xf_ref[t, c * cp:(c + 1) * cp, :] = srcs[t][0, c * cp:(c + 1) * cp, :].astype(F32)
    for t in range(3):
        for r in range(4):
            for c in range(l4 // cp):
                x4_ref[t, r * l4 + c * cp:r * l4 + (c + 1) * cp, :] = (
                    xf_ref[t, pl.ds(r + 4 * c * cp, cp, stride=4), :])

    low = lax.broadcasted_iota(jnp.int32, (w, LANES), 1) < HEAD_DIM
    nt = (((1,), (1,)), ((), ()))

    def rows(t, br, r, j0, n):
        if br == 0:
            return srcs[t][0, pl.ds(pl.multiple_of(j0, w), n), :]
        if br == 1:
            return x4_ref[t, pl.ds(pl.multiple_of(r * l4 + j0, w), n), :].astype(BF16)
        return x4_ref[t, pl.ds((r % 4) * l4 + r // 4 + 4 * j0, n, stride=4), :].astype(BF16)

    def attend(br, r, n):
        k0 = jnp.maximum(n - 1, 0) * w
        kind = jnp.where(n == 0, 1, 0)
        qp = rows(0, br, r, n * w, w)
        kk = rows(1, br, r, k0, 2 * w)
        vv = jnp.concatenate([rows(2, br, r, k0, 2 * w), jnp.ones((2 * w, LANES), BF16)], axis=1)
        zero = jnp.zeros_like(qp)
        qm = jnp.concatenate([jnp.where(low, qp, zero), jnp.where(low, zero, qp)], axis=0)
        s = lax.dot_general(qm, kk, nt, preferred_element_type=F32) + bias_ref[br, 0, kind]
        m = jnp.max(s, axis=1, keepdims=True)
        e = jnp.exp2((s - m).astype(BF16))
        pv = jnp.dot(e, vv, preferred_element_type=F32)
        dil = DILATED_CONFIGS[br][1]
        start = r + dil * w * n
        dst = pl.ds(pl.multiple_of(start, w), w) if dil == 1 else pl.ds(start, w, stride=dil)
        nat_ref[br, 0, dst, :] = jnp.where(low, pv[:w, :LANES], pv[w:, :LANES])
        nat_ref[br, 1, dst, :] = jnp.where(low, m[:w], m[w:])
        nat_ref[br, 2, dst, :] = jnp.where(low, pv[:w, LANES:], pv[w:, LANES:])

    for br, (_, dil) in enumerate(DILATED_CONFIGS):
        nblk = s_len // dil // w

        def blocks(i, carry, br=br, nblk=nblk):
            for u in range(DIL_UNROLL):
                idx = i * DIL_UNROLL + u
                attend(br, idx // nblk, idx % nblk)
            return carry

        lax.fori_loop(0, dil * nblk // DIL_UNROLL, blocks, 0)

    for c in range(s_len // cp):
        sl = slice(c * cp, (c + 1) * cp)
        ms = [nat_ref[br, 1, sl, :] for br in range(3)]
        top = jnp.maximum(jnp.maximum(ms[0], ms[1]), ms[2])
        wts = [jnp.exp2(v - top) for v in ms]
        num = wts[0] * nat_ref[0, 0, sl, :] + wts[1] * nat_ref[1, 0, sl, :] + wts[2] * nat_ref[2, 0, sl, :]
        den = wts[0] * nat_ref[0, 2, sl, :] + wts[1] * nat_ref[1, 2, sl, :] + wts[2] * nat_ref[2, 2, sl, :]
        o_ref[0, sl, :] = (num / den).astype(BF16)


def _dilated_attn(qkv, bias):
    b, s, _ = qkv.shape
    w = DIL_W
    col0 = 3 * DIFF_WIDTH // LANES
    pairs = DIL_WIDTH // LANES
    part = lambda t: pl.BlockSpec((1, s, LANES), lambda bb, p: (bb, 0, col0 + t * pairs + p))
    return pl.pallas_call(
        _dilated_kernel,
        grid=(b, pairs),
        in_specs=[part(0), part(1), part(2),
                  pl.BlockSpec((len(DILATED_CONFIGS), 1, 2, 2 * w, 2 * w), lambda bb, p: (0, p, 0, 0, 0))],
        out_specs=pl.BlockSpec((1, s, LANES), lambda bb, p: (bb, 0, p)),
        out_shape=jax.ShapeDtypeStruct((b, s, DIL_WIDTH), BF16),
        scratch_shapes=[
            pltpu.VMEM((3, s, LANES), F32),
            pltpu.VMEM((3, s, LANES), F32),
            pltpu.VMEM((len(DILATED_CONFIGS), 3, s, LANES), F32),
        ],
        compiler_params=pltpu.CompilerParams(
            dimension_semantics=("arbitrary", "arbitrary"), vmem_limit_bytes=VMEM_LIMIT),
        name="dilated_attn",
    )(qkv, qkv, qkv, bias)


def kernel(x, ffn1_norm, ffn1_w_gate, ffn1_w_up, ffn1_w_down, mix_norm, w_in, lambda_q1, lambda_k1,
           lambda_q2, lambda_k2, subln_gain, w_out, ffn2_norm, ffn2_w_gate, ffn2_w_up, ffn2_w_down,
           rel_bias, final_norm):
    b, s, d = x.shape
    depth = w_in.shape[0]
    m = b * s
    diff_bias = _diff_bias_tiles(rel_bias)
    dil_bias = _dil_bias_tiles(rel_bias)
    qscale = jnp.concatenate([
        jnp.full((DIFF_WIDTH,), HEAD_DIM ** -0.5 * LOG2E, F32), jnp.ones((2 * DIFF_WIDTH,), F32),
        jnp.full((DIL_WIDTH,), HEAD_DIM ** -0.5 * LOG2E, F32), jnp.ones((2 * DIL_WIDTH,), F32)])
    gains = lambda g: g.reshape(depth, 1, d)
    w_in16 = (w_in * qscale).astype(BF16)
    w_out16 = w_out.astype(BF16)
    wd1, wd2 = ffn1_w_down.astype(BF16), ffn2_w_down.astype(BF16)
    fin = final_norm.reshape(1, 1, d)
    h = x.reshape(m, d)
    for layer in range(depth):
        row = lambda v: v[layer].reshape(1, -1)
        h = _ffn(h, layer, gains(ffn1_norm), ffn1_w_gate, ffn1_w_up, wd1, fin, final=False)
        qkv, vt = _inproj(h, layer, gains(mix_norm), w_in16, seq=s)
        qkv = qkv.reshape(b, s, -1)
        oa = _diff_attn(qkv, vt, diff_bias, row(lambda_q1), row(lambda_k1), row(lambda_q2), row(lambda_k2),
                        subln_gain[layer].reshape(-1, 1), lam0=_lambda_init(layer))
        ob = _dilated_attn(qkv, dil_bias)
        h = _ffn(h, layer, gains(ffn2_norm), ffn2_w_gate, ffn2_w_up, wd2, fin, final=layer == depth - 1,
                 mix=(oa.reshape(m, DIFF_WIDTH), ob.reshape(m, DIL_WIDTH), w_out16))
    return h.reshape(b, s, d)
```

```python
import functools
import math

import numpy as np
import jax
import jax.numpy as jnp
from jax import lax
from jax.experimental import pallas as pl
from jax.experimental.pallas import tpu as pltpu

HEAD_DIM = 64
N_DIFF_HEADS = 4
N_DIL_HEADS = 8
DIFF_WIDTH = 512
DIL_WIDTH = 512
DILATED_CONFIGS = ((128, 1), (512, 4), (2048, 16))
NUM_BUCKETS = 32
MAX_DISTANCE = 2048
EPS = 1e-6
NEG = -1e30

LANES = 128
DIL_W = 128
DIL_UNROLL = 32
ATT_TQ = 512
ATT_TK = 512
ATT_R = ATT_TQ // ATT_TK
ATT_PAD = 16
ATT_CW = 256
VMEM_LIMIT = 56 * 1024 * 1024
LOG2E = math.log2(math.e)

F32 = jnp.float32
BF16 = jnp.bfloat16


def _bucket_lower_bounds():
    d = np.arange(0, 2 * MAX_DISTANCE + 1)
    max_exact = NUM_BUCKETS // 2
    v = np.log(np.maximum(d, 1) / max_exact) / math.log(MAX_DISTANCE / max_exact) * (NUM_BUCKETS - max_exact)
    b = np.where(d < max_exact, d, np.minimum(max_exact + v.astype(np.int64), NUM_BUCKETS - 1))
    return [int(np.argmax(b == k)) for k in range(NUM_BUCKETS)]


BUCKET_LO = _bucket_lower_bounds()
N_BIAS_TILES = ATT_R + -(-(BUCKET_LO[-1] + ATT_TK - 1) // ATT_TK)
BIAS_ROLL_W = pl.next_power_of_2(ATT_TQ + ATT_TK)


def _lambda_init(layer):
    return 0.8 - 0.6 * math.exp(-0.3 * layer)


def _bias_of_distance(rb_ref, col, dist):
    val = jnp.full(dist.shape, rb_ref[0, col], F32)
    for b in range(1, NUM_BUCKETS):
        val = jnp.where(dist >= BUCKET_LO[b], rb_ref[b, col], val)
    return val


def _diff_bias_kernel(rb_ref, out_ref):
    h = pl.program_id(0)
    delta = (pl.program_id(1) - (ATT_R - 1)) * ATT_TK
    tq, tk = ATT_TQ, ATT_TK
    u = lax.broadcasted_iota(jnp.int32, (8, BIAS_ROLL_W), 1)
    d = u + (delta - tk)
    for m in range(2):
        val = _bias_of_distance(rb_ref, 2 * h + m, d) * LOG2E
        val = jnp.where(d < 0, NEG, val)
        x = jnp.broadcast_to(val[0:1, :], (tk, BIAS_ROLL_W))
        rolled = pltpu.roll(x, 0, 1, stride=1, stride_axis=0)
        out_ref[0, 0, :, m * tq:(m + 1) * tq] = rolled[:, tk:tk + tq]


def _diff_bias_tiles(rel_bias):
    tq, tk = ATT_TQ, ATT_TK
    return pl.pallas_call(
        _diff_bias_kernel,
        grid=(N_DIFF_HEADS, N_BIAS_TILES),
        in_specs=[pl.BlockSpec(memory_space=pltpu.SMEM)],
        out_specs=pl.BlockSpec((1, 1, tk, 2 * tq), lambda h, d: (h, d, 0, 0)),
        out_shape=jax.ShapeDtypeStruct((N_DIFF_HEADS, N_BIAS_TILES, tk, 2 * tq), F32),
        name="diff_bias_tiles",
    )(rel_bias)


def _dil_bias_kernel(rb_ref, out_ref):
    br = pl.program_id(0)
    w = DIL_W
    dil = jnp.where(br == 0, DILATED_CONFIGS[0][1],
                    jnp.where(br == 1, DILATED_CONFIGS[1][1], DILATED_CONFIGS[2][1]))
    u = lax.broadcasted_iota(jnp.int32, (8, 4 * w), 1)
    for h in range(N_DIL_HEADS):
        for kind in range(2):
            dist = (2 - kind) * w - u
            val = _bias_of_distance(rb_ref, 2 * N_DIFF_HEADS + h, dist * dil) * LOG2E
            val = jnp.where((dist >= 0) & (dist <= w), val, NEG)
            x = jnp.broadcast_to(val[0:1, :], (w, 4 * w))
            rolled = pltpu.roll(x, 0, 1, stride=1, stride_axis=0)
            out_ref[0, h // 2, kind, (h % 2) * w:(h % 2 + 1) * w, :] = rolled[:, w:3 * w]


def _dil_bias_tiles(rel_bias):
    w = DIL_W
    nb = len(DILATED_CONFIGS)
    return pl.pallas_call(
        _dil_bias_kernel,
        grid=(nb,),
        in_specs=[pl.BlockSpec(memory_space=pltpu.SMEM)],
        out_specs=pl.BlockSpec((1, N_DIL_HEADS // 2, 2, 2 * w, 2 * w), lambda b: (b, 0, 0, 0, 0)),
        out_shape=jax.ShapeDtypeStruct((nb, N_DIL_HEADS // 2, 2, 2 * w, 2 * w), F32),
        name="dil_bias_tiles",
    )(rel_bias)


def _rms(x, g):
    return x * lax.rsqrt(jnp.mean(x * x, axis=-1, keepdims=True) + EPS) * g


def _ffn_kernel(*refs, chunk, final, mixed):
    if mixed:
        x_ref, oa_ref, ob_ref, wo_ref, g_ref, wg_ref, wu_ref, wd_ref, fg_ref, o_ref, h_ref = refs
        x = (x_ref[...] + jnp.dot(oa_ref[...], wo_ref[0:DIFF_WIDTH, :], preferred_element_type=F32)
             + jnp.dot(ob_ref[...], wo_ref[DIFF_WIDTH:, :], preferred_element_type=F32))
    else:
        x_ref, g_ref, wg_ref, wu_ref, wd_ref, fg_ref, o_ref, h_ref = refs
        x = x_ref[...]
    xn = _rms(x, g_ref[...]).astype(BF16)
    for c in range(wg_ref.shape[1] // chunk):
        sl = slice(c * chunk, (c + 1) * chunk)
        gate = jnp.dot(xn, wg_ref[:, sl].astype(BF16), preferred_element_type=F32)
        up = jnp.dot(xn, wu_ref[:, sl].astype(BF16), preferred_element_type=F32)
        h_ref[:, sl] = (gate * jax.nn.sigmoid(gate) * up).astype(BF16)
    y = x + 0.5 * jnp.dot(h_ref[...], wd_ref[...], preferred_element_type=F32)
    if final:
        y = _rms(y, fg_ref[...])
    o_ref[...] = y


def _layer_spec(arr, layer):
    tail = arr.shape[1:]
    return pl.BlockSpec((None,) + tail, lambda i: (layer,) + (0,) * len(tail), pipeline_mode=pl.Buffered(1))


def _ffn(x, layer, gain, wg, wu, wd, final_gain, *, final, mix=None, tm=512, chunk=256):
    m, d = x.shape
    dff = wg.shape[2]
    row = lambda n: pl.BlockSpec((tm, n), lambda i: (i, 0))
    mix_specs = [] if mix is None else [row(mix[0].shape[1]), row(mix[1].shape[1]), _layer_spec(mix[2], layer)]
    return pl.pallas_call(
        functools.partial(_ffn_kernel, chunk=chunk, final=final, mixed=mix is not None),
        grid=(m // tm,),
        in_specs=[row(d)] + mix_specs + [_layer_spec(gain, layer), _layer_spec(wg, layer), _layer_spec(wu, layer),
                                         _layer_spec(wd, layer), _layer_spec(final_gain, 0)],
        out_specs=row(d),
        out_shape=jax.ShapeDtypeStruct((m, d), F32),
        scratch_shapes=[pltpu.VMEM((tm, dff), BF16)],
        compiler_params=pltpu.CompilerParams(
            dimension_semantics=("arbitrary",), vmem_limit_bytes=VMEM_LIMIT),
        name="ffn_mix" if mix is not None else "ffn",
    )(x, *(mix or ()), gain, wg, wu, wd, final_gain)


def _inproj_kernel(x_ref, g_ref, w_ref, o_ref, vt_ref, *, n_chunks, chunk):
    xn = _rms(x_ref[...], g_ref[...]).astype(BF16)
    hw = 2 * HEAD_DIM
    for c in range(n_chunks):
        sl = slice(c * chunk, (c + 1) * chunk)
        r = jnp.dot(xn, w_ref[:, sl], preferred_element_type=F32)
        o_ref[:, sl] = r.astype(BF16)
        for h in range(N_DIFF_HEADS):
            col = 2 * DIFF_WIDTH + h * hw
            if sl.start <= col < sl.stop:
                vt_ref[0, h] = r[:, col - sl.start:col - sl.start + hw].T.astype(BF16)


def _inproj(x, layer, gain, w, *, seq, tm=1024, chunk=512):
    m, d = x.shape
    n = w.shape[2]
    per_seq = seq // tm
    return pl.pallas_call(
        functools.partial(_inproj_kernel, n_chunks=n // chunk, chunk=chunk),
        grid=(m // tm,),
        in_specs=[pl.BlockSpec((tm, d), lambda i: (i, 0)), _layer_spec(gain, layer), _layer_spec(w, layer)],
        out_specs=[pl.BlockSpec((tm, n), lambda i: (i, 0)),
                   pl.BlockSpec((1, N_DIFF_HEADS, 2 * HEAD_DIM, tm), lambda i: (i // per_seq, 0, 0, i % per_seq))],
        out_shape=[jax.ShapeDtypeStruct((m, n), BF16),
                   jax.ShapeDtypeStruct((m // seq, N_DIFF_HEADS, 2 * HEAD_DIM, seq), BF16)],
        compiler_params=pltpu.CompilerParams(
            dimension_semantics=("arbitrary",), vmem_limit_bytes=VMEM_LIMIT),
        name="inproj",
    )(x, gain, w)


def _diff_attn_kernel(q_ref, k_ref, v_ref, bias_ref, lq1_ref, lk1_ref, lq2_ref, lk2_ref, sg_ref,
                      o_ref, vt_ref, qs_ref, acc_ref, m_ref, al_ref, mxa_ref, mxb_ref,
                      sa_ref, sb_ref, ea_ref, eb_ref, *, lam0, n_q):
    tq, tk = ATT_TQ, ATT_TK
    hw = 2 * HEAD_DIM

    ones_row = (lax.broadcasted_iota(jnp.int32, (ATT_PAD, tk), 0) == 0).astype(BF16)
    low = lax.broadcasted_iota(jnp.int32, (tq, hw), 1) < HEAD_DIM
    for c in range(n_q):
        blk = slice(c * tk, (c + 1) * tk)
        vt_ref[c, 0:hw, :] = v_ref[0, 0, :, blk]
        vt_ref[c, hw:, :] = ones_row
        qblk = q_ref[0, blk, :]
        qs_ref[c, 0:tq, :] = jnp.where(low, qblk, jnp.zeros_like(qblk))
        qs_ref[c, tq:, :] = jnp.where(low, jnp.zeros_like(qblk), qblk)

    cw = ATT_CW
    chunks = [slice(c * cw, (c + 1) * cw) for c in range(2 * tq // cw)]

    def key_rows(tile, cols):
        qb, j = tile
        return cw if (qb == j and cols.start % tq + cw <= tk // 2) else tk

    def far_bias(tile, cols):
        qb, j = tile
        return bias_ref[0, N_BIAS_TILES - 1, 0:1, cols] if qb - j >= N_BIAS_TILES - 1 else None

    def score_chunk(tile, cols, s_ref, mx_ref):
        qb, j = tile
        kr = key_rows(tile, cols)
        kblk = k_ref[0, j * tk:j * tk + kr, :]
        t = lax.dot_general(kblk, qs_ref[qb, cols, :], (((1,), (1,)), ((), ())), preferred_element_type=F32)
        const = far_bias(tile, cols)
        if const is None:
            t = t + bias_ref[0, qb - j, 0:kr, cols]
            mx_ref[:, cols] = jnp.max(t, axis=0, keepdims=True)
        else:
            mx_ref[:, cols] = jnp.max(t, axis=0, keepdims=True) + const
        s_ref[0:kr, cols] = t

    def pv_chunk(tile, cols, e_ref):
        qb, j = tile
        kr = key_rows(tile, cols)
        pv = jnp.dot(vt_ref[j, :, 0:kr], e_ref[0:kr, cols], preferred_element_type=F32)
        acc_ref[qb, :, cols] = pv if j == 0 else al_ref[:, cols] * acc_ref[qb, :, cols] + pv

    def softmax_chunk(tile, cols, s_ref, mx_ref, e_ref):
        kr = key_rows(tile, cols)
        if tile[1] == 0:
            m_new = mx_ref[:, cols]
        else:
            m_old = m_ref[:, cols]
            m_new = jnp.maximum(m_old, mx_ref[:, cols])
            al_ref[:, cols] = jnp.exp2(m_old - m_new)
        m_ref[:, cols] = m_new
        const = far_bias(tile, cols)
        shift = m_new if const is None else m_new - const
        e_ref[0:kr, cols] = jnp.exp2(s_ref[0:kr, cols] - shift).astype(BF16)

    tiles = [(qb, j) for qb in range(n_q) for j in range(qb + 1)]
    bufs = ((sa_ref, mxa_ref, ea_ref), (sb_ref, mxb_ref, eb_ref))
    for cols in chunks:
        score_chunk(tiles[0], cols, *bufs[0][:2])
    for t, tile in enumerate(tiles):
        cur, oth = bufs[t % 2], bufs[(t + 1) % 2]
        for cols in chunks:
            if t + 1 < len(tiles):
                score_chunk(tiles[t + 1], cols, *oth[:2])
            if t > 0:
                pv_chunk(tiles[t - 1], cols, oth[2])
            softmax_chunk(tile, cols, *cur)
    for cols in chunks:
        pv_chunk(tiles[-1], cols, bufs[(len(tiles) - 1) % 2][2])

    lam = (jnp.exp(jnp.sum(lq1_ref[...] * lk1_ref[...])) - jnp.exp(jnp.sum(lq2_ref[...] * lk2_ref[...])) + lam0)
    for c in range(n_q):
        o = acc_ref[c, 0:hw, :] / acc_ref[c, hw:hw + 1, :]
        o = o[:, :tq] - lam * o[:, tq:]
        y = o * lax.rsqrt(jnp.mean(o * o, axis=0, keepdims=True) + EPS) * sg_ref[...]
        y = y * (1.0 - lam0)
        o_ref[0, c * tq:(c + 1) * tq, :] = y.T.astype(BF16)


def _diff_attn(qkv, vt, bias_tiles, lq1, lk1, lq2, lk2, sub_gain, *, lam0):
    b, s, _ = qkv.shape
    tq, tk = ATT_TQ, ATT_TK
    hw = 2 * HEAD_DIM
    n_q = s // tq
    assert ATT_R == 1 and ATT_CW * 2 == ATT_TK
    small = lambda shape: pl.BlockSpec(shape, lambda h, bb: (0,) * len(shape))
    return pl.pallas_call(
        functools.partial(_diff_attn_kernel, lam0=lam0, n_q=n_q),
        grid=(N_DIFF_HEADS, b),
        in_specs=[
            pl.BlockSpec((1, s, hw), lambda h, bb: (bb, 0, h)),
            pl.BlockSpec((1, s, hw), lambda h, bb: (bb, 0, N_DIFF_HEADS + h)),
            pl.BlockSpec((1, 1, hw, s), lambda h, bb: (bb, h, 0, 0)),
            pl.BlockSpec((1, N_BIAS_TILES, tk, 2 * tq), lambda h, bb: (h, 0, 0, 0),
                         pipeline_mode=pl.Buffered(1)),
            small((1, HEAD_DIM)), small((1, HEAD_DIM)), small((1, HEAD_DIM)), small((1, HEAD_DIM)),
            small((hw, 1)),
        ],
        out_specs=pl.BlockSpec((1, s, hw), lambda h, bb: (bb, 0, h)),
        out_shape=jax.ShapeDtypeStruct((b, s, DIFF_WIDTH), BF16),
        scratch_shapes=[
            pltpu.VMEM((n_q, hw + ATT_PAD, tk), BF16),
            pltpu.VMEM((n_q, 2 * tq, hw), BF16),
            pltpu.VMEM((n_q, hw + ATT_PAD, 2 * tq), F32),
            pltpu.VMEM((1, 2 * tq), F32),
            pltpu.VMEM((1, 2 * tq), F32),
            pltpu.VMEM((1, 2 * tq), F32),
            pltpu.VMEM((1, 2 * tq), F32),
            pltpu.VMEM((tk, 2 * tq), F32),
            pltpu.VMEM((tk, 2 * tq), F32),
            pltpu.VMEM((tk, 2 * tq), BF16),
            pltpu.VMEM((tk, 2 * tq), BF16),
        ],
        compiler_params=pltpu.CompilerParams(
            dimension_semantics=("arbitrary", "arbitrary"), vmem_limit_bytes=VMEM_LIMIT),
        name="diff_attn",
    )(qkv, qkv, vt, bias_tiles, lq1, lk1, lq2, lk2, sub_gain)


def _dilated_kernel(q_ref, k_ref, v_ref, bias_ref, o_ref, xf_ref, x4_ref, nat_ref):
    w = DIL_W
    s_len = q_ref.shape[1]
    l4 = s_len // 4
    srcs = (q_ref, k_ref, v_ref)
    cp = 512
    for t in range(3):
        for c in range(s_len // cp):
            xf_ref[t, c * cp:(c + 1) * cp, :] = srcs[t][0, c * cp:(c + 1) * cp, :].astype(F32)
    for t in range(3):
        for r in range(4):
            for c in range(l4 // cp):
                x4_ref[t, r * l4 + c * cp:r * l4 + (c + 1) * cp, :] = (
                    xf_ref[t, pl.ds(r + 4 * c * cp, cp, stride=4), :])

    low = lax.broadcasted_iota(jnp.int32, (w, LANES), 1) < HEAD_DIM
    nt = (((1,), (1,)), ((), ()))

    def rows(t, br, r, j0, n):
        if br == 0:
            return srcs[t][0, pl.ds(pl.multiple_of(j0, w), n), :]
        if br == 1:
            return x4_ref[t, pl.ds(pl.multiple_of(r * l4 + j0, w), n), :].astype(BF16)
        return x4_ref[t, pl.ds((r % 4) * l4 + r // 4 + 4 * j0, n, stride=4), :].astype(BF16)

    def attend(br, r, n):
        k0 = jnp.maximum(n - 1, 0) * w
        kind = jnp.where(n == 0, 1, 0)
        qp = rows(0, br, r, n * w, w)
        kk = rows(1, br, r, k0, 2 * w)
        vv = jnp.concatenate([rows(2, br, r, k0, 2 * w), jnp.ones((2 * w, LANES), BF16)], axis=1)
        zero = jnp.zeros_like(qp)
        qm = jnp.concatenate([jnp.where(low, qp, zero), jnp.where(low, zero, qp)], axis=0)
        s = lax.dot_general(qm, kk, nt, preferred_element_type=F32) + bias_ref[br, 0, kind]
        m = jnp.max(s, axis=1, keepdims=True)
        e = jnp.exp2((s - m).astype(BF16))
        pv = jnp.dot(e, vv, preferred_element_type=F32)
        dil = DILATED_CONFIGS[br][1]
        start = r + dil * w * n
        dst = pl.ds(pl.multiple_of(start, w), w) if dil == 1 else pl.ds(start, w, stride=dil)
        nat_ref[br, 0, dst, :] = jnp.where(low, pv[:w, :LANES], pv[w:, :LANES])
        nat_ref[br, 1, dst, :] = jnp.where(low, m[:w], m[w:])
        nat_ref[br, 2, dst, :] = jnp.where(low, pv[:w, LANES:], pv[w:, LANES:])

    for br, (_, dil) in enumerate(DILATED_CONFIGS):
        nblk = s_len // dil // w

        def blocks(i, carry, br=br, nblk=nblk):
            for u in range(DIL_UNROLL):
                idx = i * DIL_UNROLL + u
                attend(br, idx // nblk, idx % nblk)
            return carry

        lax.fori_loop(0, dil * nblk // DIL_UNROLL, blocks, 0)

    for c in range(s_len // cp):
        sl = slice(c * cp, (c + 1) * cp)
        ms = [nat_ref[br, 1, sl, :] for br in range(3)]
        top = jnp.maximum(jnp.maximum(ms[0], ms[1]), ms[2])
        wts = [jnp.exp2(v - top) for v in ms]
        num = wts[0] * nat_ref[0, 0, sl, :] + wts[1] * nat_ref[1, 0, sl, :] + wts[2] * nat_ref[2, 0, sl, :]
        den = wts[0] * nat_ref[0, 2, sl, :] + wts[1] * nat_ref[1, 2, sl, :] + wts[2] * nat_ref[2, 2, sl, :]
        o_ref[0, sl, :] = (num / den).astype(BF16)


def _dilated_attn(qkv, bias):
    b, s, _ = qkv.shape
    w = DIL_W
    col0 = 3 * DIFF_WIDTH // LANES
    pairs = DIL_WIDTH // LANES
    part = lambda t: pl.BlockSpec((1, s, LANES), lambda bb, p: (bb, 0, col0 + t * pairs + p))
    return pl.pallas_call(
        _dilated_kernel,
        grid=(b, pairs),
        in_specs=[part(0), part(1), part(2),
                  pl.BlockSpec((len(DILATED_CONFIGS), 1, 2, 2 * w, 2 * w), lambda bb, p: (0, p, 0, 0, 0))],
        out_specs=pl.BlockSpec((1, s, LANES), lambda bb, p: (bb, 0, p)),
        out_shape=jax.ShapeDtypeStruct((b, s, DIL_WIDTH), BF16),
        scratch_shapes=[
            pltpu.VMEM((3, s, LANES), F32),
            pltpu.VMEM((3, s, LANES), F32),
            pltpu.VMEM((len(DILATED_CONFIGS), 3, s, LANES), F32),
        ],
        compiler_params=pltpu.CompilerParams(
            dimension_semantics=("arbitrary", "arbitrary"), vmem_limit_bytes=VMEM_LIMIT),
        name="dilated_attn",
    )(qkv, qkv, qkv, bias)


def kernel(x, ffn1_norm, ffn1_w_gate, ffn1_w_up, ffn1_w_down, mix_norm, w_in, lambda_q1, lambda_k1,
           lambda_q2, lambda_k2, subln_gain, w_out, ffn2_norm, ffn2_w_gate, ffn2_w_up, ffn2_w_down,
           rel_bias, final_norm):
    b, s, d = x.shape
    depth = w_in.shape[0]
    m = b * s
    diff_bias = _diff_bias_tiles(rel_bias)
    dil_bias = _dil_bias_tiles(rel_bias)
    qscale = jnp.concatenate([
        jnp.full((DIFF_WIDTH,), HEAD_DIM ** -0.5 * LOG2E, F32), jnp.ones((2 * DIFF_WIDTH,), F32),
        jnp.full((DIL_WIDTH,), HEAD_DIM ** -0.5 * LOG2E, F32), jnp.ones((2 * DIL_WIDTH,), F32)])
    gains = lambda g: g.reshape(depth, 1, d)
    w_in16 = (w_in * qscale).astype(BF16)
    w_out16 = w_out.astype(BF16)
    wd1, wd2 = ffn1_w_down.astype(BF16), ffn2_w_down.astype(BF16)
    fin = final_norm.reshape(1, 1, d)
    h = x.reshape(m, d)
    for layer in range(depth):
        row = lambda v: v[layer].reshape(1, -1)
        h = _ffn(h, layer, gains(ffn1_norm), ffn1_w_gate, ffn1_w_up, wd1, fin, final=False)
        qkv, vt = _inproj(h, layer, gains(mix_norm), w_in16, seq=s)
        qkv = qkv.reshape(b, s, -1)
        oa = _diff_attn(qkv, vt, diff_bias, row(lambda_q1), row(lambda_k1), row(lambda_q2), row(lambda_k2),
                        subln_gain[layer].reshape(-1, 1), lam0=_lambda_init(layer))
        ob = _dilated_attn(qkv, dil_bias)
        h = _ffn(h, layer, gains(ffn2_norm), ffn2_w_gate, ffn2_w_up, wd2, fin, final=layer == depth - 1,
                 mix=(oa.reshape(m, DIFF_WIDTH), ob.reshape(m, DIL_WIDTH), w_out16))
    return h.reshape(b, s, d)
```
